```python
import math
import jax, jax.numpy as jnp
from jax import lax
import numpy as np

D_MODEL = 2048
BATCH = 4
SEQ = 2048
DEPTH = 4
DEC_BATCH = 128
DEC_SEQ = 4
PAST_LEN = 16384
PAGE_SIZE = 128

N_MIXERS = 4
N_RWKV = (DEPTH + 3) // 4
N_MLSTM = (DEPTH + 2) // 4
N_CONF = (DEPTH + 1) // 4
N_SCONV = DEPTH // 4

D_FF = 5632
NORM_EPS = 1e-6

RW_HEAD = 64
RW_HEADS = D_MODEL // RW_HEAD
RW_DECAY_LORA = max(32, int(round(1.8 * D_MODEL ** 0.5 / 32)) * 32)
RW_AAA_LORA = max(32, int(round(1.8 * D_MODEL ** 0.5 / 32)) * 32)
RW_GATE_LORA = max(32, int(round(0.6 * D_MODEL ** 0.8 / 32)) * 32)
RW_GN_EPS = 64e-5

ML_HEADS = 8
ML_DV = D_MODEL // ML_HEADS
ML_DK = ML_DV // 2
ML_CHUNK = 64
ML_GATE_CAP = 15.0
ML_NORM_EPS = 1e-6
ML_PROJ = 2 * ML_HEADS * ML_DK + 2 * ML_HEADS * ML_DV + 2 * ML_HEADS

CF_KERNEL = 31
CF_LN_EPS = 1e-5

SC_KERNEL = 3

RW_KEYS = ('rw_mu', 'rw_w_rkv', 'rw_w0', 'rw_w1', 'rw_w2', 'rw_a0', 'rw_a1', 'rw_a2', 'rw_g1', 'rw_g2',
           'rw_k_k', 'rw_k_a', 'rw_r_k', 'rw_ln_g', 'rw_ln_b', 'rw_w_o')
ML_KEYS = ('ml_w_in', 'ml_b_gates', 'ml_norm_g', 'ml_w_o')
CF_KEYS = ('cf_w_pw1', 'cf_b_pw1', 'cf_w_dw', 'cf_b_dw', 'cf_ln_g', 'cf_ln_b', 'cf_w_pw2', 'cf_b_pw2')
SC_KEYS = ('sc_w_in', 'sc_w_dw', 'sc_w_out')

kernel_name = 'hybrid_rwkv7_mlstm_conformer_shortconv_step'

F32 = jnp.float32


def rms_norm(x, g, eps=NORM_EPS):
    xf = x.astype(F32)
    y = xf * lax.rsqrt(jnp.mean(xf * xf, axis=-1, keepdims=True) + eps)
    return (y * g.astype(F32)).astype(x.dtype)


def layer_norm(x, g, b, eps):
    xf = x.astype(F32)
    mu = jnp.mean(xf, axis=-1, keepdims=True)
    var = jnp.mean(jnp.square(xf - mu), axis=-1, keepdims=True)
    return ((xf - mu) * lax.rsqrt(var + eps) * g.astype(F32) + b.astype(F32)).astype(x.dtype)


def swiglu(x, w_gu, w_down):
    g, u = jnp.split(x @ w_gu, 2, axis=-1)
    return (jax.nn.silu(g) * u) @ w_down


def causal_depthwise_conv(u, buf, w):
    K = w.shape[0]
    ext = jnp.concatenate([buf.astype(u.dtype), u], axis=1)
    y = lax.conv_general_dilated(ext, w[:, None, :].astype(u.dtype), window_strides=(1,), padding='VALID',
                                 dimension_numbers=('NWC', 'WIO', 'NWC'), feature_group_count=u.shape[-1])
    return y, ext[:, ext.shape[1] - (K - 1):]


def rwkv7_time_mix(x, shift_prev, wkv0, mu, w_rkv, w0, w1, w2, a0, a1, a2, g1, g2, k_k, k_a, r_k, ln_g, ln_b, w_o):
    Bn, T, D = x.shape
    Hh, Nh = RW_HEADS, RW_HEAD
    x_prev = jnp.concatenate([shift_prev[:, None, :].astype(x.dtype), x[:, :-1]], axis=1)
    xmix = x[None] + (x_prev - x)[None] * mu[:, None, None, :]
    xr, xw, xk, xv, xa, xg = (xmix[i] for i in range(6))
    r = xr @ w_rkv[0]
    k = xk @ w_rkv[1]
    v = xv @ w_rkv[2]
    logw = -jax.nn.softplus(-(w0 + jnp.tanh(xw @ w1) @ w2).astype(F32)) - 0.5
    decay = jnp.exp(-jnp.exp(logw))
    a = jax.nn.sigmoid((a0 + (xa @ a1) @ a2).astype(F32))
    g = jax.nn.sigmoid(xg @ g1) @ g2
    heads = lambda t: t.astype(F32).reshape(Bn, T, Hh, Nh)
    kk = heads(k * k_k)
    kk = kk / jnp.maximum(jnp.sqrt(jnp.sum(kk * kk, axis=-1, keepdims=True)), 1e-12)
    k_mod = k.astype(F32) * (1.0 + (a - 1.0) * k_a.astype(F32))
    r_h, k_h, v_h, a_h, d_h = heads(r), heads(k_mod), heads(v), heads(a), heads(decay)

    def step(S, inp):
        r_t, d_t, k_t, v_t, kk_t, a_t = inp
        S = (S * d_t[:, :, None, :]
             - jnp.einsum('bhvk,bhk->bhv', S, kk_t)[..., None] * (kk_t * a_t)[:, :, None, :]
             + v_t[..., None] * k_t[:, :, None, :])
        return S, jnp.einsum('bhvk,bhk->bhv', S, r_t)

    tmaj = lambda t: jnp.swapaxes(t, 0, 1)
    S, y = lax.scan(step, wkv0.astype(F32), tuple(tmaj(t) for t in (r_h, d_h, k_h, v_h, kk, a_h)))
    y = tmaj(y)
    mean = jnp.mean(y, axis=-1, keepdims=True)
    var = jnp.mean(jnp.square(y - mean), axis=-1, keepdims=True)
    y = ((y - mean) * lax.rsqrt(var + RW_GN_EPS)).reshape(Bn, T, D) * ln_g.astype(F32) + ln_b.astype(F32)
    bonus = jnp.sum(r_h * k_h * r_k.astype(F32).reshape(Hh, Nh), axis=-1, keepdims=True) * v_h
    y = y + bonus.reshape(Bn, T, D)
    out = (y.astype(x.dtype) * g) @ w_o
    return out, S.astype(wkv0.dtype), x[:, -1]


def mlstm_chunkwise(q, k, v, log_i, log_f, C0, n0, m0):
    Bn, Hh, T, _ = q.shape
    L = math.gcd(T, ML_CHUNK)
    NC = T // L
    to_chunks = lambda t: jnp.moveaxis(t.reshape(Bn, Hh, NC, L, *t.shape[3:]), 2, 0)
    causal = jnp.tril(jnp.ones((L, L), dtype=bool))

    def step(carry, inp):
        C, n, m = carry
        q_, k_, v_, li, lf = inp
        b = jnp.cumsum(lf, axis=-1)
        dmat = jnp.where(causal, b[..., :, None] - b[..., None, :] + li[..., None, :], -jnp.inf)
        inter = b + m[..., None]
        m_t = jnp.maximum(jnp.max(dmat, axis=-1), inter)
        s = jnp.einsum('bhtd,bhsd->bhts', q_, k_) * jnp.exp(dmat - m_t[..., None])
        w_inter = jnp.exp(inter - m_t)
        num = jnp.einsum('bhts,bhsv->bhtv', s, v_) + w_inter[..., None] * jnp.einsum('bhvd,bhtd->bhtv', C, q_)
        den = jnp.sum(s, axis=-1) + w_inter * jnp.einsum('bhd,bhtd->bht', n, q_)
        h = num / jnp.maximum(jnp.abs(den), jnp.exp(-m_t))[..., None]
        bL = b[..., -1]
        ws_log = bL[..., None] - b + li
        m_new = jnp.maximum(bL + m, jnp.max(ws_log, axis=-1))
        ws = jnp.exp(ws_log - m_new[..., None])
        dec = jnp.exp(bL + m - m_new)
        C = dec[..., None, None] * C + jnp.einsum('bhsv,bhsd->bhvd', v_ * ws[..., None], k_)
        n = dec[..., None] * n + jnp.einsum('bhs,bhsd->bhd', ws, k_)
        return (C, n, m_new), h

    (C, n, m), h = lax.scan(step, (C0, n0, m0),
                            (to_chunks(q), to_chunks(k), to_chunks(v), to_chunks(log_i), to_chunks(log_f)))
    h = jnp.moveaxis(h, 0, 2).reshape(Bn, Hh, T, v.shape[-1])
    return h, C, n, m


def mlstm_mix(x, C0, n0, m0, w_in, b_gates, norm_g, w_o):
    Bn, T, D = x.shape
    Hh = ML_HEADS
    q, k, v, og, gates = jnp.split(x @ w_in, [Hh * ML_DK, 2 * Hh * ML_DK, 2 * Hh * ML_DK + Hh * ML_DV,
                                              2 * Hh * ML_DK + 2 * Hh * ML_DV], axis=-1)
    hd = lambda t, d: jnp.swapaxes(t.astype(F32).reshape(Bn, T, Hh, d), 1, 2)
    q, k, v = hd(q, ML_DK), hd(k, ML_DK) * (ML_DK ** -0.5), hd(v, ML_DV)
    gates = gates.astype(F32) + b_gates.astype(F32)
    gates = ML_GATE_CAP * jnp.tanh(gates / ML_GATE_CAP)
    log_i = jnp.swapaxes(gates[..., :Hh], 1, 2)
    log_f = jax.nn.log_sigmoid(jnp.swapaxes(gates[..., Hh:], 1, 2))
    h, C, n, m = mlstm_chunkwise(q, k, v, log_i, log_f, C0.astype(F32), n0.astype(F32), m0.astype(F32))
    h = h * lax.rsqrt(jnp.mean(h * h, axis=-1, keepdims=True) + ML_NORM_EPS)
    h = jnp.swapaxes(h, 1, 2).reshape(Bn, T, Hh * ML_DV) * norm_g.astype(F32)
    out = (jax.nn.sigmoid(og) * h.astype(x.dtype)) @ w_o
    return out, C.astype(C0.dtype), n.astype(n0.dtype), m.astype(m0.dtype)


def conformer_conv(x, buf, w_pw1, b_pw1, w_dw, b_dw, ln_g, ln_b, w_pw2, b_pw2):
    a, gate = jnp.split(x @ w_pw1 + b_pw1, 2, axis=-1)
    u = a * jax.nn.sigmoid(gate)
    y, new_buf = causal_depthwise_conv(u, buf, w_dw)
    y = jax.nn.silu(layer_norm(y + b_dw, ln_g, ln_b, CF_LN_EPS))
    return y @ w_pw2 + b_pw2, new_buf


def short_gated_conv(x, buf, w_in, w_dw, w_out):
    b_gate, c_gate, h = jnp.split(x @ w_in, 3, axis=-1)
    y, new_buf = causal_depthwise_conv(c_gate * h, buf, w_dw)
    return (b_gate * y) @ w_out, new_buf


def trunk(x, states, p):
    wkv, shift, mC, mn, mm, cfb, scb = states
    o_wkv, o_shift, o_C, o_n, o_m, o_cf, o_sc = [], [], [], [], [], [], []
    for i in range(DEPTH):
        j = i // N_MIXERS
        kind = i % N_MIXERS
        x = x + 0.5 * swiglu(rms_norm(x, p['norm_g'][i, 0]), p['ffn_w_gu'][i, 0], p['ffn_w_down'][i, 0])
        h = rms_norm(x, p['norm_g'][i, 1])
        if kind == 0:
            mix, s_new, sh_new = rwkv7_time_mix(h, shift[j], wkv[j], *[p[name][j] for name in RW_KEYS])
            o_wkv.append(s_new)
            o_shift.append(sh_new)
        elif kind == 1:
            mix, C_new, n_new, m_new = mlstm_mix(h, mC[j], mn[j], mm[j], *[p[name][j] for name in ML_KEYS])
            o_C.append(C_new)
            o_n.append(n_new)
            o_m.append(m_new)
        elif kind == 2:
            mix, b_new = conformer_conv(h, cfb[j], *[p[name][j] for name in CF_KEYS])
            o_cf.append(b_new)
        else:
            mix, b_new = short_gated_conv(h, scb[j], *[p[name][j] for name in SC_KEYS])
            o_sc.append(b_new)
        x = x + mix
        x = x + 0.5 * swiglu(rms_norm(x, p['norm_g'][i, 2]), p['ffn_w_gu'][i, 1], p['ffn_w_down'][i, 1])
    y = rms_norm(x, p['final_norm_g'])
    return y, (jnp.stack(o_wkv), jnp.stack(o_shift), jnp.stack(o_C), jnp.stack(o_n), jnp.stack(o_m),
               jnp.stack(o_cf), jnp.stack(o_sc))


def setup_inputs(seed: int = 0) -> dict:
    key = jax.random.key(seed)
    ks = iter(jax.random.split(key, 64))
    D = D_MODEL

    def nrm(shape, scale):
        return jax.random.normal(next(ks), shape, jnp.float32) * scale

    def gain(shape):
        return 1.0 + nrm(shape, 0.02)

    return {
        'x_prompt': nrm((BATCH, SEQ, D), 1.0),
        'x_sample': nrm((DEC_BATCH, DEC_SEQ, D), 1.0),
        'state_rwkv_wkv': nrm((N_RWKV, DEC_BATCH, RW_HEADS, RW_HEAD, RW_HEAD), 0.3),
        'state_rwkv_shift': nrm((N_RWKV, DEC_BATCH, D), 1.0),
        'state_mlstm_C': nrm((N_MLSTM, DEC_BATCH, ML_HEADS, ML_DV, ML_DK), 0.1),
        'state_mlstm_n': nrm((N_MLSTM, DEC_BATCH, ML_HEADS, ML_DK), 0.1),
        'state_mlstm_m': nrm((N_MLSTM, DEC_BATCH, ML_HEADS), 0.5),
        'state_conf_conv': nrm((N_CONF, DEC_BATCH, CF_KERNEL - 1, D), 0.5),
        'state_sconv_conv': nrm((N_SCONV, DEC_BATCH, SC_KERNEL - 1, D), 0.5),
        'norm_g': gain((DEPTH, 3, D)),
        'ffn_w_gu': nrm((DEPTH, 2, D, 2 * D_FF), D ** -0.5),
        'ffn_w_down': nrm((DEPTH, 2, D_FF, D), D_FF ** -0.5),
        'final_norm_g': gain((D,)),
        'rw_mu': jax.random.uniform(next(ks), (N_RWKV, 6, D), jnp.float32),
        'rw_w_rkv': nrm((N_RWKV, 3, D, D), D ** -0.5),
        'rw_w0': jax.random.uniform(next(ks), (N_RWKV, D), jnp.float32, minval=-6.0, maxval=-1.0),
        'rw_w1': nrm((N_RWKV, D, RW_DECAY_LORA), D ** -0.5),
        'rw_w2': nrm((N_RWKV, RW_DECAY_LORA, D), 0.5 * RW_DECAY_LORA ** -0.5),
        'rw_a0': nrm((N_RWKV, D), 0.1),
        'rw_a1': nrm((N_RWKV, D, RW_AAA_LORA), D ** -0.5),
        'rw_a2': nrm((N_RWKV, RW_AAA_LORA, D), RW_AAA_LORA ** -0.5),
        'rw_g1': nrm((N_RWKV, D, RW_GATE_LORA), D ** -0.5),
        'rw_g2': nrm((N_RWKV, RW_GATE_LORA, D), RW_GATE_LORA ** -0.5),
        'rw_k_k': 0.85 + nrm((N_RWKV, D), 0.05),
        'rw_k_a': 1.0 + nrm((N_RWKV, D), 0.05),
        'rw_r_k': nrm((N_RWKV, D), 0.1),
        'rw_ln_g': gain((N_RWKV, D)),
        'rw_ln_b': nrm((N_RWKV, D), 0.02),
        'rw_w_o': nrm((N_RWKV, D, D), D ** -0.5),
        'ml_w_in': nrm((N_MLSTM, D, ML_PROJ), D ** -0.5),
        'ml_b_gates': jnp.concatenate([nrm((N_MLSTM, ML_HEADS), 0.1),
                                       3.0 + nrm((N_MLSTM, ML_HEADS), 0.5)], axis=-1),
        'ml_norm_g': gain((N_MLSTM, ML_HEADS * ML_DV)),
        'ml_w_o': nrm((N_MLSTM, ML_HEADS * ML_DV, D), (ML_HEADS * ML_DV) ** -0.5),
        'cf_w_pw1': nrm((N_CONF, D, 2 * D), D ** -0.5),
        'cf_b_pw1': nrm((N_CONF, 2 * D), 0.02),
        'cf_w_dw': nrm((N_CONF, CF_KERNEL, D), CF_KERNEL ** -0.5),
        'cf_b_dw': nrm((N_CONF, D), 0.02),
        'cf_ln_g': gain((N_CONF, D)),
        'cf_ln_b': nrm((N_CONF, D), 0.02),
        'cf_w_pw2': nrm((N_CONF, D, D), D ** -0.5),
        'cf_b_pw2': nrm((N_CONF, D), 0.02),
        'sc_w_in': nrm((N_SCONV, D, 3 * D), D ** -0.5),
        'sc_w_dw': nrm((N_SCONV, SC_KERNEL, D), SC_KERNEL ** -0.5),
        'sc_w_out': nrm((N_SCONV, D, D), D ** -0.5),
    }


def reference(x_prompt, x_sample, state_rwkv_wkv, state_rwkv_shift, state_mlstm_C, state_mlstm_n, state_mlstm_m,
              state_conf_conv, state_sconv_conv, norm_g, ffn_w_gu, ffn_w_down, final_norm_g,
              rw_mu, rw_w_rkv, rw_w0, rw_w1, rw_w2, rw_a0, rw_a1, rw_a2, rw_g1, rw_g2, rw_k_k, rw_k_a, rw_r_k,
              rw_ln_g, rw_ln_b, rw_w_o, ml_w_in, ml_b_gates, ml_norm_g, ml_w_o,
              cf_w_pw1, cf_b_pw1, cf_w_dw, cf_b_dw, cf_ln_g, cf_ln_b, cf_w_pw2, cf_b_pw2,
              sc_w_in, sc_w_dw, sc_w_out):
    p = dict(norm_g=norm_g, ffn_w_gu=ffn_w_gu, ffn_w_down=ffn_w_down, final_norm_g=final_norm_g,
             rw_mu=rw_mu, rw_w_rkv=rw_w_rkv, rw_w0=rw_w0, rw_w1=rw_w1, rw_w2=rw_w2, rw_a0=rw_a0, rw_a1=rw_a1,
             rw_a2=rw_a2, rw_g1=rw_g1, rw_g2=rw_g2, rw_k_k=rw_k_k, rw_k_a=rw_k_a, rw_r_k=rw_r_k,
             rw_ln_g=rw_ln_g, rw_ln_b=rw_ln_b, rw_w_o=rw_w_o,
             ml_w_in=ml_w_in, ml_b_gates=ml_b_gates, ml_norm_g=ml_norm_g, ml_w_o=ml_w_o,
             cf_w_pw1=cf_w_pw1, cf_b_pw1=cf_b_pw1, cf_w_dw=cf_w_dw, cf_b_dw=cf_b_dw, cf_ln_g=cf_ln_g,
             cf_ln_b=cf_ln_b, cf_w_pw2=cf_w_pw2, cf_b_pw2=cf_b_pw2,
             sc_w_in=sc_w_in, sc_w_dw=sc_w_dw, sc_w_out=sc_w_out)
    zdt = x_prompt.dtype
    Bp = x_prompt.shape[0]
    prompt_states = (
        jnp.zeros((N_RWKV, Bp, RW_HEADS, RW_HEAD, RW_HEAD), zdt),
        jnp.zeros((N_RWKV, Bp, D_MODEL), zdt),
        jnp.zeros((N_MLSTM, Bp, ML_HEADS, ML_DV, ML_DK), zdt),
        jnp.zeros((N_MLSTM, Bp, ML_HEADS, ML_DK), zdt),
        jnp.zeros((N_MLSTM, Bp, ML_HEADS), zdt),
        jnp.zeros((N_CONF, Bp, CF_KERNEL - 1, D_MODEL), zdt),
        jnp.zeros((N_SCONV, Bp, SC_KERNEL - 1, D_MODEL), zdt),
    )
    sample_states = (state_rwkv_wkv, state_rwkv_shift, state_mlstm_C, state_mlstm_n, state_mlstm_m,
                     state_conf_conv, state_sconv_conv)
    y_prompt, (p_wkv, p_shift, p_C, p_n, p_m, p_cf, p_sc) = trunk(x_prompt, prompt_states, p)
    y_sample, (s_wkv, s_shift, s_C, s_n, s_m, s_cf, s_sc) = trunk(x_sample, sample_states, p)
    return (y_prompt, y_sample, p_wkv, p_shift, p_C, p_n, p_m, p_cf, p_sc,
            s_wkv, s_shift, s_C, s_n, s_m, s_cf, s_sc)
```

```python
import functools
import math

import jax
import jax.numpy as jnp
from jax import lax
from jax.experimental import pallas as pl
from jax.experimental.pallas import tpu as pltpu

F32 = jnp.float32
BF16 = jnp.bfloat16

NORM_EPS = 1e-6
RW_HEAD = 64
RW_GN_EPS = 64e-5
ML_HEADS = 8
ML_GATE_CAP = 15.0
ML_NORM_EPS = 1e-6
CF_LN_EPS = 1e-5

V7X_LANES = 128
V7X_SUBLANES = 8
V7X_VMEM_BYTES = 64 * 1024 * 1024
VMEM_LIMIT_CAP = V7X_VMEM_BYTES - 6 * 1024 * 1024


def _nbytes(shape, dtype):
    return math.prod(shape) * jnp.dtype(dtype).itemsize


def _params(sem, vmem_est):
    limit = int(min(max(vmem_est * 1.2 + (2 << 20), 16 << 20), VMEM_LIMIT_CAP))
    return pltpu.CompilerParams(dimension_semantics=sem, vmem_limit_bytes=limit)


def _pick_bm(m, target):
    best = None
    for bm in range(16, min(m, target) + 1, 16):
        if m % bm == 0:
            best = bm
    return best if best is not None else m


def _rms(x):
    return x * lax.rsqrt(jnp.mean(x * x, axis=-1, keepdims=True) + NORM_EPS)


def _mm_kernel(*refs, n_a, n_av, n_w, n_e, n_ev, prologue, epilogue, use_scratch):
    a_refs = refs[:n_a]
    av_refs = refs[n_a:n_a + n_av]
    w_refs = refs[n_a + n_av:n_a + n_av + n_w]
    e_refs = refs[n_a + n_av + n_w:n_a + n_av + n_w + n_e]
    ev_refs = refs[n_a + n_av + n_w + n_e:n_a + n_av + n_w + n_e + n_ev]
    o_ref = refs[n_a + n_av + n_w + n_e + n_ev]
    if use_scratch:
        lhs_ref = refs[-1]

        @pl.when(pl.program_id(1) == 0)
        def _():
            lhs_ref[...] = prologue([r[...] for r in a_refs], [r[...] for r in av_refs]).astype(BF16)

        lhs = lhs_ref[...]
    else:
        lhs = a_refs[0][...]
    accs = [jnp.dot(lhs, w[...], preferred_element_type=F32) for w in w_refs]
    out = epilogue(accs, [r[...] for r in e_refs], [r[...] for r in ev_refs])
    o_ref[...] = out.astype(o_ref.dtype)


def _mm(a_list, w_list, *, n, out_dtype, prologue=None, epilogue=None, avecs=(), e_blocks=(), evecs=(),
        w_col_offsets=None, bm_target=1088, bn=512, name="mm"):
    m, k = a_list[0].shape
    bm = _pick_bm(m, bm_target)
    bn = min(bn, n)
    assert n % bn == 0 and m % bm == 0
    if w_col_offsets is None:
        w_col_offsets = (0,) * len(w_list)
    for off in w_col_offsets:
        assert off % bn == 0
    use_scratch = prologue is not None or a_list[0].dtype != BF16
    if prologue is None:
        prologue = lambda a, v: a[0]
    if epilogue is None:
        epilogue = lambda accs, e, ev: accs[0]

    in_specs = []
    for _ in a_list:
        in_specs.append(pl.BlockSpec((bm, k), lambda i, j: (i, 0)))
    for _ in avecs:
        in_specs.append(pl.BlockSpec((1, k), lambda i, j: (0, 0)))
    for off in w_col_offsets:
        in_specs.append(pl.BlockSpec((k, bn), functools.partial(lambda i, j, o: (0, j + o), o=off // bn)))
    for _ in e_blocks:
        in_specs.append(pl.BlockSpec((bm, bn), lambda i, j: (i, j)))
    for _ in evecs:
        in_specs.append(pl.BlockSpec((1, bn), lambda i, j: (0, j)))

    est = sum(2 * _nbytes((bm, k), a.dtype) for a in a_list)
    est += len(w_list) * 2 * _nbytes((k, bn), BF16)
    est += sum(2 * _nbytes((bm, bn), e.dtype) for e in e_blocks)
    est += 2 * _nbytes((bm, bn), out_dtype)
    est += (len(w_list) + 2) * _nbytes((bm, bn), F32)
    scratch = []
    if use_scratch:
        scratch.append(pltpu.VMEM((bm, k), BF16))
        est += _nbytes((bm, k), BF16) + 2 * _nbytes((bm, k), F32)

    kern = functools.partial(_mm_kernel, n_a=len(a_list), n_av=len(avecs), n_w=len(w_list), n_e=len(e_blocks),
                             n_ev=len(evecs), prologue=prologue, epilogue=epilogue, use_scratch=use_scratch)
    return pl.pallas_call(
        kern,
        grid=(m // bm, n // bn),
        in_specs=in_specs,
        out_specs=pl.BlockSpec((bm, bn), lambda i, j: (i, j)),
        out_shape=jax.ShapeDtypeStruct((m, n), out_dtype),
        scratch_shapes=scratch,
        compiler_params=_params(("parallel", "arbitrary"), est),
        name=name,
    )(*a_list, *avecs, *w_list, *e_blocks, *evecs)


def _rmsnorm_kernel(x_ref, g_ref, o_ref):
    o_ref[...] = (_rms(x_ref[...]) * g_ref[...]).astype(o_ref.dtype)


def _rmsnorm(x, g, out_dtype):
    m, d = x.shape
    bm = _pick_bm(m, 1088)
    est = 2 * _nbytes((bm, d), F32) * 3
    return pl.pallas_call(
        _rmsnorm_kernel,
        grid=(m // bm,),
        in_specs=[pl.BlockSpec((bm, d), lambda i: (i, 0)), pl.BlockSpec((1, d), lambda i: (0, 0))],
        out_specs=pl.BlockSpec((bm, d), lambda i: (i, 0)),
        out_shape=jax.ShapeDtypeStruct((m, d), out_dtype),
        compiler_params=_params(("parallel",), est),
        name="rmsnorm",
    )(x, g.reshape(1, d))


def _ffn(x, g, w_gu, w_down):
    d_ff = w_down.shape[0]
    h = _mm([x], [w_gu, w_gu], n=d_ff, out_dtype=BF16, avecs=[g.reshape(1, -1)],
            prologue=lambda a, v: _rms(a[0]) * v[0],
            epilogue=lambda accs, e, ev: accs[0] * jax.nn.sigmoid(accs[0]) * accs[1],
            w_col_offsets=(0, d_ff), name="ffn_up")
    return _mm([h], [w_down], n=x.shape[1], out_dtype=F32, e_blocks=[x],
               epilogue=lambda accs, e, ev: e[0] + 0.5 * accs[0], name="ffn_down")


def _rwkv_scan_kernel(r_ref, k_ref, v_ref, w_ref, a_ref, kk_p_ref, ka_p_ref, rk_p_ref, lg_ref, lb_ref, s0_ref,
                      y_ref, sT_ref, s_ref, *, tb, nh):
    tblk = pl.program_id(1)

    @pl.when(tblk == 0)
    def _():
        s_ref[...] = s0_ref[...]

    kk_p = kk_p_ref[...]
    ka_p = ka_p_ref[...]
    rk_p = rk_p_ref[...]
    ln_g = lg_ref[...]
    ln_b = lb_ref[...]
    vgrp = V7X_SUBLANES

    def step(t, carry):
        r = r_ref[t]
        k = k_ref[t]
        v = v_ref[t]
        logw = -jax.nn.softplus(-w_ref[t]) - 0.5
        d = jnp.exp(-jnp.exp(logw))
        a = jax.nn.sigmoid(a_ref[t])
        kk = k * kk_p
        kk = kk / jnp.maximum(jnp.sqrt(jnp.sum(kk * kk, axis=0, keepdims=True)), 1e-12)
        kmod = k * (1.0 + (a - 1.0) * ka_p)
        b = kk * a

        def vgroup(g, c2):
            base = pl.multiple_of(g * vgrp, vgrp)
            v8 = v_ref[t, pl.ds(base, vgrp), :]
            ys = []
            for j in range(vgrp):
                s_old = s_ref[base + j]
                sa = jnp.sum(s_old * kk, axis=0, keepdims=True)
                s_new = s_old * d - sa * b + v8[j:j + 1, :] * kmod
                s_ref[base + j] = s_new
                ys.append(jnp.sum(s_new * r, axis=0, keepdims=True))
            y_ref[t, pl.ds(base, vgrp), :] = jnp.concatenate(ys, axis=0)
            return c2

        lax.fori_loop(0, nh // vgrp, vgroup, 0)

        y = y_ref[t]
        mean = jnp.mean(y, axis=0, keepdims=True)
        var = jnp.mean(jnp.square(y - mean), axis=0, keepdims=True)
        yn = (y - mean) * lax.rsqrt(var + RW_GN_EPS) * ln_g + ln_b
        bonus = jnp.sum(r * kmod * rk_p, axis=0, keepdims=True) * v
        y_ref[t] = yn + bonus
        return carry

    lax.fori_loop(0, tb, step, 0)

    @pl.when(tblk == pl.num_programs(1) - 1)
    def _():
        sT_ref[...] = s_ref[...]


def _rwkv_scan(r, k, v, w_pre, a_pre, kk_p, ka_p, rk_p, ln_g, ln_b, s0, *, tb):
    t, nh, c = r.shape
    lanes = V7X_LANES
    assert c % lanes == 0 and t % tb == 0
    seq = pl.BlockSpec((tb, nh, lanes), lambda ci, ti: (ti, 0, ci))
    par = pl.BlockSpec((nh, lanes), lambda ci, ti: (0, ci))
    st = pl.BlockSpec((nh, nh, lanes), lambda ci, ti: (0, 0, ci))
    est = 6 * 2 * _nbytes((tb, nh, lanes), F32) + 5 * _nbytes((nh, nh, lanes), F32)
    kern = functools.partial(_rwkv_scan_kernel, tb=tb, nh=nh)
    return pl.pallas_call(
        kern,
        grid=(c // lanes, t // tb),
        in_specs=[seq] * 5 + [par] * 5 + [st],
        out_specs=[seq, st],
        out_shape=[jax.ShapeDtypeStruct((t, nh, c), F32), jax.ShapeDtypeStruct((nh, nh, c), F32)],
        scratch_shapes=[pltpu.VMEM((nh, nh, lanes), F32)],
        compiler_params=_params(("parallel", "arbitrary"), est),
        name="rwkv_scan",
    )(r, k, v, w_pre, a_pre, kk_p, ka_p, rk_p, ln_g, ln_b, s0)


def _to_chains(x, bsz, t, nheads):
    return x.reshape(bsz, t, nheads, RW_HEAD).transpose(1, 3, 0, 2).reshape(t, RW_HEAD, bsz * nheads)


def _from_chains(y, bsz, t, nheads):
    return y.reshape(t, RW_HEAD, bsz, nheads).transpose(2, 0, 3, 1).reshape(bsz * t, nheads * RW_HEAD)


def _rwkv_group(r, k, v, w_pre, a_pre, wkv0, p, bsz, t):
    d_model = r.shape[1]
    nheads = d_model // RW_HEAD
    c = bsz * nheads
    chain_param = lambda q: jnp.tile(q.reshape(nheads, RW_HEAD).T, (1, bsz))
    if wkv0 is None:
        s0 = jnp.zeros((RW_HEAD, RW_HEAD, c), F32)
    else:
        s0 = wkv0.astype(F32).transpose(2, 3, 0, 1).reshape(RW_HEAD, RW_HEAD, c)
    tb = t if t <= 32 else 32
    y, s_fin = _rwkv_scan(*[_to_chains(q, bsz, t, nheads) for q in (r, k, v, w_pre, a_pre)],
                          *[chain_param(p[name]) for name in ('rw_k_k', 'rw_k_a', 'rw_r_k', 'rw_ln_g', 'rw_ln_b')],
                          s0, tb=tb)
    s_fin = s_fin.reshape(RW_HEAD, RW_HEAD, bsz, nheads).transpose(2, 3, 0, 1)
    return _from_chains(y, bsz, t, nheads), s_fin


def _shift_rows(h, shift_prev, bsz, t):
    d = h.shape[1]
    h3 = h.reshape(bsz, t, d)
    return jnp.concatenate([shift_prev[:, None, :], h3[:, :-1]], axis=1).reshape(bsz * t, d)


def _rwkv_layer(x, groups, states, p, wb):
    d = x.shape[1]
    h = _rmsnorm(x, p['norm_g'], F32)
    h_prev = jnp.concatenate([_shift_rows(h[r0:r0 + b * t], st[1], b, t) for (r0, b, t), st in zip(groups, states)],
                             axis=0)
    mu = p['rw_mu']
    mix = lambda a, v: a[0] + (a[1] - a[0]) * v[0]
    proj = lambda i, w, n, **kw: _mm([h, h_prev], [w], n=n, out_dtype=F32, avecs=[mu[i].reshape(1, d)],
                                     prologue=mix, **kw)
    r = proj(0, wb['rw_w_rkv'][0], d, name="rw_r")
    k = proj(2, wb['rw_w_rkv'][1], d, name="rw_k")
    v = proj(3, wb['rw_w_rkv'][2], d, name="rw_v")
    wl = proj(1, wb['rw_w1'], wb['rw_w1'].shape[1], epilogue=lambda accs, e, ev: jnp.tanh(accs[0]), name="rw_w1")
    al = proj(4, wb['rw_a1'], wb['rw_a1'].shape[1], name="rw_a1")
    gl = proj(5, wb['rw_g1'], wb['rw_g1'].shape[1], epilogue=lambda accs, e, ev: jax.nn.sigmoid(accs[0]),
              name="rw_g1")
    add_vec = lambda accs, e, ev: ev[0] + accs[0]
    w_pre = _mm([wl], [wb['rw_w2']], n=d, out_dtype=F32, evecs=[p['rw_w0'].reshape(1, d)], epilogue=add_vec,
                name="rw_w2")
    a_pre = _mm([al], [wb['rw_a2']], n=d, out_dtype=F32, evecs=[p['rw_a0'].reshape(1, d)], epilogue=add_vec,
                name="rw_a2")
    g = _mm([gl], [wb['rw_g2']], n=d, out_dtype=F32, name="rw_g2")

    ys, new_states = [], []
    for (r0, b, t), st in zip(groups, states):
        sl = slice(r0, r0 + b * t)
        y, s_fin = _rwkv_group(r[sl], k[sl], v[sl], w_pre[sl], a_pre[sl], st[0], p, b, t)
        ys.append(y)
        new_states.append((s_fin, h[sl].reshape(b, t, d)[:, -1]))
    y = jnp.concatenate(ys, axis=0)
    x = _mm([y, g], [wb['rw_w_o']], n=d, out_dtype=F32, prologue=lambda a, v: a[0] * a[1], e_blocks=[x],
            epilogue=lambda accs, e, ev: e[0] + accs[0], name="rw_out")
    return x, new_states


def _mlstm_kernel(q_ref, k_ref, v_ref, gt_ref, gtt_ref, bg_ref, bgt_ref, ng_ref, c0_ref, n0_ref, m0_ref,
                  h_ref, cT_ref, nT_ref, mT_ref, c_ref, n_ref, m_ref, *, nheads, dk, dv, chunk):
    ci = pl.program_id(1)

    @pl.when(ci == 0)
    def _():
        c_ref[...] = c0_ref[...]
        n_ref[...] = n0_ref[...]
        m_ref[...] = m0_ref[...]

    L = chunk
    cap = lambda z: ML_GATE_CAP * jnp.tanh(z / ML_GATE_CAP)
    gates_c = cap(gt_ref[...] + bg_ref[...])
    gates_r = cap(gtt_ref[...] + bgt_ref[...])
    row_id = lax.broadcasted_iota(jnp.int32, (L, L), 0)
    col_id = lax.broadcasted_iota(jnp.int32, (L, L), 1)
    causal = col_id <= row_id
    k_scale = dk ** -0.5

    for hd in range(nheads):
        li_c = gates_c[:, hd:hd + 1]
        li_r = gates_r[hd:hd + 1, :]
        lf_c = jax.nn.log_sigmoid(gates_c[:, nheads + hd:nheads + hd + 1])
        lf_r = jax.nn.log_sigmoid(gates_r[nheads + hd:nheads + hd + 1, :])
        b_c = jnp.sum(jnp.where(causal, lf_r, 0.0), axis=1, keepdims=True)
        b_r = jnp.sum(jnp.where(row_id <= col_id, lf_c, 0.0), axis=0, keepdims=True)
        m_prev = m_ref[hd][:, :1]
        dmat = jnp.where(causal, b_c - b_r + li_r, -jnp.inf)
        inter = b_c + m_prev
        m_t = jnp.maximum(jnp.max(dmat, axis=1, keepdims=True), inter)

        q = q_ref[:, hd * dk:(hd + 1) * dk]
        k = k_ref[:, hd * dk:(hd + 1) * dk] * k_scale
        v = v_ref[:, hd * dv:(hd + 1) * dv]
        qb, kb, vb = q.astype(BF16), k.astype(BF16), v.astype(BF16)
        c_old = c_ref[hd]
        n_old = n_ref[hd]

        s = lax.dot_general(qb, kb, (((1,), (1,)), ((), ())), preferred_element_type=F32) * jnp.exp(dmat - m_t)
        w_inter = jnp.exp(inter - m_t)
        cq = lax.dot_general(qb, c_old.astype(BF16), (((1,), (1,)), ((), ())), preferred_element_type=F32)
        num = jnp.dot(s.astype(BF16), vb, preferred_element_type=F32) + w_inter * cq
        nq = jnp.sum(q * n_old, axis=1, keepdims=True)
        den = jnp.sum(s, axis=1, keepdims=True) + w_inter * nq
        h = num / jnp.maximum(jnp.abs(den), jnp.exp(-m_t))
        hn = h * lax.rsqrt(jnp.mean(h * h, axis=1, keepdims=True) + ML_NORM_EPS)
        h_ref[:, hd * dv:(hd + 1) * dv] = hn * ng_ref[:, hd * dv:(hd + 1) * dv]

        b_last = b_c[L - 1:L, :]
        ws_log = b_last - b_c + li_c
        m_new = jnp.maximum(b_last + m_prev, jnp.max(ws_log, axis=0, keepdims=True))
        ws = jnp.exp(ws_log - m_new)
        dec = jnp.exp(b_last + m_prev - m_new)
        upd = lax.dot_general((v * ws).astype(BF16), kb, (((0,), (0,)), ((), ())), preferred_element_type=F32)
        c_ref[hd] = dec * c_old + upd
        n_ref[hd] = dec * n_old + jnp.sum(ws * k, axis=0, keepdims=True)
        m_ref[hd] = jnp.broadcast_to(m_new, m_ref.shape[1:])

    @pl.when(ci == pl.num_programs(1) - 1)
    def _():
        cT_ref[...] = c_ref[...]
        nT_ref[...] = n_ref[...]
        mT_ref[...] = m_ref[...]


def _mlstm_group(q, k, v, gates, b_gates, norm_g, c0, n0, m0, bsz, t):
    nheads = ML_HEADS
    dk = q.shape[1] // nheads
    dv = v.shape[1] // nheads
    chunk = min(t, V7X_LANES)
    assert t % chunk == 0
    nc = t // chunk
    q3, k3, v3 = (z.reshape(bsz, t, -1) for z in (q, k, v))
    g3 = gates.reshape(bsz, t, 2 * nheads)
    g3t = g3.transpose(0, 2, 1)
    lanes = V7X_LANES
    n0r = n0.astype(F32).reshape(bsz, nheads, 1, dk)
    m0r = jnp.broadcast_to(m0.astype(F32).reshape(bsz, nheads, 1, 1), (bsz, nheads, 1, lanes))
    seq = lambda w: pl.BlockSpec((None, chunk, w), lambda b, c: (b, c, 0))
    st_c = pl.BlockSpec((None, nheads, dv, dk), lambda b, c: (b, 0, 0, 0))
    st_n = pl.BlockSpec((None, nheads, 1, dk), lambda b, c: (b, 0, 0, 0))
    st_m = pl.BlockSpec((None, nheads, 1, lanes), lambda b, c: (b, 0, 0, 0))
    est = 2 * 3 * _nbytes((chunk, nheads * (2 * dk + 2 * dv)), F32) + 5 * _nbytes((nheads, dv, dk), F32) + (8 << 20)
    kern = functools.partial(_mlstm_kernel, nheads=nheads, dk=dk, dv=dv, chunk=chunk)
    h, c_fin, n_fin, m_fin = pl.pallas_call(
        kern,
        grid=(bsz, nc),
        in_specs=[seq(nheads * dk), seq(nheads * dk), seq(nheads * dv), seq(2 * nheads),
                  pl.BlockSpec((None, 2 * nheads, chunk), lambda b, c: (b, 0, c)),
                  pl.BlockSpec((1, 2 * nheads), lambda b, c: (0, 0)),
                  pl.BlockSpec((2 * nheads, 1), lambda b, c: (0, 0)),
                  pl.BlockSpec((1, nheads * dv), lambda b, c: (0, 0)),
                  st_c, st_n, st_m],
        out_specs=[seq(nheads * dv), st_c, st_n, st_m],
        out_shape=[jax.ShapeDtypeStruct((bsz, t, nheads * dv), F32),
                   jax.ShapeDtypeStruct((bsz, nheads, dv, dk), F32),
                   jax.ShapeDtypeStruct((bsz, nheads, 1, dk), F32),
                   jax.ShapeDtypeStruct((bsz, nheads, 1, lanes), F32)],
        scratch_shapes=[pltpu.VMEM((nheads, dv, dk), F32), pltpu.VMEM((nheads, 1, dk), F32),
                        pltpu.VMEM((nheads, 1, lanes), F32)],
        compiler_params=_params(("parallel", "arbitrary"), est),
        name="mlstm_chunk",
    )(q3, k3, v3, g3, g3t, b_gates.reshape(1, -1), b_gates.reshape(-1, 1), norm_g.reshape(1, -1),
      c0.astype(F32), n0r, m0r)
    return h.reshape(bsz * t, nheads * dv), c_fin, n_fin.reshape(bsz, nheads, dk), m_fin[:, :, 0, 0]


def _mlstm_layer(x, groups, states, p, wb):
    d = x.shape[1]
    nheads = ML_HEADS
    w_in = wb['ml_w_in']
    dv = d // nheads
    dk = dv // 2
    n_q = nheads * dk
    xn = _rmsnorm(x, p['norm_g'], BF16)
    q = _mm([xn], [w_in], n=n_q, out_dtype=F32, w_col_offsets=(0,), name="ml_q")
    k = _mm([xn], [w_in], n=n_q, out_dtype=F32, w_col_offsets=(n_q,), name="ml_k")
    v = _mm([xn], [w_in], n=d, out_dtype=F32, w_col_offsets=(2 * n_q,), name="ml_v")
    og = _mm([xn], [w_in], n=d, out_dtype=F32, w_col_offsets=(2 * n_q + d,), name="ml_og")
    gates = _mm([xn], [wb['ml_w_gates']], n=2 * nheads, out_dtype=F32, name="ml_gates")
    hs, new_states = [], []
    for (r0, b, t), st in zip(groups, states):
        sl = slice(r0, r0 + b * t)
        h, c_fin, n_fin, m_fin = _mlstm_group(q[sl], k[sl], v[sl], gates[sl], p['ml_b_gates'], p['ml_norm_g'],
                                              st[0], st[1], st[2], b, t)
        hs.append(h)
        new_states.append((c_fin, n_fin, m_fin))
    h = jnp.concatenate(hs, axis=0)
    x = _mm([og, h], [wb['ml_w_o']], n=d, out_dtype=F32, prologue=lambda a, v: jax.nn.sigmoid(a[0]) * a[1],
            e_blocks=[x], epilogue=lambda accs, e, ev: e[0] + accs[0], name="ml_out")
    return x, new_states


def _dwconv_kernel(ext_ref, w_ref, y_ref, *, ksize, t, bb, rows, halo):
    w = w_ref[...]
    n_tiles = t // rows

    for b in range(bb):
        def tile(ti, carry):
            base = ti * rows if isinstance(ti, int) else pl.multiple_of(ti * rows, V7X_SUBLANES)
            win = ext_ref[b, pl.ds(base, rows + halo), :]
            acc = w[0:1, :] * win[0:rows, :]
            for j in range(1, ksize):
                acc = acc + w[j:j + 1, :] * win[j:j + rows, :]
            y_ref[b, pl.ds(base, rows), :] = acc
            return carry

        if n_tiles == 1:
            tile(0, 0)
        else:
            lax.fori_loop(0, n_tiles, tile, 0)


def _dwconv(ext, w, t):
    bsz, text, c = ext.shape
    ksize = w.shape[0]
    assert text == t + ksize - 1
    halo = -(-(ksize - 1) // V7X_SUBLANES) * V7X_SUBLANES
    ext = jnp.pad(ext, ((0, 0), (0, halo - (ksize - 1)), (0, 0)))
    bc = 256
    bb = 1 if t >= 64 else 8
    rows = min(t, 64)
    assert bsz % bb == 0 and t % rows == 0 and c % bc == 0
    assert rows % V7X_SUBLANES == 0 or rows == t
    est = 2 * bb * (_nbytes((t + halo + 8, bc), F32) + _nbytes((t + 8, bc), F32)) + (4 << 20)
    kern = functools.partial(_dwconv_kernel, ksize=ksize, t=t, bb=bb, rows=rows, halo=halo)
    return pl.pallas_call(
        kern,
        grid=(bsz // bb, c // bc),
        in_specs=[pl.BlockSpec((bb, t + halo, bc), lambda b, ci: (b, 0, ci)),
                  pl.BlockSpec((ksize, bc), lambda b, ci: (0, ci))],
        out_specs=pl.BlockSpec((bb, t, bc), lambda b, ci: (b, 0, ci)),
        out_shape=jax.ShapeDtypeStruct((bsz, t, c), F32),
        compiler_params=_params(("parallel", "parallel"), est),
        name="dwconv",
    )(ext, w)


def _conv_groups(u, w_dw, groups, bufs):
    ys, new_bufs = [], []
    c = u.shape[1]
    for (r0, b, t), buf in zip(groups, bufs):
        ext = jnp.concatenate([buf.astype(F32), u[r0:r0 + b * t].reshape(b, t, c)], axis=1)
        ys.append(_dwconv(ext, w_dw, t).reshape(b * t, c))
        new_bufs.append(ext[:, t:])
    return jnp.concatenate(ys, axis=0), new_bufs


def _conformer_layer(x, groups, states, p, wb):
    d = x.shape[1]
    xn = _rmsnorm(x, p['norm_g'], BF16)
    b1 = p['cf_b_pw1'].reshape(1, -1)
    u = _mm([xn], [wb['cf_w_pw1'], wb['cf_w_pw1']], n=d, out_dtype=F32, w_col_offsets=(0, d),
            evecs=[b1[:, :d], b1[:, d:]],
            epilogue=lambda accs, e, ev: (accs[0] + ev[0]) * jax.nn.sigmoid(accs[1] + ev[1]), name="cf_pw1")
    y, new_bufs = _conv_groups(u, p['cf_w_dw'], groups, [st[0] for st in states])

    def ln_silu(a, v):
        z = a[0] + v[0]
        mu = jnp.mean(z, axis=-1, keepdims=True)
        var = jnp.mean(jnp.square(z - mu), axis=-1, keepdims=True)
        z = (z - mu) * lax.rsqrt(var + CF_LN_EPS) * v[1] + v[2]
        return z * jax.nn.sigmoid(z)

    x = _mm([y], [wb['cf_w_pw2']], n=d, out_dtype=F32, prologue=ln_silu,
            avecs=[p['cf_b_dw'].reshape(1, d), p['cf_ln_g'].reshape(1, d), p['cf_ln_b'].reshape(1, d)],
            e_blocks=[x], evecs=[p['cf_b_pw2'].reshape(1, d)],
            epilogue=lambda accs, e, ev: e[0] + (accs[0] + ev[0]), name="cf_pw2")
    return x, [(nb,) for nb in new_bufs]


def _sconv_layer(x, groups, states, p, wb):
    d = x.shape[1]
    xn = _rmsnorm(x, p['norm_g'], BF16)
    w_in = wb['sc_w_in']
    b_gate = _mm([xn], [w_in], n=d, out_dtype=F32, w_col_offsets=(0,), name="sc_b")
    ch = _mm([xn], [w_in, w_in], n=d, out_dtype=F32, w_col_offsets=(d, 2 * d),
             epilogue=lambda accs, e, ev: accs[0] * accs[1], name="sc_ch")
    y, new_bufs = _conv_groups(ch, p['sc_w_dw'], groups, [st[0] for st in states])
    x = _mm([b_gate, y], [wb['sc_w_out']], n=d, out_dtype=F32, prologue=lambda a, v: a[0] * a[1], e_blocks=[x],
            epilogue=lambda accs, e, ev: e[0] + accs[0], name="sc_out")
    return x, [(nb,) for nb in new_bufs]


def kernel(x_prompt, x_sample, state_rwkv_wkv, state_rwkv_shift, state_mlstm_C, state_mlstm_n, state_mlstm_m,
           state_conf_conv, state_sconv_conv, norm_g, ffn_w_gu, ffn_w_down, final_norm_g,
           rw_mu, rw_w_rkv, rw_w0, rw_w1, rw_w2, rw_a0, rw_a1, rw_a2, rw_g1, rw_g2, rw_k_k, rw_k_a, rw_r_k,
           rw_ln_g, rw_ln_b, rw_w_o, ml_w_in, ml_b_gates, ml_norm_g, ml_w_o,
           cf_w_pw1, cf_b_pw1, cf_w_dw, cf_b_dw, cf_ln_g, cf_ln_b, cf_w_pw2, cf_b_pw2,
           sc_w_in, sc_w_dw, sc_w_out):
    bp, tp, d = x_prompt.shape
    bs, ts, _ = x_sample.shape
    depth = norm_g.shape[0]
    n_mixers = 4
    groups = [(0, bp, tp), (bp * tp, bs, ts)]
    x = jnp.concatenate([x_prompt.reshape(bp * tp, d), x_sample.reshape(bs * ts, d)], axis=0).astype(F32)

    nheads_rw = d // RW_HEAD
    dv = d // ML_HEADS
    dk = dv // 2
    cf_k = cf_w_dw.shape[1]
    sc_k = sc_w_dw.shape[1]
    zeros = lambda *s: jnp.zeros(s, F32)

    out_states = {name: ([], []) for name in ('wkv', 'shift', 'C', 'n', 'm', 'cf', 'sc')}
    for i in range(depth):
        j = i // n_mixers
        kind = i % n_mixers
        x = _ffn(x, norm_g[i, 0], ffn_w_gu[i, 0].astype(BF16), ffn_w_down[i, 0].astype(BF16))
        if kind == 0:
            p = dict(norm_g=norm_g[i, 1], rw_mu=rw_mu[j], rw_w0=rw_w0[j], rw_a0=rw_a0[j], rw_k_k=rw_k_k[j],
                     rw_k_a=rw_k_a[j], rw_r_k=rw_r_k[j], rw_ln_g=rw_ln_g[j], rw_ln_b=rw_ln_b[j])
            wb = dict(rw_w_rkv=rw_w_rkv[j].astype(BF16), rw_w1=rw_w1[j].astype(BF16), rw_w2=rw_w2[j].astype(BF16),
                      rw_a1=rw_a1[j].astype(BF16), rw_a2=rw_a2[j].astype(BF16), rw_g1=rw_g1[j].astype(BF16),
                      rw_g2=rw_g2[j].astype(BF16), rw_w_o=rw_w_o[j].astype(BF16))
            states = [(None, zeros(bp, d)), (state_rwkv_wkv[j], state_rwkv_shift[j].astype(F32))]
            x, new = _rwkv_layer(x, groups, states, p, wb)
            for gi in range(2):
                out_states['wkv'][gi].append(new[gi][0])
                out_states['shift'][gi].append(new[gi][1])
        elif kind == 1:
            p = dict(norm_g=norm_g[i, 1], ml_b_gates=ml_b_gates[j], ml_norm_g=ml_norm_g[j])
            n_main = 2 * ML_HEADS * dk + 2 * ML_HEADS * dv
            wb = dict(ml_w_in=ml_w_in[j][:, :n_main].astype(BF16), ml_w_gates=ml_w_in[j][:, n_main:].astype(BF16),
                      ml_w_o=ml_w_o[j].astype(BF16))
            states = [(zeros(bp, ML_HEADS, dv, dk), zeros(bp, ML_HEADS, dk), zeros(bp, ML_HEADS)),
                      (state_mlstm_C[j], state_mlstm_n[j], state_mlstm_m[j])]
            x, new = _mlstm_layer(x, groups, states, p, wb)
            for gi in range(2):
                out_states['C'][gi].append(new[gi][0])
                out_states['n'][gi].append(new[gi][1])
                out_states['m'][gi].append(new[gi][2])
        elif kind == 2:
            p = dict(norm_g=norm_g[i, 1], cf_b_pw1=cf_b_pw1[j], cf_w_dw=cf_w_dw[j], cf_b_dw=cf_b_dw[j],
                     cf_ln_g=cf_ln_g[j], cf_ln_b=cf_ln_b[j], cf_b_pw2=cf_b_pw2[j])
            wb = dict(cf_w_pw1=cf_w_pw1[j].astype(BF16), cf_w_pw2=cf_w_pw2[j].astype(BF16))
            states = [(zeros(bp, cf_k - 1, d),), (state_conf_conv[j],)]
            x, new = _conformer_layer(x, groups, states, p, wb)
            for gi in range(2):
                out_states['cf'][gi].append(new[gi][0])
        else:
            p = dict(norm_g=norm_g[i, 1], sc_w_dw=sc_w_dw[j])
            wb = dict(sc_w_in=sc_w_in[j].astype(BF16), sc_w_out=sc_w_out[j].astype(BF16))
            states = [(zeros(bp, sc_k - 1, d),), (state_sconv_conv[j],)]
            x, new = _sconv_layer(x, groups, states, p, wb)
            for gi in range(2):
                out_states['sc'][gi].append(new[gi][0])
        x = _ffn(x, norm_g[i, 2], ffn_w_gu[i, 1].astype(BF16), ffn_w_down[i, 1].astype(BF16))

    y = _rmsnorm(x, final_norm_g, F32)
    y_prompt = y[:bp * tp].reshape(bp, tp, d).astype(x_prompt.dtype)
    y_sample = y[bp * tp:].reshape(bs, ts, d).astype(x_sample.dtype)
    order = ('wkv', 'shift', 'C', 'n', 'm', 'cf', 'sc')
    dt = x_prompt.dtype
    prompt_out = tuple(jnp.stack(out_states[name][0]).astype(dt) for name in order)
    sample_out = tuple(jnp.stack(out_states[name][1]).astype(st.dtype) for name, st in
                       zip(order, (state_rwkv_wkv, state_rwkv_shift, state_mlstm_C, state_mlstm_n, state_mlstm_m,
                                   state_conf_conv, state_sconv_conv)))
    return (y_prompt, y_sample) + prompt_out + sample_out
```

```python
import functools
import math

import jax
import jax.numpy as jnp
from jax import lax
from jax.experimental import pallas as pl
from jax.experimental.pallas import tpu as pltpu

F32 = jnp.float32
BF16 = jnp.bfloat16

NORM_EPS = 1e-6
RW_HEAD = 64
RW_GN_EPS = 64e-5
ML_HEADS = 8
ML_GATE_CAP = 15.0
ML_NORM_EPS = 1e-6
CF_LN_EPS = 1e-5

V7X_LANES = 128
V7X_SUBLANES = 8
V7X_VMEM_BYTES = 64 * 1024 * 1024
VMEM_LIMIT_CAP = V7X_VMEM_BYTES - 6 * 1024 * 1024


def _nbytes(shape, dtype):
    return math.prod(shape) * jnp.dtype(dtype).itemsize


def _params(sem, vmem_est):
    limit = int(min(max(vmem_est * 1.2 + (2 << 20), 16 << 20), VMEM_LIMIT_CAP))
    return pltpu.CompilerParams(dimension_semantics=sem, vmem_limit_bytes=limit)


def _pick_bm(m, target, also_divides=()):
    best = None
    for bm in range(16, min(m, target) + 1, 16):
        if m % bm == 0 and all(o % bm == 0 for o in also_divides):
            best = bm
    assert best is not None
    return best


def _rms(x):
    return x * lax.rsqrt(jnp.mean(x * x, axis=-1, keepdims=True) + NORM_EPS)


def _round_up(x, m):
    return -(-x // m) * m


def _mm_kernel(*refs, counts, prologue, epilogue, use_scratch):
    n_a, n_av, n_pw, n_w, n_e, n_ev = counts
    pos = 0
    groups = []
    for c in counts:
        groups.append(refs[pos:pos + c])
        pos += c
    a_refs, av_refs, pw_refs, w_refs, e_refs, ev_refs = groups
    o_ref = refs[pos]
    if use_scratch:
        lhs_ref = refs[-1]

        @pl.when(pl.program_id(1) == 0)
        def _():
            lhs_ref[...] = prologue([r[...] for r in a_refs], [r[...] for r in av_refs],
                                    [r[...] for r in pw_refs]).astype(BF16)

        lhs = lhs_ref[...]
    else:
        lhs = a_refs[0][...]
    accs = [jnp.dot(lhs, w[...], preferred_element_type=F32) for w in w_refs]
    out = epilogue(accs, [r[...] for r in e_refs], [r[...] for r in ev_refs])
    o_ref[...] = out.astype(o_ref.dtype)


def _mm(a_list, w_list, *, n, out_dtype, prologue=None, epilogue=None, avecs=(), pw_list=(), e_blocks=(), evecs=(),
        bm_target=1088, bn=512, name="mm"):
    m = a_list[0][0].shape[0]
    k = w_list[0][0].shape[-2]
    bm = _pick_bm(m, bm_target)
    bn = min(bn, n)
    assert n % bn == 0
    use_scratch = prologue is not None or a_list[0][0].dtype != BF16
    if prologue is None:
        prologue = lambda a, v, pw: a[0]
    if epilogue is None:
        epilogue = lambda accs, e, ev: accs[0]

    in_specs, args, est = [], [], 0
    for arr, ka, cb in a_list:
        in_specs.append(pl.BlockSpec((bm, ka), functools.partial(lambda i, j, c: (i, c), c=cb)))
        args.append(arr)
        est += 2 * _nbytes((bm, ka), arr.dtype)
    for v in avecs:
        in_specs.append(pl.BlockSpec(v.shape, lambda i, j: (0, 0)))
        args.append(v)
    for pw in pw_list:
        in_specs.append(pl.BlockSpec(pw.shape, lambda i, j: (0, 0)))
        args.append(pw)
        est += 2 * _nbytes(pw.shape, pw.dtype)
    for arr, lead, off in w_list:
        assert off % bn == 0 and arr.shape[-2] == k
        in_specs.append(pl.BlockSpec((None,) * len(lead) + (k, bn),
                                     functools.partial(lambda i, j, l, o: l + (0, j + o), l=tuple(lead), o=off // bn)))
        args.append(arr)
        est += 2 * _nbytes((k, bn), BF16)
    for e in e_blocks:
        in_specs.append(pl.BlockSpec((bm, bn), lambda i, j: (i, j)))
        args.append(e)
        est += 2 * _nbytes((bm, bn), e.dtype)
    for v in evecs:
        in_specs.append(pl.BlockSpec((1, bn), lambda i, j: (0, j)))
        args.append(v)
    est += 2 * _nbytes((bm, bn), out_dtype) + (len(w_list) + 2) * _nbytes((bm, bn), F32)
    scratch = []
    if use_scratch:
        scratch.append(pltpu.VMEM((bm, k), BF16))
        est += _nbytes((bm, k), BF16) + 2 * _nbytes((bm, a_list[0][1]), F32)

    counts = (len(a_list), len(avecs), len(pw_list), len(w_list), len(e_blocks), len(evecs))
    kern = functools.partial(_mm_kernel, counts=counts, prologue=prologue, epilogue=epilogue, use_scratch=use_scratch)
    return pl.pallas_call(
        kern,
        grid=(m // bm, n // bn),
        in_specs=in_specs,
        out_specs=pl.BlockSpec((bm, bn), lambda i, j: (i, j)),
        out_shape=jax.ShapeDtypeStruct((m, n), out_dtype),
        scratch_shapes=scratch,
        compiler_params=_params(("parallel", "arbitrary"), est),
        name=name,
    )(*args)


def _full(arr):
    return (arr, arr.shape[1], 0)


def _row(v):
    return v.reshape(1, -1)


def _rmsnorm_kernel(x_ref, g_ref, o_ref):
    o_ref[...] = (_rms(x_ref[...]) * g_ref[...]).astype(o_ref.dtype)


def _rmsnorm(x, g, out_dtype):
    m, d = x.shape
    bm = _pick_bm(m, 1088)
    est = 2 * _nbytes((bm, d), F32) * 3
    return pl.pallas_call(
        _rmsnorm_kernel,
        grid=(m // bm,),
        in_specs=[pl.BlockSpec((bm, d), lambda i: (i, 0)), pl.BlockSpec((1, d), lambda i: (0, 0))],
        out_specs=pl.BlockSpec((bm, d), lambda i: (i, 0)),
        out_shape=jax.ShapeDtypeStruct((m, d), out_dtype),
        compiler_params=_params(("parallel",), est),
        name="rmsnorm",
    )(x, _row(g))


def _ffn(x, g, w_gu, w_down, lead):
    d_ff = w_down.shape[-2]
    h = _mm([_full(x)], [(w_gu, lead, 0), (w_gu, lead, d_ff)], n=d_ff, out_dtype=BF16, avecs=[_row(g)],
            prologue=lambda a, v, pw: _rms(a[0]) * v[0],
            epilogue=lambda accs, e, ev: accs[0] * jax.nn.sigmoid(accs[0]) * accs[1], name="ffn_up")
    return _mm([_full(h)], [(w_down, lead, 0)], n=x.shape[1], out_dtype=F32, e_blocks=[x],
               epilogue=lambda accs, e, ev: e[0] + 0.5 * accs[0], name="ffn_down")


def _rw_mix_kernel(x_ref, g_ref, mu_ref, ov_ref, h_ref, *rest, n_prompt_blocks, blocks_per_seq, ts):
    outs, carry_ref = rest[:-1], rest[-1]
    pid = pl.program_id(0)

    @pl.when(pid == 0)
    def _():
        carry_ref[...] = jnp.zeros_like(carry_ref)

    h = _rms(x_ref[...]) * g_ref[...]
    h_ref[...] = h
    bm = h.shape[0]
    rolled = pltpu.roll(h, 1, axis=0)
    row = lax.broadcasted_iota(jnp.int32, h.shape, 0)
    carry = jnp.where(pid % blocks_per_seq == 0, 0.0, carry_ref[...])
    prev_prompt = jnp.where(row == 0, carry, rolled)
    prev_sample = jnp.where(row % ts == 0, ov_ref[...], rolled)
    prev = jnp.where(pid >= n_prompt_blocks, prev_sample, prev_prompt)
    carry_ref[...] = h[bm - 1:bm, :]
    delta = prev - h
    for i, o in enumerate(outs):
        o[...] = (h + delta * mu_ref[i:i + 1, :]).astype(BF16)


def _rw_mix(x, g, mu, shift_sample, groups):
    m, d = x.shape
    (_, bp, tp), (r0s, bs, ts) = groups
    bm = _pick_bm(tp, 512, also_divides=(bs * ts,))
    assert bm % ts == 0 and r0s == bp * tp
    n_prompt_blocks = (bp * tp) // bm
    ov = jnp.zeros((bs, ts, d), F32).at[:, 0].set(shift_sample.astype(F32)).reshape(bs * ts, d)
    n_mix = mu.shape[0]
    blk = pl.BlockSpec((bm, d), lambda i: (i, 0))
    est = 2 * _nbytes((bm, d), F32) * 3 + 2 * n_mix * _nbytes((bm, d), BF16) + 6 * _nbytes((bm, d), F32)
    kern = functools.partial(_rw_mix_kernel, n_prompt_blocks=n_prompt_blocks, blocks_per_seq=tp // bm, ts=ts)
    outs = pl.pallas_call(
        kern,
        grid=(m // bm,),
        in_specs=[blk, pl.BlockSpec((1, d), lambda i: (0, 0)), pl.BlockSpec((n_mix, d), lambda i: (0, 0)),
                  pl.BlockSpec((bm, d), lambda i: (jnp.maximum(i - n_prompt_blocks, 0), 0))],
        out_specs=[blk] * (1 + n_mix),
        out_shape=[jax.ShapeDtypeStruct((m, d), F32)] + [jax.ShapeDtypeStruct((m, d), BF16)] * n_mix,
        scratch_shapes=[pltpu.VMEM((1, d), F32)],
        compiler_params=_params(("arbitrary",), est),
        name="rw_mix",
    )(x, _row(g), mu, ov)
    return outs[0], outs[1:]


def _rwkv_steps(r_ref, k_ref, v_ref, w_ref, a_ref, y_ref, param_refs, s_ref, tb, nh):
    kk_p, ka_p, rk_p, ln_g, ln_b = (q[...] for q in param_refs)
    vgrp = V7X_SUBLANES

    def step(t, carry):
        r = r_ref[t]
        k = k_ref[t]
        v = v_ref[t]
        logw = -jax.nn.softplus(-w_ref[t]) - 0.5
        d = jnp.exp(-jnp.exp(logw))
        a = jax.nn.sigmoid(a_ref[t])
        kk = k * kk_p
        kk = kk / jnp.maximum(jnp.sqrt(jnp.sum(kk * kk, axis=0, keepdims=True)), 1e-12)
        kmod = k * (1.0 + (a - 1.0) * ka_p)
        b = kk * a

        def vgroup(g, c2):
            base = pl.multiple_of(g * vgrp, vgrp)
            v8 = v_ref[t, pl.ds(base, vgrp), :]
            ys = []
            for j in range(vgrp):
                s_old = s_ref[base + j]
                sa = jnp.sum(s_old * kk, axis=0, keepdims=True)
                s_new = s_old * d - sa * b + v8[j:j + 1, :] * kmod
                s_ref[base + j] = s_new
                ys.append(jnp.sum(s_new * r, axis=0, keepdims=True))
            y_ref[t, pl.ds(base, vgrp), :] = jnp.concatenate(ys, axis=0)
            return c2

        lax.fori_loop(0, nh // vgrp, vgroup, 0)

        y = y_ref[t]
        mean = jnp.mean(y, axis=0, keepdims=True)
        var = jnp.mean(jnp.square(y - mean), axis=0, keepdims=True)
        yn = (y - mean) * lax.rsqrt(var + RW_GN_EPS) * ln_g + ln_b
        bonus = jnp.sum(r * kmod * rk_p, axis=0, keepdims=True) * v
        y_ref[t] = yn + bonus
        return carry

    lax.fori_loop(0, tb, step, 0)


def _rwkv_scan_kernel(r_ref, k_ref, v_ref, w_ref, a_ref, kk_p_ref, ka_p_ref, rk_p_ref, lg_ref, lb_ref, s0_ref,
                      y_ref, sT_ref, s_ref, *, tb, nh):
    tblk = pl.program_id(1)

    @pl.when(tblk == 0)
    def _():
        s_ref[...] = s0_ref[...]

    _rwkv_steps(r_ref, k_ref, v_ref, w_ref, a_ref, y_ref, (kk_p_ref, ka_p_ref, rk_p_ref, lg_ref, lb_ref), s_ref, tb, nh)

    @pl.when(tblk == pl.num_programs(1) - 1)
    def _():
        sT_ref[...] = s_ref[...]


RW_PAIR_ROWS = 16
RW_RELAYOUT_STEPS = 2


def _rows_to_slabs(x_refs, dst_ref, t0):
    half = V7X_LANES // 2
    low = lax.broadcasted_iota(jnp.int32, (RW_HEAD, V7X_LANES), 1) < half
    tile = jnp.concatenate([x_ref[t0 + t] for t in range(RW_RELAYOUT_STEPS) for x_ref in x_refs], axis=0)
    tt = tile.T
    par0, par1 = tt[:RW_HEAD], tt[RW_HEAD:]
    dst_ref[t0] = jnp.where(low, par0, pltpu.roll(par1, half, axis=1))
    dst_ref[t0 + 1] = jnp.where(low, pltpu.roll(par0, half, axis=1), par1)


def _slabs_to_rows(src_ref, y_ref, t0):
    half = V7X_LANES // 2
    low = lax.broadcasted_iota(jnp.int32, (RW_HEAD, V7X_LANES), 1) < half
    even, odd = src_ref[t0], src_ref[t0 + 1]
    par0 = jnp.where(low, even, pltpu.roll(odd, half, axis=1))
    par1 = jnp.where(low, pltpu.roll(even, half, axis=1), odd)
    rows = jnp.concatenate([par0, par1], axis=0).T
    nb = y_ref.shape[0]
    for t in range(RW_RELAYOUT_STEPS):
        for b in range(nb):
            r0 = (t * nb + b) * RW_PAIR_ROWS
            y_ref[b, t0 + t] = rows[r0:r0 + RW_PAIR_ROWS]


def _rwkv_scan_rows_kernel(*refs, tb, nh, nb):
    x_refs = [refs[i * nb:(i + 1) * nb] for i in range(5)]
    param_refs = refs[5 * nb:5 * nb + 5]
    y_ref, sT_ref = refs[5 * nb + 5:5 * nb + 7]
    slab_refs = refs[5 * nb + 7:5 * nb + 13]
    s_ref = refs[5 * nb + 13]
    tblk = pl.program_id(0)

    @pl.when(tblk == 0)
    def _():
        s_ref[...] = jnp.zeros_like(s_ref)

    n = RW_RELAYOUT_STEPS

    def load(sb, carry):
        t0 = pl.multiple_of(sb * n, n)
        for xr, dst in zip(x_refs, slab_refs[:5]):
            _rows_to_slabs(xr, dst, t0)
        return carry

    lax.fori_loop(0, tb // n, load, 0)
    _rwkv_steps(*slab_refs[:5], slab_refs[5], param_refs, s_ref, tb, nh)

    def store(sb, carry):
        _slabs_to_rows(slab_refs[5], y_ref, pl.multiple_of(sb * n, n))
        return carry

    lax.fori_loop(0, tb // n, store, 0)

    @pl.when(tblk == pl.num_programs(0) - 1)
    def _():
        sT_ref[...] = s_ref[...]


def _rwkv_scan_rows(arrs, params, row0, bsz, t, *, tb):
    m, d = arrs[0].shape
    nh, lanes, pr = RW_HEAD, V7X_LANES, RW_PAIR_ROWS
    assert d == pr * lanes and bsz * 2 * pr == lanes and t % tb == 0 and row0 % tb == 0 and tb % RW_RELAYOUT_STEPS == 0
    nblk = t // tb
    in_specs, args = [], []
    for a in arrs:
        a3 = a.reshape(m, pr, lanes)
        for b in range(bsz):
            in_specs.append(pl.BlockSpec((tb, pr, lanes),
                                         functools.partial(lambda ti, o: (o + ti, 0, 0), o=row0 // tb + b * nblk)))
            args.append(a3)
    in_specs += [pl.BlockSpec((nh, lanes), lambda ti: (0, 0))] * 5
    args += list(params)
    est = (5 * bsz + bsz) * 2 * _nbytes((tb, pr, lanes), F32) + 6 * _nbytes((tb, nh, lanes), F32) \
        + 3 * _nbytes((nh, nh, lanes), F32) + (4 << 20)
    kern = functools.partial(_rwkv_scan_rows_kernel, tb=tb, nh=nh, nb=bsz)
    y, s_fin = pl.pallas_call(
        kern,
        grid=(nblk,),
        in_specs=in_specs,
        out_specs=[pl.BlockSpec((bsz, tb, pr, lanes), lambda ti: (0, ti, 0, 0)),
                   pl.BlockSpec((nh, nh, lanes), lambda ti: (0, 0, 0))],
        out_shape=[jax.ShapeDtypeStruct((bsz, t, pr, lanes), F32), jax.ShapeDtypeStruct((nh, nh, lanes), F32)],
        scratch_shapes=[pltpu.VMEM((tb, nh, lanes), F32)] * 6 + [pltpu.VMEM((nh, nh, lanes), F32)],
        compiler_params=_params(("arbitrary",), est),
        name="rwkv_scan_rows",
    )(*args)
    return y.reshape(bsz * t, d), s_fin


def _rwkv_scan(r, k, v, w_pre, a_pre, kk_p, ka_p, rk_p, ln_g, ln_b, s0, *, tb):
    t, nh, c = r.shape
    lanes = V7X_LANES
    assert c % lanes == 0 and t % tb == 0
    seq = pl.BlockSpec((tb, nh, lanes), lambda ci, ti: (ti, 0, ci))
    par = pl.BlockSpec((nh, lanes), lambda ci, ti: (0, ci))
    st = pl.BlockSpec((nh, nh, lanes), lambda ci, ti: (0, 0, ci))
    est = 6 * 2 * _nbytes((tb, nh, lanes), F32) + 5 * _nbytes((nh, nh, lanes), F32)
    kern = functools.partial(_rwkv_scan_kernel, tb=tb, nh=nh)
    return pl.pallas_call(
        kern,
        grid=(c // lanes, t // tb),
        in_specs=[seq] * 5 + [par] * 5 + [st],
        out_specs=[seq, st],
        out_shape=[jax.ShapeDtypeStruct((t, nh, c), F32), jax.ShapeDtypeStruct((nh, nh, c), F32)],
        scratch_shapes=[pltpu.VMEM((nh, nh, lanes), F32)],
        compiler_params=_params(("parallel", "arbitrary"), est),
        name="rwkv_scan",
    )(r, k, v, w_pre, a_pre, kk_p, ka_p, rk_p, ln_g, ln_b, s0)


def _to_chains(x, bsz, t, nheads):
    return x.reshape(bsz, t, nheads, RW_HEAD).transpose(1, 3, 0, 2).reshape(t, RW_HEAD, bsz * nheads)


def _from_chains(y, bsz, t, nheads):
    return y.reshape(t, RW_HEAD, bsz, nheads).transpose(2, 0, 3, 1).reshape(bsz * t, nheads * RW_HEAD)


def _rwkv_group(r, k, v, w_pre, a_pre, wkv0, p, bsz, t):
    d_model = r.shape[1]
    nheads = d_model // RW_HEAD
    c = bsz * nheads
    chain_param = lambda q: jnp.tile(q.reshape(nheads, RW_HEAD).T, (1, bsz))
    if wkv0 is None:
        s0 = jnp.zeros((RW_HEAD, RW_HEAD, c), F32)
    else:
        s0 = wkv0.astype(F32).transpose(2, 3, 0, 1).reshape(RW_HEAD, RW_HEAD, c)
    tb = t if t <= 32 else 32
    y, s_fin = _rwkv_scan(*[_to_chains(q, bsz, t, nheads) for q in (r, k, v, w_pre, a_pre)],
                          *[chain_param(p[name]) for name in ('rw_k_k', 'rw_k_a', 'rw_r_k', 'rw_ln_g', 'rw_ln_b')],
                          s0, tb=tb)
    s_fin = s_fin.reshape(RW_HEAD, RW_HEAD, bsz, nheads).transpose(2, 3, 0, 1)
    return _from_chains(y, bsz, t, nheads), s_fin


def _rwkv_group_rows(arrs, p, row0, bsz, t):
    pr = RW_PAIR_ROWS

    def chain_param(q):
        slab = q.reshape(pr, 2, RW_HEAD).transpose(2, 1, 0)[:, :, None, :]
        return jnp.broadcast_to(slab, (RW_HEAD, 2, bsz, pr)).reshape(RW_HEAD, 2 * bsz * pr)

    params = [chain_param(p[name]) for name in ('rw_k_k', 'rw_k_a', 'rw_r_k', 'rw_ln_g', 'rw_ln_b')]
    y, s_fin = _rwkv_scan_rows(arrs, params, row0, bsz, t, tb=32)
    s_fin = s_fin.reshape(RW_HEAD, RW_HEAD, 2, bsz, pr).transpose(3, 4, 2, 0, 1)
    return y, s_fin.reshape(bsz, 2 * pr, RW_HEAD, RW_HEAD)


def _rwkv_layer(x, groups, states, p, wb):
    d = x.shape[1]
    h, (xr, xw, xk, xv, xa, xg) = _rw_mix(x, p['norm_g'], p['rw_mu'], states[1][1], groups)
    w_rkv = wb['rw_w_rkv']
    r = _mm([_full(xr)], [(w_rkv, (0,), 0)], n=d, out_dtype=F32, name="rw_r")
    k = _mm([_full(xk)], [(w_rkv, (1,), 0)], n=d, out_dtype=F32, name="rw_k")
    v = _mm([_full(xv)], [(w_rkv, (2,), 0)], n=d, out_dtype=F32, name="rw_v")
    add_vec = lambda accs, e, ev: ev[0] + accs[0]
    lora = lambda act: (lambda a, v_, pw: act(jnp.dot(a[0], pw[0], preferred_element_type=F32)))
    w_pre = _mm([_full(xw)], [(wb['rw_w2'], (), 0)], n=d, out_dtype=F32, pw_list=[wb['rw_w1']],
                prologue=lora(jnp.tanh), evecs=[_row(p['rw_w0'])], epilogue=add_vec, name="rw_w")
    a_pre = _mm([_full(xa)], [(wb['rw_a2'], (), 0)], n=d, out_dtype=F32, pw_list=[wb['rw_a1']],
                prologue=lora(lambda z: z), evecs=[_row(p['rw_a0'])], epilogue=add_vec, name="rw_a")
    g = _mm([_full(xg)], [(wb['rw_g2'], (), 0)], n=d, out_dtype=F32, pw_list=[wb['rw_g1']],
            prologue=lora(jax.nn.sigmoid), name="rw_g")

    ys, new_states = [], []
    for (r0, b, t), st in zip(groups, states):
        sl = slice(r0, r0 + b * t)
        if st[0] is None:
            y, s_fin = _rwkv_group_rows((r, k, v, w_pre, a_pre), p, r0, b, t)
        else:
            y, s_fin = _rwkv_group(r[sl], k[sl], v[sl], w_pre[sl], a_pre[sl], st[0], p, b, t)
        ys.append(y)
        new_states.append((s_fin, h[sl].reshape(b, t, d)[:, -1]))
    y = jnp.concatenate(ys, axis=0)
    x = _mm([_full(y), _full(g)], [(wb['rw_w_o'], (), 0)], n=d, out_dtype=F32,
            prologue=lambda a, v_, pw: a[0] * a[1], e_blocks=[x],
            epilogue=lambda accs, e, ev: e[0] + accs[0], name="rw_out")
    return x, new_states


def _mlstm_kernel(q_ref, k_ref, v_ref, gt_ref, gtt_ref, bg_ref, bgt_ref, ng_ref, c0_ref, n0_ref, m0_ref, *rest,
                  nheads, dk, dv, chunk, bb, aliased):
    if aliased:
        rest = rest[1:]
    h_ref, cT_ref, nT_ref, mT_ref, c_ref, n_ref, m_ref = rest
    ci = pl.program_id(1)

    @pl.when(ci == 0)
    def _():
        c_ref[...] = c0_ref[...]
        n_ref[...] = n0_ref[...]
        m_ref[...] = m0_ref[...]

    L = chunk
    cap = lambda z: ML_GATE_CAP * jnp.tanh(z / ML_GATE_CAP)
    gates_c_all = cap(gt_ref[...] + bg_ref[...])
    gates_r_all = cap(gtt_ref[...] + bgt_ref[...])
    row_id = lax.broadcasted_iota(jnp.int32, (L, L), 0)
    col_id = lax.broadcasted_iota(jnp.int32, (L, L), 1)
    causal = col_id <= row_id
    k_scale = dk ** -0.5

    for bi in range(bb):
        rows = slice(bi * L, (bi + 1) * L)
        gates_c = gates_c_all[rows, :]
        gates_r = gates_r_all[:, rows]
        for hd in range(nheads):
            li_c = gates_c[:, hd:hd + 1]
            li_r = gates_r[hd:hd + 1, :]
            lf_c = jax.nn.log_sigmoid(gates_c[:, nheads + hd:nheads + hd + 1])
            lf_r = jax.nn.log_sigmoid(gates_r[nheads + hd:nheads + hd + 1, :])
            b_c = jnp.sum(jnp.where(causal, lf_r, 0.0), axis=1, keepdims=True)
            b_r = jnp.sum(jnp.where(row_id <= col_id, lf_c, 0.0), axis=0, keepdims=True)
            m_prev = m_ref[bi, hd][:, :1]
            dmat = jnp.where(causal, b_c - b_r + li_r, -jnp.inf)
            inter = b_c + m_prev
            m_t = jnp.maximum(jnp.max(dmat, axis=1, keepdims=True), inter)

            q = q_ref[rows, hd * dk:(hd + 1) * dk]
            k = k_ref[rows, hd * dk:(hd + 1) * dk] * k_scale
            v = v_ref[rows, hd * dv:(hd + 1) * dv]
            qb, kb, vb = q.astype(BF16), k.astype(BF16), v.astype(BF16)
            c_old = c_ref[bi, hd]
            n_old = n_ref[bi, hd]

            s = lax.dot_general(qb, kb, (((1,), (1,)), ((), ())), preferred_element_type=F32) * jnp.exp(dmat - m_t)
            w_inter = jnp.exp(inter - m_t)
            cq = lax.dot_general(qb, c_old.astype(BF16), (((1,), (1,)), ((), ())), preferred_element_type=F32)
            num = jnp.dot(s.astype(BF16), vb, preferred_element_type=F32) + w_inter * cq
            nq = jnp.sum(q * n_old, axis=1, keepdims=True)
            den = jnp.sum(s, axis=1, keepdims=True) + w_inter * nq
            h = num / jnp.maximum(jnp.abs(den), jnp.exp(-m_t))
            hn = h * lax.rsqrt(jnp.mean(h * h, axis=1, keepdims=True) + ML_NORM_EPS)
            h_ref[rows, hd * dv:(hd + 1) * dv] = hn * ng_ref[:, hd * dv:(hd + 1) * dv]

            b_last = b_c[L - 1:L, :]
            ws_log = b_last - b_c + li_c
            m_new = jnp.maximum(b_last + m_prev, jnp.max(ws_log, axis=0, keepdims=True))
            ws = jnp.exp(ws_log - m_new)
            dec = jnp.exp(b_last + m_prev - m_new)
            upd = lax.dot_general((v * ws).astype(BF16), kb, (((0,), (0,)), ((), ())), preferred_element_type=F32)
            c_ref[bi, hd] = dec * c_old + upd
            n_ref[bi, hd] = dec * n_old + jnp.sum(ws * k, axis=0, keepdims=True)
            m_ref[bi, hd] = jnp.broadcast_to(m_new, m_ref.shape[2:])

    @pl.when(ci == pl.num_programs(1) - 1)
    def _():
        cT_ref[...] = c_ref[...]
        nT_ref[...] = n_ref[...]
        mT_ref[...] = m_ref[...]


def _mlstm_group(qkvo, gates, b_gates, norm_g, c0, n0, m0, group, h_prev_out):
    r0, bsz, t = group
    m_rows = qkvo.shape[0]
    nheads = ML_HEADS
    dk = c0.shape[-1]
    dv = c0.shape[-2]
    lanes = V7X_LANES
    chunk = min(t, lanes)
    bb = 1 if chunk >= V7X_SUBLANES else 2 * V7X_SUBLANES // chunk
    rows = bb * chunk
    assert t % chunk == 0 and bsz % bb == 0 and r0 % rows == 0
    nc = t // chunk
    nsteps = (bsz // bb) * nc
    rb0 = r0 // rows
    g_grp = gates[r0:r0 + bsz * t]
    g3t = g_grp.reshape(nsteps, rows, 2 * nheads).transpose(0, 2, 1)
    n0r = n0.astype(F32).reshape(bsz, nheads, 1, dk)
    m0r = jnp.broadcast_to(m0.astype(F32).reshape(bsz, nheads, 1, 1), (bsz, nheads, 1, lanes))
    wq, wv = nheads * dk, nheads * dv
    rowblk = lambda b, c: rb0 + b * nc + c
    seq = lambda w, cb: pl.BlockSpec((rows, w), functools.partial(lambda b, c, cb_: (rowblk(b, c), cb_), cb_=cb))
    st_c = pl.BlockSpec((bb, nheads, dv, dk), lambda b, c: (b, 0, 0, 0))
    st_n = pl.BlockSpec((bb, nheads, 1, dk), lambda b, c: (b, 0, 0, 0))
    st_m = pl.BlockSpec((bb, nheads, 1, lanes), lambda b, c: (b, 0, 0, 0))
    est = 2 * 3 * _nbytes((rows, 2 * wq + 2 * wv), F32) + 5 * bb * _nbytes((nheads, dv, dk), F32) + (8 << 20)
    aliased = h_prev_out is not None
    kern = functools.partial(_mlstm_kernel, nheads=nheads, dk=dk, dv=dv, chunk=chunk, bb=bb, aliased=aliased)
    in_specs = [seq(wq, 0), seq(wq, 1), seq(wv, wq * 2 // wv), pl.BlockSpec((rows, 2 * nheads), lambda b, c: (rowblk(b, c), 0)),
                pl.BlockSpec((None, 2 * nheads, rows), lambda b, c: (b * nc + c, 0, 0)),
                pl.BlockSpec((1, 2 * nheads), lambda b, c: (0, 0)),
                pl.BlockSpec((2 * nheads, 1), lambda b, c: (0, 0)),
                pl.BlockSpec((1, wv), lambda b, c: (0, 0)),
                st_c, st_n, st_m]
    args = [qkvo, qkvo, qkvo, gates, g3t, _row(b_gates), b_gates.reshape(-1, 1), _row(norm_g),
            c0.astype(F32), n0r, m0r]
    io_alias = {}
    if aliased:
        in_specs.append(pl.BlockSpec(memory_space=pl.ANY))
        args.append(h_prev_out)
        io_alias = {len(args) - 1: 0}
    h, c_fin, n_fin, m_fin = pl.pallas_call(
        kern,
        grid=(bsz // bb, nc),
        in_specs=in_specs,
        out_specs=[pl.BlockSpec((rows, wv), lambda b, c: (rowblk(b, c), 0)), st_c, st_n, st_m],
        out_shape=[jax.ShapeDtypeStruct((m_rows, wv), F32),
                   jax.ShapeDtypeStruct((bsz, nheads, dv, dk), F32),
                   jax.ShapeDtypeStruct((bsz, nheads, 1, dk), F32),
                   jax.ShapeDtypeStruct((bsz, nheads, 1, lanes), F32)],
        scratch_shapes=[pltpu.VMEM((bb, nheads, dv, dk), F32), pltpu.VMEM((bb, nheads, 1, dk), F32),
                        pltpu.VMEM((bb, nheads, 1, lanes), F32)],
        input_output_aliases=io_alias,
        compiler_params=_params(("parallel", "arbitrary"), est),
        name="mlstm_chunk",
    )(*args)
    return h, c_fin, n_fin.reshape(bsz, nheads, dk), m_fin[:, :, 0, 0]


def _mlstm_layer(x, groups, states, p, wb):
    d = x.shape[1]
    nheads = ML_HEADS
    dv = d // nheads
    dk = dv // 2
    n_main = 2 * nheads * dk + 2 * nheads * dv
    xn = _rmsnorm(x, p['norm_g'], BF16)
    qkvo = _mm([_full(xn)], [(wb['ml_w_main'], (), 0)], n=n_main, out_dtype=F32, name="ml_in")
    gates = _mm([_full(xn)], [(wb['ml_w_gates'], (), 0)], n=2 * nheads, out_dtype=F32, name="ml_gates")
    h, new_states = None, []
    for grp, st in zip(groups, states):
        h, c_fin, n_fin, m_fin = _mlstm_group(qkvo, gates, p['ml_b_gates'], p['ml_norm_g'], st[0], st[1], st[2], grp, h)
        new_states.append((c_fin, n_fin, m_fin))
    og_block = (2 * nheads * dk + nheads * dv) // d
    x = _mm([(qkvo, d, og_block), _full(h)], [(wb['ml_w_o'], (), 0)], n=d, out_dtype=F32,
            prologue=lambda a, v_, pw: jax.nn.sigmoid(a[0]) * a[1],
            e_blocks=[x], epilogue=lambda accs, e, ev: e[0] + accs[0], name="ml_out")
    return x, new_states


def _dwconv_kernel(u_ref, buf_ref, w_ref, *rest, ksize, t, bb, rows, halo, aliased):
    if aliased:
        rest = rest[1:]
    y_ref, ext_ref = rest
    w = w_ref[...]
    sub = V7X_SUBLANES
    lead = halo - (ksize - 1)
    n_tiles = t // rows
    taps_by_shift = [[j for j in range(ksize) if (j + lead) % sub == r] for r in range(sub)]
    ext_ref[pl.ds(halo + t, sub), :] = jnp.zeros((sub, ext_ref.shape[1]), F32)

    for b in range(bb):
        ext_ref[pl.ds(lead, ksize - 1), :] = buf_ref[b]
        ext_ref[pl.ds(halo, t), :] = u_ref[pl.ds(b * t, t), :]

        def tile(ti, carry):
            base = ti * rows if isinstance(ti, int) else pl.multiple_of(ti * rows, sub)
            acc = None
            for r in range(sub):
                part = None
                for j in taps_by_shift[r]:
                    q8 = (j + lead) - r
                    term = w[j:j + 1, :] * ext_ref[pl.ds(base + q8, rows + sub), :]
                    part = term if part is None else part + term
                if part is not None:
                    shifted = part[r:r + rows, :]
                    acc = shifted if acc is None else acc + shifted
            y_ref[pl.ds(b * t + base, rows), :] = acc
            return carry

        if n_tiles == 1:
            tile(0, 0)
        else:
            lax.fori_loop(0, n_tiles, tile, 0)


def _dwconv_group(u, buf, w, group, y_prev_out):
    r0, bsz, t = group
    m_rows, c = u.shape
    ksize = w.shape[0]
    sub = V7X_SUBLANES
    halo = _round_up(ksize - 1, sub)
    bc = 256
    bb = 1 if t >= 64 else 8
    rows = min(t, 64)
    blk = bb * t
    assert bsz % bb == 0 and t % rows == 0 and c % bc == 0 and r0 % blk == 0 and blk % sub == 0
    rb0 = r0 // blk
    aliased = y_prev_out is not None
    est = 4 * _nbytes((blk, bc), F32) + _nbytes((halo + t + sub, bc), F32) + (4 << 20)
    kern = functools.partial(_dwconv_kernel, ksize=ksize, t=t, bb=bb, rows=rows, halo=halo, aliased=aliased)
    in_specs = [pl.BlockSpec((blk, bc), lambda b, ci: (rb0 + b, ci)),
                pl.BlockSpec((bb, ksize - 1, bc), lambda b, ci: (b, 0, ci)),
                pl.BlockSpec((ksize, bc), lambda b, ci: (0, ci))]
    args = [u, buf.astype(F32), w]
    io_alias = {}
    if aliased:
        in_specs.append(pl.BlockSpec(memory_space=pl.ANY))
        args.append(y_prev_out)
        io_alias = {3: 0}
    return pl.pallas_call(
        kern,
        grid=(bsz // bb, c // bc),
        in_specs=in_specs,
        out_specs=pl.BlockSpec((blk, bc), lambda b, ci: (rb0 + b, ci)),
        out_shape=jax.ShapeDtypeStruct((m_rows, c), F32),
        scratch_shapes=[pltpu.VMEM((halo + t + sub, bc), F32)],
        input_output_aliases=io_alias,
        compiler_params=_params(("parallel", "parallel"), est),
        name="dwconv",
    )(*args)


def _conv_groups(u, w_dw, groups, bufs):
    y, new_bufs = None, []
    c = u.shape[1]
    km1 = w_dw.shape[0] - 1
    for (r0, b, t), buf in zip(groups, bufs):
        y = _dwconv_group(u, buf, w_dw, (r0, b, t), y)
        u3 = u[r0:r0 + b * t].reshape(b, t, c)
        if t >= km1:
            new_bufs.append(u3[:, t - km1:])
        else:
            new_bufs.append(jnp.concatenate([buf.astype(F32)[:, t:], u3], axis=1))
    return y, new_bufs


def _conformer_layer(x, groups, states, p, wb):
    d = x.shape[1]
    b1 = _row(p['cf_b_pw1'])
    w1 = wb['cf_w_pw1']
    u = _mm([_full(x)], [(w1, (), 0), (w1, (), d)], n=d, out_dtype=F32, avecs=[_row(p['norm_g'])],
            prologue=lambda a, v, pw: _rms(a[0]) * v[0], evecs=[b1[:, :d], b1[:, d:]],
            epilogue=lambda accs, e, ev: (accs[0] + ev[0]) * jax.nn.sigmoid(accs[1] + ev[1]), name="cf_pw1")
    y, new_bufs = _conv_groups(u, p['cf_w_dw'], groups, [st[0] for st in states])

    def ln_silu(a, v, pw):
        z = a[0] + v[0]
        mu = jnp.mean(z, axis=-1, keepdims=True)
        var = jnp.mean(jnp.square(z - mu), axis=-1, keepdims=True)
        z = (z - mu) * lax.rsqrt(var + CF_LN_EPS) * v[1] + v[2]
        return z * jax.nn.sigmoid(z)

    x = _mm([_full(y)], [(wb['cf_w_pw2'], (), 0)], n=d, out_dtype=F32, prologue=ln_silu,
            avecs=[_row(p['cf_b_dw']), _row(p['cf_ln_g']), _row(p['cf_ln_b'])],
            e_blocks=[x], evecs=[_row(p['cf_b_pw2'])],
            epilogue=lambda accs, e, ev: e[0] + (accs[0] + ev[0]), name="cf_pw2")
    return x, [(nb,) for nb in new_bufs]


def _sconv_layer(x, groups, states, p, wb):
    d = x.shape[1]
    xn = _rmsnorm(x, p['norm_g'], BF16)
    w_in = wb['sc_w_in']
    b_gate = _mm([_full(xn)], [(w_in, (), 0)], n=d, out_dtype=F32, name="sc_b")
    ch = _mm([_full(xn)], [(w_in, (), d), (w_in, (), 2 * d)], n=d, out_dtype=F32,
             epilogue=lambda accs, e, ev: accs[0] * accs[1], name="sc_ch")
    y, new_bufs = _conv_groups(ch, p['sc_w_dw'], groups, [st[0] for st in states])
    x = _mm([_full(b_gate), _full(y)], [(wb['sc_w_out'], (), 0)], n=d, out_dtype=F32,
            prologue=lambda a, v, pw: a[0] * a[1], e_blocks=[x],
            epilogue=lambda accs, e, ev: e[0] + accs[0], name="sc_out")
    return x, [(nb,) for nb in new_bufs]


def kernel(x_prompt, x_sample, state_rwkv_wkv, state_rwkv_shift, state_mlstm_C, state_mlstm_n, state_mlstm_m,
           state_conf_conv, state_sconv_conv, norm_g, ffn_w_gu, ffn_w_down, final_norm_g,
           rw_mu, rw_w_rkv, rw_w0, rw_w1, rw_w2, rw_a0, rw_a1, rw_a2, rw_g1, rw_g2, rw_k_k, rw_k_a, rw_r_k,
           rw_ln_g, rw_ln_b, rw_w_o, ml_w_in, ml_b_gates, ml_norm_g, ml_w_o,
           cf_w_pw1, cf_b_pw1, cf_w_dw, cf_b_dw, cf_ln_g, cf_ln_b, cf_w_pw2, cf_b_pw2,
           sc_w_in, sc_w_dw, sc_w_out):
    bp, tp, d = x_prompt.shape
    bs, ts, _ = x_sample.shape
    depth = norm_g.shape[0]
    n_mixers = 4
    groups = [(0, bp, tp), (bp * tp, bs, ts)]
    x = jnp.concatenate([x_prompt.reshape(bp * tp, d), x_sample.reshape(bs * ts, d)], axis=0).astype(F32)

    dv = d // ML_HEADS
    dk = dv // 2
    cf_k = cf_w_dw.shape[1]
    sc_k = sc_w_dw.shape[1]
    zeros = lambda *s: jnp.zeros(s, F32)
    w_gu_b = ffn_w_gu.astype(BF16)
    w_down_b = ffn_w_down.astype(BF16)

    out_states = {name: ([], []) for name in ('wkv', 'shift', 'C', 'n', 'm', 'cf', 'sc')}
    for i in range(depth):
        j = i // n_mixers
        kind = i % n_mixers
        x = _ffn(x, norm_g[i, 0], w_gu_b, w_down_b, (i, 0))
        if kind == 0:
            p = dict(norm_g=norm_g[i, 1], rw_mu=rw_mu[j], rw_w0=rw_w0[j], rw_a0=rw_a0[j], rw_k_k=rw_k_k[j],
                     rw_k_a=rw_k_a[j], rw_r_k=rw_r_k[j], rw_ln_g=rw_ln_g[j], rw_ln_b=rw_ln_b[j])
            wb = dict(rw_w_rkv=rw_w_rkv[j].astype(BF16), rw_w1=rw_w1[j].astype(BF16), rw_w2=rw_w2[j].astype(BF16),
                      rw_a1=rw_a1[j].astype(BF16), rw_a2=rw_a2[j].astype(BF16), rw_g1=rw_g1[j].astype(BF16),
                      rw_g2=rw_g2[j].astype(BF16), rw_w_o=rw_w_o[j].astype(BF16))
            states = [(None, zeros(bp, d)), (state_rwkv_wkv[j], state_rwkv_shift[j].astype(F32))]
            x, new = _rwkv_layer(x, groups, states, p, wb)
            for gi in range(2):
                out_states['wkv'][gi].append(new[gi][0])
                out_states['shift'][gi].append(new[gi][1])
        elif kind == 1:
            p = dict(norm_g=norm_g[i, 1], ml_b_gates=ml_b_gates[j], ml_norm_g=ml_norm_g[j])
            n_main = 2 * ML_HEADS * dk + 2 * ML_HEADS * dv
            wb = dict(ml_w_main=ml_w_in[j][:, :n_main].astype(BF16), ml_w_gates=ml_w_in[j][:, n_main:].astype(BF16),
                      ml_w_o=ml_w_o[j].astype(BF16))
            states = [(zeros(bp, ML_HEADS, dv, dk), zeros(bp, ML_HEADS, dk), zeros(bp, ML_HEADS)),
                      (state_mlstm_C[j], state_mlstm_n[j], state_mlstm_m[j])]
            x, new = _mlstm_layer(x, groups, states, p, wb)
            for gi in range(2):
                out_states['C'][gi].append(new[gi][0])
                out_states['n'][gi].append(new[gi][1])
                out_states['m'][gi].append(new[gi][2])
        elif kind == 2:
            p = dict(norm_g=norm_g[i, 1], cf_b_pw1=cf_b_pw1[j], cf_w_dw=cf_w_dw[j], cf_b_dw=cf_b_dw[j],
                     cf_ln_g=cf_ln_g[j], cf_ln_b=cf_ln_b[j], cf_b_pw2=cf_b_pw2[j])
            wb = dict(cf_w_pw1=cf_w_pw1[j].astype(BF16), cf_w_pw2=cf_w_pw2[j].astype(BF16))
            states = [(zeros(bp, cf_k - 1, d),), (state_conf_conv[j],)]
            x, new = _conformer_layer(x, groups, states, p, wb)
            for gi in range(2):
                out_states['cf'][gi].append(new[gi][0])
        else:
            p = dict(norm_g=norm_g[i, 1], sc_w_dw=sc_w_dw[j])
            wb = dict(sc_w_in=sc_w_in[j].astype(BF16), sc_w_out=sc_w_out[j].astype(BF16))
            states = [(zeros(bp, sc_k - 1, d),), (state_sconv_conv[j],)]
            x, new = _sconv_layer(x, groups, states, p, wb)
            for gi in range(2):
                out_states['sc'][gi].append(new[gi][0])
        x = _ffn(x, norm_g[i, 2], w_gu_b, w_down_b, (i, 1))

    y = _rmsnorm(x, final_norm_g, F32)
    y_prompt = y[:bp * tp].reshape(bp, tp, d).astype(x_prompt.dtype)
    y_sample = y[bp * tp:].reshape(bs, ts, d).astype(x_sample.dtype)
    order = ('wkv', 'shift', 'C', 'n', 'm', 'cf', 'sc')
    dt = x_prompt.dtype
    prompt_out = tuple(jnp.stack(out_states[name][0]).astype(dt) for name in order)
    sample_out = tuple(jnp.stack(out_states[name][1]).astype(st.dtype) for name, st in
                       zip(order, (state_rwkv_wkv, state_rwkv_shift, state_mlstm_C, state_mlstm_n, state_mlstm_m,
                                   state_conf_conv, state_sconv_conv)))
    return (y_prompt, y_sample) + prompt_out + sample_out
```

```python
import functools
import math

import jax
import jax.numpy as jnp
from jax import lax
from jax.experimental import pallas as pl
from jax.experimental.pallas import tpu as pltpu

F32 = jnp.float32
BF16 = jnp.bfloat16

NORM_EPS = 1e-6
RW_HEAD = 64
RW_GN_EPS = 64e-5
ML_HEADS = 8
ML_GATE_CAP = 15.0
ML_NORM_EPS = 1e-6
CF_LN_EPS = 1e-5

V7X_LANES = 128
V7X_SUBLANES = 8
V7X_VMEM_BYTES = 64 * 1024 * 1024
VMEM_LIMIT_CAP = V7X_VMEM_BYTES - 6 * 1024 * 1024


def _nbytes(shape, dtype):
    return math.prod(shape) * jnp.dtype(dtype).itemsize


def _params(sem, vmem_est):
    limit = int(min(max(vmem_est * 1.2 + (2 << 20), 16 << 20), VMEM_LIMIT_CAP))
    return pltpu.CompilerParams(dimension_semantics=sem, vmem_limit_bytes=limit)


def _pick_bm(m, target, also_divides=()):
    best = None
    for bm in range(16, min(m, target) + 1, 16):
        if m % bm == 0 and all(o % bm == 0 for o in also_divides):
            best = bm
    assert best is not None
    return best


def _rms(x):
    return x * lax.rsqrt(jnp.mean(x * x, axis=-1, keepdims=True) + NORM_EPS)


def _round_up(x, m):
    return -(-x // m) * m


def _mm_kernel(*refs, counts, prologue, epilogue, use_scratch):
    n_a, n_av, n_pw, n_w, n_e, n_ev = counts
    pos = 0
    groups = []
    for c in counts:
        groups.append(refs[pos:pos + c])
        pos += c
    a_refs, av_refs, pw_refs, w_refs, e_refs, ev_refs = groups
    o_ref = refs[pos]
    if use_scratch:
        lhs_ref = refs[-1]

        @pl.when(pl.program_id(1) == 0)
        def _():
            lhs_ref[...] = prologue([r[...] for r in a_refs], [r[...] for r in av_refs],
                                    [r[...] for r in pw_refs]).astype(BF16)

        lhs = lhs_ref[...]
    else:
        lhs = a_refs[0][...]
    accs = [jnp.dot(lhs, w[...], preferred_element_type=F32) for w in w_refs]
    out = epilogue(accs, [r[...] for r in e_refs], [r[...] for r in ev_refs])
    o_ref[...] = out.astype(o_ref.dtype)


def _mm(a_list, w_list, *, n, out_dtype, prologue=None, epilogue=None, avecs=(), pw_list=(), e_blocks=(), evecs=(),
        bm_target=1088, bn=512, name="mm"):
    m = a_list[0][0].shape[0]
    k = w_list[0][0].shape[-2]
    bm = _pick_bm(m, bm_target)
    bn = min(bn, n)
    assert n % bn == 0
    use_scratch = prologue is not None or a_list[0][0].dtype != BF16
    if prologue is None:
        prologue = lambda a, v, pw: a[0]
    if epilogue is None:
        epilogue = lambda accs, e, ev: accs[0]

    in_specs, args, est = [], [], 0
    for arr, ka, cb in a_list:
        in_specs.append(pl.BlockSpec((bm, ka), functools.partial(lambda i, j, c: (i, c), c=cb)))
        args.append(arr)
        est += 2 * _nbytes((bm, ka), arr.dtype)
    for v in avecs:
        in_specs.append(pl.BlockSpec(v.shape, lambda i, j: (0, 0)))
        args.append(v)
    for pw in pw_list:
        in_specs.append(pl.BlockSpec(pw.shape, lambda i, j: (0, 0)))
        args.append(pw)
        est += 2 * _nbytes(pw.shape, pw.dtype)
    for arr, lead, off in w_list:
        assert off % bn == 0 and arr.shape[-2] == k
        in_specs.append(pl.BlockSpec((None,) * len(lead) + (k, bn),
                                     functools.partial(lambda i, j, l, o: l + (0, j + o), l=tuple(lead), o=off // bn)))
        args.append(arr)
        est += 2 * _nbytes((k, bn), BF16)
    for e in e_blocks:
        in_specs.append(pl.BlockSpec((bm, bn), lambda i, j: (i, j)))
        args.append(e)
        est += 2 * _nbytes((bm, bn), e.dtype)
    for v in evecs:
        in_specs.append(pl.BlockSpec((1, bn), lambda i, j: (0, j)))
        args.append(v)
    est += 2 * _nbytes((bm, bn), out_dtype) + (len(w_list) + 2) * _nbytes((bm, bn), F32)
    scratch = []
    if use_scratch:
        scratch.append(pltpu.VMEM((bm, k), BF16))
        est += _nbytes((bm, k), BF16) + 2 * _nbytes((bm, a_list[0][1]), F32)

    counts = (len(a_list), len(avecs), len(pw_list), len(w_list), len(e_blocks), len(evecs))
    kern = functools.partial(_mm_kernel, counts=counts, prologue=prologue, epilogue=epilogue, use_scratch=use_scratch)
    return pl.pallas_call(
        kern,
        grid=(m // bm, n // bn),
        in_specs=in_specs,
        out_specs=pl.BlockSpec((bm, bn), lambda i, j: (i, j)),
        out_shape=jax.ShapeDtypeStruct((m, n), out_dtype),
        scratch_shapes=scratch,
        compiler_params=_params(("parallel", "arbitrary"), est),
        name=name,
    )(*args)


def _full(arr):
    return (arr, arr.shape[1], 0)


def _row(v):
    return v.reshape(1, -1)


def _rmsnorm_kernel(x_ref, g_ref, o_ref):
    o_ref[...] = (_rms(x_ref[...]) * g_ref[...]).astype(o_ref.dtype)


def _rmsnorm(x, g, out_dtype, row0=0, nrows=None):
    d = x.shape[1]
    m = x.shape[0] if nrows is None else nrows
    bm = _pick_bm(m, 1088, also_divides=(row0,) if row0 else ())
    rb0 = row0 // bm
    est = 2 * _nbytes((bm, d), F32) * 3
    return pl.pallas_call(
        _rmsnorm_kernel,
        grid=(m // bm,),
        in_specs=[pl.BlockSpec((bm, d), lambda i: (rb0 + i, 0)), pl.BlockSpec((1, d), lambda i: (0, 0))],
        out_specs=pl.BlockSpec((bm, d), lambda i: (i, 0)),
        out_shape=jax.ShapeDtypeStruct((m, d), out_dtype),
        compiler_params=_params(("parallel",), est),
        name="rmsnorm",
    )(x, _row(g))


def _ffn(x, g, w_gu, w_down, lead):
    d_ff = w_down.shape[-2]
    h = _mm([_full(x)], [(w_gu, lead, 0), (w_gu, lead, d_ff)], n=d_ff, out_dtype=BF16, avecs=[_row(g)],
            prologue=lambda a, v, pw: _rms(a[0]) * v[0],
            epilogue=lambda accs, e, ev: accs[0] * jax.nn.sigmoid(accs[0]) * accs[1], name="ffn_up")
    return _mm([_full(h)], [(w_down, lead, 0)], n=x.shape[1], out_dtype=F32, e_blocks=[x],
               epilogue=lambda accs, e, ev: e[0] + 0.5 * accs[0], name="ffn_down")


def _rw_mix_kernel(x_ref, g_ref, mu_ref, ov_ref, h_ref, *rest, n_prompt_blocks, blocks_per_seq, ts):
    outs, carry_ref = rest[:-1], rest[-1]
    pid = pl.program_id(0)

    @pl.when(pid == 0)
    def _():
        carry_ref[...] = jnp.zeros_like(carry_ref)

    h = _rms(x_ref[...]) * g_ref[...]
    h_ref[...] = h
    bm = h.shape[0]
    rolled = pltpu.roll(h, 1, axis=0)
    row = lax.broadcasted_iota(jnp.int32, h.shape, 0)
    carry = jnp.where(pid % blocks_per_seq == 0, 0.0, carry_ref[...])
    prev_prompt = jnp.where(row == 0, carry, rolled)
    prev_sample = jnp.where(row % ts == 0, ov_ref[...], rolled)
    prev = jnp.where(pid >= n_prompt_blocks, prev_sample, prev_prompt)
    carry_ref[...] = h[bm - 1:bm, :]
    delta = prev - h
    for i, o in enumerate(outs):
        o[...] = (h + delta * mu_ref[i:i + 1, :]).astype(BF16)


def _rw_mix(x, g, mu, shift_sample, groups):
    m, d = x.shape
    (_, bp, tp), (r0s, bs, ts) = groups
    bm = _pick_bm(tp, 512, also_divides=(bs * ts,))
    assert bm % ts == 0 and r0s == bp * tp
    n_prompt_blocks = (bp * tp) // bm
    ov = jnp.zeros((bs, ts, d), F32).at[:, 0].set(shift_sample.astype(F32)).reshape(bs * ts, d)
    n_mix = mu.shape[0]
    blk = pl.BlockSpec((bm, d), lambda i: (i, 0))
    est = 2 * _nbytes((bm, d), F32) * 3 + 2 * n_mix * _nbytes((bm, d), BF16) + 6 * _nbytes((bm, d), F32)
    kern = functools.partial(_rw_mix_kernel, n_prompt_blocks=n_prompt_blocks, blocks_per_seq=tp // bm, ts=ts)
    outs = pl.pallas_call(
        kern,
        grid=(m // bm,),
        in_specs=[blk, pl.BlockSpec((1, d), lambda i: (0, 0)), pl.BlockSpec((n_mix, d), lambda i: (0, 0)),
                  pl.BlockSpec((bm, d), lambda i: (jnp.maximum(i - n_prompt_blocks, 0), 0))],
        out_specs=[blk] * (1 + n_mix),
        out_shape=[jax.ShapeDtypeStruct((m, d), F32)] + [jax.ShapeDtypeStruct((m, d), BF16)] * n_mix,
        scratch_shapes=[pltpu.VMEM((1, d), F32)],
        compiler_params=_params(("arbitrary",), est),
        name="rw_mix",
    )(x, _row(g), mu, ov)
    return outs[0], outs[1:]


def _rwkv_steps(r_ref, k_ref, v_ref, w_ref, a_ref, y_ref, param_refs, s_ref, tb, nh):
    kk_p, ka_p, rk_p, ln_g, ln_b = (q[...] for q in param_refs)
    vgrp = V7X_SUBLANES

    def step(t, carry):
        r = r_ref[t]
        k = k_ref[t]
        v = v_ref[t]
        logw = -jax.nn.softplus(-w_ref[t]) - 0.5
        d = jnp.exp(-jnp.exp(logw))
        a = jax.nn.sigmoid(a_ref[t])
        kk = k * kk_p
        kk = kk / jnp.maximum(jnp.sqrt(jnp.sum(kk * kk, axis=0, keepdims=True)), 1e-12)
        kmod = k * (1.0 + (a - 1.0) * ka_p)
        b = kk * a

        def vgroup(g, c2):
            base = pl.multiple_of(g * vgrp, vgrp)
            v8 = v_ref[t, pl.ds(base, vgrp), :]
            ys = []
            for j in range(vgrp):
                s_old = s_ref[base + j]
                sa = jnp.sum(s_old * kk, axis=0, keepdims=True)
                s_new = s_old * d - sa * b + v8[j:j + 1, :] * kmod
                s_ref[base + j] = s_new
                ys.append(jnp.sum(s_new * r, axis=0, keepdims=True))
            y_ref[t, pl.ds(base, vgrp), :] = jnp.concatenate(ys, axis=0)
            return c2

        lax.fori_loop(0, nh // vgrp, vgroup, 0)

        y = y_ref[t]
        mean = jnp.mean(y, axis=0, keepdims=True)
        var = jnp.mean(jnp.square(y - mean), axis=0, keepdims=True)
        yn = (y - mean) * lax.rsqrt(var + RW_GN_EPS) * ln_g + ln_b
        bonus = jnp.sum(r * kmod * rk_p, axis=0, keepdims=True) * v
        y_ref[t] = yn + bonus
        return carry

    lax.fori_loop(0, tb, step, 0)


def _rwkv_scan_kernel(r_ref, k_ref, v_ref, w_ref, a_ref, kk_p_ref, ka_p_ref, rk_p_ref, lg_ref, lb_ref, s0_ref,
                      y_ref, sT_ref, s_ref, *, tb, nh):
    tblk = pl.program_id(1)

    @pl.when(tblk == 0)
    def _():
        s_ref[...] = s0_ref[...]

    _rwkv_steps(r_ref, k_ref, v_ref, w_ref, a_ref, y_ref, (kk_p_ref, ka_p_ref, rk_p_ref, lg_ref, lb_ref), s_ref, tb, nh)

    @pl.when(tblk == pl.num_programs(1) - 1)
    def _():
        sT_ref[...] = s_ref[...]


RW_PAIR_ROWS = 16
RW_RELAYOUT_STEPS = 2


def _rows_to_slabs(x_refs, dst_ref, t0):
    half = V7X_LANES // 2
    low = lax.broadcasted_iota(jnp.int32, (RW_HEAD, V7X_LANES), 1) < half
    tile = jnp.concatenate([x_ref[t0 + t] for t in range(RW_RELAYOUT_STEPS) for x_ref in x_refs], axis=0)
    tt = tile.T
    par0, par1 = tt[:RW_HEAD], tt[RW_HEAD:]
    dst_ref[t0] = jnp.where(low, par0, pltpu.roll(par1, half, axis=1))
    dst_ref[t0 + 1] = jnp.where(low, pltpu.roll(par0, half, axis=1), par1)


def _slabs_to_rows(src_ref, y_ref, t0):
    half = V7X_LANES // 2
    low = lax.broadcasted_iota(jnp.int32, (RW_HEAD, V7X_LANES), 1) < half
    even, odd = src_ref[t0], src_ref[t0 + 1]
    par0 = jnp.where(low, even, pltpu.roll(odd, half, axis=1))
    par1 = jnp.where(low, pltpu.roll(even, half, axis=1), odd)
    rows = jnp.concatenate([par0, par1], axis=0).T
    nb = y_ref.shape[0]
    for t in range(RW_RELAYOUT_STEPS):
        for b in range(nb):
            r0 = (t * nb + b) * RW_PAIR_ROWS
            y_ref[b, t0 + t] = rows[r0:r0 + RW_PAIR_ROWS]


def _rwkv_scan_rows_kernel(*refs, tb, nh, nb):
    x_refs = [refs[i * nb:(i + 1) * nb] for i in range(5)]
    param_refs = refs[5 * nb:5 * nb + 5]
    y_ref, sT_ref = refs[5 * nb + 5:5 * nb + 7]
    slab_refs = refs[5 * nb + 7:5 * nb + 13]
    s_ref = refs[5 * nb + 13]
    tblk = pl.program_id(0)

    @pl.when(tblk == 0)
    def _():
        s_ref[...] = jnp.zeros_like(s_ref)

    n = RW_RELAYOUT_STEPS

    def load(sb, carry):
        t0 = pl.multiple_of(sb * n, n)
        for xr, dst in zip(x_refs, slab_refs[:5]):
            _rows_to_slabs(xr, dst, t0)
        return carry

    lax.fori_loop(0, tb // n, load, 0)
    _rwkv_steps(*slab_refs[:5], slab_refs[5], param_refs, s_ref, tb, nh)

    def store(sb, carry):
        _slabs_to_rows(slab_refs[5], y_ref, pl.multiple_of(sb * n, n))
        return carry

    lax.fori_loop(0, tb // n, store, 0)

    @pl.when(tblk == pl.num_programs(0) - 1)
    def _():
        sT_ref[...] = s_ref[...]


def _rwkv_scan_rows(arrs, params, row0, bsz, t, *, tb):
    m, d = arrs[0].shape
    nh, lanes, pr = RW_HEAD, V7X_LANES, RW_PAIR_ROWS
    assert d == pr * lanes and bsz * 2 * pr == lanes and t % tb == 0 and row0 % tb == 0 and tb % RW_RELAYOUT_STEPS == 0
    nblk = t // tb
    in_specs, args = [], []
    for a in arrs:
        a3 = a.reshape(m, pr, lanes)
        for b in range(bsz):
            in_specs.append(pl.BlockSpec((tb, pr, lanes),
                                         functools.partial(lambda ti, o: (o + ti, 0, 0), o=row0 // tb + b * nblk)))
            args.append(a3)
    in_specs += [pl.BlockSpec((nh, lanes), lambda ti: (0, 0))] * 5
    args += list(params)
    est = (5 * bsz + bsz) * 2 * _nbytes((tb, pr, lanes), F32) + 6 * _nbytes((tb, nh, lanes), F32) \
        + 3 * _nbytes((nh, nh, lanes), F32) + (4 << 20)
    kern = functools.partial(_rwkv_scan_rows_kernel, tb=tb, nh=nh, nb=bsz)
    y, s_fin = pl.pallas_call(
        kern,
        grid=(nblk,),
        in_specs=in_specs,
        out_specs=[pl.BlockSpec((bsz, tb, pr, lanes), lambda ti: (0, ti, 0, 0)),
                   pl.BlockSpec((nh, nh, lanes), lambda ti: (0, 0, 0))],
        out_shape=[jax.ShapeDtypeStruct((bsz, t, pr, lanes), F32), jax.ShapeDtypeStruct((nh, nh, lanes), F32)],
        scratch_shapes=[pltpu.VMEM((tb, nh, lanes), F32)] * 6 + [pltpu.VMEM((nh, nh, lanes), F32)],
        compiler_params=_params(("arbitrary",), est),
        name="rwkv_scan_rows",
    )(*args)
    return y.reshape(bsz * t, d), s_fin


def _rwkv_scan(r, k, v, w_pre, a_pre, kk_p, ka_p, rk_p, ln_g, ln_b, s0, *, tb):
    t, nh, c = r.shape
    lanes = V7X_LANES
    assert c % lanes == 0 and t % tb == 0
    seq = pl.BlockSpec((tb, nh, lanes), lambda ci, ti: (ti, 0, ci))
    par = pl.BlockSpec((nh, lanes), lambda ci, ti: (0, ci))
    st = pl.BlockSpec((nh, nh, lanes), lambda ci, ti: (0, 0, ci))
    est = 6 * 2 * _nbytes((tb, nh, lanes), F32) + 5 * _nbytes((nh, nh, lanes), F32)
    kern = functools.partial(_rwkv_scan_kernel, tb=tb, nh=nh)
    return pl.pallas_call(
        kern,
        grid=(c // lanes, t // tb),
        in_specs=[seq] * 5 + [par] * 5 + [st],
        out_specs=[seq, st],
        out_shape=[jax.ShapeDtypeStruct((t, nh, c), F32), jax.ShapeDtypeStruct((nh, nh, c), F32)],
        scratch_shapes=[pltpu.VMEM((nh, nh, lanes), F32)],
        compiler_params=_params(("parallel", "arbitrary"), est),
        name="rwkv_scan",
    )(r, k, v, w_pre, a_pre, kk_p, ka_p, rk_p, ln_g, ln_b, s0)


def _to_chains(x, bsz, t, nheads):
    return x.reshape(bsz, t, nheads, RW_HEAD).transpose(1, 3, 0, 2).reshape(t, RW_HEAD, bsz * nheads)


def _from_chains(y, bsz, t, nheads):
    return y.reshape(t, RW_HEAD, bsz, nheads).transpose(2, 0, 3, 1).reshape(bsz * t, nheads * RW_HEAD)


def _rwkv_group(r, k, v, w_pre, a_pre, wkv0, p, bsz, t):
    d_model = r.shape[1]
    nheads = d_model // RW_HEAD
    c = bsz * nheads
    chain_param = lambda q: jnp.tile(q.reshape(nheads, RW_HEAD).T, (1, bsz))
    if wkv0 is None:
        s0 = jnp.zeros((RW_HEAD, RW_HEAD, c), F32)
    else:
        s0 = wkv0.astype(F32).transpose(2, 3, 0, 1).reshape(RW_HEAD, RW_HEAD, c)
    tb = t if t <= 32 else 32
    y, s_fin = _rwkv_scan(*[_to_chains(q, bsz, t, nheads) for q in (r, k, v, w_pre, a_pre)],
                          *[chain_param(p[name]) for name in ('rw_k_k', 'rw_k_a', 'rw_r_k', 'rw_ln_g', 'rw_ln_b')],
                          s0, tb=tb)
    s_fin = s_fin.reshape(RW_HEAD, RW_HEAD, bsz, nheads).transpose(2, 3, 0, 1)
    return _from_chains(y, bsz, t, nheads), s_fin


def _rwkv_group_rows(arrs, p, row0, bsz, t):
    pr = RW_PAIR_ROWS

    def chain_param(q):
        slab = q.reshape(pr, 2, RW_HEAD).transpose(2, 1, 0)[:, :, None, :]
        return jnp.broadcast_to(slab, (RW_HEAD, 2, bsz, pr)).reshape(RW_HEAD, 2 * bsz * pr)

    params = [chain_param(p[name]) for name in ('rw_k_k', 'rw_k_a', 'rw_r_k', 'rw_ln_g', 'rw_ln_b')]
    y, s_fin = _rwkv_scan_rows(arrs, params, row0, bsz, t, tb=32)
    s_fin = s_fin.reshape(RW_HEAD, RW_HEAD, 2, bsz, pr).transpose(3, 4, 2, 0, 1)
    return y, s_fin.reshape(bsz, 2 * pr, RW_HEAD, RW_HEAD)


def _rwkv_chunk_kernel(r_ref, k_ref, v_ref, w_ref, a_ref, kkp_ref, kap_ref, rkp_ref, lg_ref, lb_ref,
                       y_ref, sT_ref, s2_ref, *, chunk, npairs, unroll):
    L = chunk
    lanes = V7X_LANES
    nh = RW_HEAD
    ci = pl.program_id(1)

    @pl.when(ci == 0)
    def _():
        s2_ref[...] = jnp.zeros_like(s2_ref)

    lane_l = lax.broadcasted_iota(jnp.int32, (L, lanes), 1)
    row_l = lax.broadcasted_iota(jnp.int32, (L, lanes), 0)
    head0 = lane_l < nh
    n2 = 2 * L
    ti = lax.broadcasted_iota(jnp.int32, (n2, n2), 0)
    si = lax.broadcasted_iota(jnp.int32, (n2, n2), 1)
    strict = si < ti
    incl = si <= ti
    sq_r = lax.broadcasted_iota(jnp.int32, (lanes, lanes), 0) // nh
    sq_c = lax.broadcasted_iota(jnp.int32, (lanes, lanes), 1) // nh
    seg_ones = jnp.where(sq_r == sq_c, 1.0, 0.0).astype(BF16)
    nt = (((1,), (1,)), ((), ()))
    tn = (((0,), (0,)), ((), ()))

    def seg_sum(x):
        hi = x.astype(BF16)
        lo = (x - hi.astype(F32)).astype(BF16)
        return (jnp.dot(hi, seg_ones, preferred_element_type=F32) + jnp.dot(lo, seg_ones, preferred_element_type=F32))

    def stack(x):
        return jnp.concatenate([jnp.where(head0, x, 0.0), jnp.where(head0, 0.0, x)], axis=0)

    dot = lambda x, y: jnp.dot(x, y, preferred_element_type=F32)
    dot_nt = lambda x, y: lax.dot_general(x, y, nt, preferred_element_type=F32)
    dot_tn = lambda x, y: lax.dot_general(x, y, tn, preferred_element_type=F32)

    def wave(ps):
        idx = range(len(ps))
        ls = [pl.ds(p * lanes, lanes) for p in ps]
        r = [r_ref[:, s] for s in ls]
        v = [v_ref[:, s] for s in ls]
        logd, cl, kk0, kmod, a = [], [], [], [], []
        for i in idx:
            k_i = k_ref[:, ls[i]]
            ld = -jnp.exp(-jax.nn.softplus(-w_ref[:, ls[i]]) - 0.5)
            c = ld
            sh = 1
            while sh < L:
                c = c + jnp.where(row_l >= sh, pltpu.roll(c, sh, axis=0), 0.0)
                sh *= 2
            a_i = jax.nn.sigmoid(a_ref[:, ls[i]])
            logd.append(ld)
            cl.append(c)
            a.append(a_i)
            kk0.append(k_i * kkp_ref[:, ls[i]])
            kmod.append(k_i * (1.0 + (a_i - 1.0) * kap_ref[:, ls[i]]))
        nrm2 = [seg_sum(kk0[i] * kk0[i]) for i in idx]
        bonus = [seg_sum(r[i] * kmod[i] * rkp_ref[:, ls[i]]) * v[i] for i in idx]
        x1, x2, vs, d_last = [], [], [], []
        for i in idx:
            kk = kk0[i] / jnp.maximum(jnp.sqrt(nrm2[i]), 1e-12)
            d_inc = jnp.exp(cl[i])
            d_inv = jnp.exp(-cl[i])
            xa = stack(kk * jnp.exp(cl[i] - logd[i]))
            xr = stack(r[i] * d_inc)
            xb = stack(kk * a[i] * d_inv)
            xk = stack(kmod[i] * d_inv)
            x1.append(jnp.concatenate([xa, xr], axis=0).astype(BF16))
            x2.append(jnp.concatenate([xb, xk], axis=0).astype(BF16))
            vs.append(stack(v[i]))
            d_last.append(d_inc[L - 1:L, :])
        s2 = [s2_ref[p] for p in ps]
        g = [dot_nt(x1[i], x2[i]) for i in idx]
        pz = [dot_nt(x1[i], s2[i].astype(BF16)) for i in idx]
        m_k = [jnp.where(strict, g[i][:n2, n2:], 0.0).astype(BF16) for i in idx]
        m_pow = [jnp.where(strict, g[i][:n2, :n2], 0.0).astype(BF16) for i in idx]
        acat = [jnp.concatenate([jnp.where(incl, g[i][n2:, :n2], 0.0), jnp.where(incl, g[i][n2:, n2:], 0.0)],
                                axis=1).astype(BF16) for i in idx]
        vsb = [vs[i].astype(BF16) for i in idx]
        rhs = [pz[i][:n2] + dot(m_k[i], vsb[i]) for i in idx]
        u = [rhs[i] - dot(m_pow[i], rhs[i].astype(BF16)) for i in idx]
        pw = 2
        while pw < L:
            m_pow = [dot(m_pow[i], m_pow[i]).astype(BF16) for i in idx]
            u = [u[i] + dot(m_pow[i], u[i].astype(BF16)) for i in idx]
            pw *= 2
        wcat = [jnp.concatenate([-u[i], vs[i]], axis=0).astype(BF16) for i in idx]
        ys = [pz[i][n2:] + dot(acat[i], wcat[i]) for i in idx]
        upd = [dot_tn(wcat[i], x2[i]) for i in idx]
        for i in idx:
            s2_ref[ps[i]] = (s2[i] + upd[i]) * d_last[i]
        y = [ys[i][:L] + ys[i][L:] for i in idx]
        mean = [seg_sum(y[i]) * (1.0 / nh) for i in idx]
        yc = [y[i] - mean[i] for i in idx]
        var = [seg_sum(yc[i] * yc[i]) * (1.0 / nh) for i in idx]
        for i in idx:
            yn = yc[i] * lax.rsqrt(var[i] + RW_GN_EPS) * lg_ref[:, ls[i]] + lb_ref[:, ls[i]]
            y_ref[:, ls[i]] = yn + bonus[i]

    for p0 in range(0, npairs, unroll):
        wave(list(range(p0, p0 + unroll)))

    @pl.when(ci == pl.num_programs(1) - 1)
    def _():
        for p in range(npairs):
            blk = s2_ref[p]
            sT_ref[2 * p] = blk[:nh, :nh]
            sT_ref[2 * p + 1] = blk[nh:, nh:]


def _rwkv_chunked(arrs, params, row0, bsz, t):
    m, d = arrs[0].shape
    lanes = V7X_LANES
    chunk = min(t, 64)
    npairs = d // lanes
    assert t % chunk == 0 and row0 % chunk == 0 and d % lanes == 0 and lanes == 2 * RW_HEAD
    nc = t // chunk
    rb0 = row0 // chunk
    seq = pl.BlockSpec((chunk, d), lambda b, c: (rb0 + b * nc + c, 0))
    vec = pl.BlockSpec((1, d), lambda b, c: (0, 0))
    est = 6 * 2 * _nbytes((chunk, d), F32) + 3 * _nbytes((npairs, lanes, lanes), F32) + (8 << 20)
    kern = functools.partial(_rwkv_chunk_kernel, chunk=chunk, npairs=npairs, unroll=16)
    return pl.pallas_call(
        kern,
        grid=(bsz, nc),
        in_specs=[seq] * 5 + [vec] * 5,
        out_specs=[seq, pl.BlockSpec((None, 2 * npairs, RW_HEAD, RW_HEAD), lambda b, c: (b, 0, 0, 0))],
        out_shape=[jax.ShapeDtypeStruct((m, d), F32),
                   jax.ShapeDtypeStruct((bsz, 2 * npairs, RW_HEAD, RW_HEAD), F32)],
        scratch_shapes=[pltpu.VMEM((npairs, lanes, lanes), F32)],
        compiler_params=_params(("parallel", "arbitrary"), est),
        name="rwkv_chunk",
    )(*arrs, *[_row(q) for q in params])


def _rwkv_layer(x, groups, states, p, wb):
    d = x.shape[1]
    h, (xr, xw, xk, xv, xa, xg) = _rw_mix(x, p['norm_g'], p['rw_mu'], states[1][1], groups)
    w_rkv = wb['rw_w_rkv']
    r = _mm([_full(xr)], [(w_rkv, (0,), 0)], n=d, out_dtype=F32, name="rw_r")
    k = _mm([_full(xk)], [(w_rkv, (1,), 0)], n=d, out_dtype=F32, name="rw_k")
    v = _mm([_full(xv)], [(w_rkv, (2,), 0)], n=d, out_dtype=F32, name="rw_v")
    add_vec = lambda accs, e, ev: ev[0] + accs[0]
    lora = lambda act: (lambda a, v_, pw: act(jnp.dot(a[0], pw[0], preferred_element_type=F32)))
    w_pre = _mm([_full(xw)], [(wb['rw_w2'], (), 0)], n=d, out_dtype=F32, pw_list=[wb['rw_w1']],
                prologue=lora(jnp.tanh), evecs=[_row(p['rw_w0'])], epilogue=add_vec, name="rw_w")
    a_pre = _mm([_full(xa)], [(wb['rw_a2'], (), 0)], n=d, out_dtype=F32, pw_list=[wb['rw_a1']],
                prologue=lora(lambda z: z), evecs=[_row(p['rw_a0'])], epilogue=add_vec, name="rw_a")
    g = _mm([_full(xg)], [(wb['rw_g2'], (), 0)], n=d, out_dtype=F32, pw_list=[wb['rw_g1']],
            prologue=lora(jax.nn.sigmoid), name="rw_g")

    y, new_states = None, []
    for (r0, b, t), st in zip(groups, states):
        sl = slice(r0, r0 + b * t)
        if st[0] is None:
            assert y is None
            y, s_fin = _rwkv_chunked((r, k, v, w_pre, a_pre),
                                     [p[name] for name in ('rw_k_k', 'rw_k_a', 'rw_r_k', 'rw_ln_g', 'rw_ln_b')], r0, b, t)
        else:
            y_g, s_fin = _rwkv_group(r[sl], k[sl], v[sl], w_pre[sl], a_pre[sl], st[0], p, b, t)
            y = lax.dynamic_update_slice(y, y_g, (r0, 0))
        new_states.append((s_fin, lax.slice(h, (r0 + t - 1, 0), (r0 + b * t, d), (t, 1))))
    x = _mm([_full(y), _full(g)], [(wb['rw_w_o'], (), 0)], n=d, out_dtype=F32,
            prologue=lambda a, v_, pw: a[0] * a[1], e_blocks=[x],
            epilogue=lambda accs, e, ev: e[0] + accs[0], name="rw_out")
    return x, new_states


def _mlstm_kernel(q_ref, k_ref, v_ref, gt_ref, gtt_ref, bg_ref, bgt_ref, ng_ref, c0_ref, n0_ref, m0_ref, *rest,
                  nheads, dk, dv, chunk, bb, aliased):
    if aliased:
        rest = rest[1:]
    h_ref, cT_ref, nT_ref, mT_ref, c_ref, n_ref, m_ref = rest
    ci = pl.program_id(1)

    @pl.when(ci == 0)
    def _():
        c_ref[...] = c0_ref[...]
        n_ref[...] = n0_ref[...]
        m_ref[...] = m0_ref[...]

    L = chunk
    cap = lambda z: ML_GATE_CAP * jnp.tanh(z / ML_GATE_CAP)
    gates_c_all = cap(gt_ref[...] + bg_ref[...])
    gates_r_all = cap(gtt_ref[...] + bgt_ref[...])
    row_id = lax.broadcasted_iota(jnp.int32, (L, L), 0)
    col_id = lax.broadcasted_iota(jnp.int32, (L, L), 1)
    causal = col_id <= row_id
    k_scale = dk ** -0.5

    for bi in range(bb):
        rows = slice(bi * L, (bi + 1) * L)
        gates_c = gates_c_all[rows, :]
        gates_r = gates_r_all[:, rows]
        for hd in range(nheads):
            li_c = gates_c[:, hd:hd + 1]
            li_r = gates_r[hd:hd + 1, :]
            lf_c = jax.nn.log_sigmoid(gates_c[:, nheads + hd:nheads + hd + 1])
            lf_r = jax.nn.log_sigmoid(gates_r[nheads + hd:nheads + hd + 1, :])
            b_c = jnp.sum(jnp.where(causal, lf_r, 0.0), axis=1, keepdims=True)
            b_r = jnp.sum(jnp.where(row_id <= col_id, lf_c, 0.0), axis=0, keepdims=True)
            m_prev = m_ref[bi, hd][:, :1]
            dmat = jnp.where(causal, b_c - b_r + li_r, -jnp.inf)
            inter = b_c + m_prev
            m_t = jnp.maximum(jnp.max(dmat, axis=1, keepdims=True), inter)

            q = q_ref[rows, hd * dk:(hd + 1) * dk]
            k = k_ref[rows, hd * dk:(hd + 1) * dk] * k_scale
            v = v_ref[rows, hd * dv:(hd + 1) * dv]
            qb, kb, vb = q.astype(BF16), k.astype(BF16), v.astype(BF16)
            c_old = c_ref[bi, hd]
            n_old = n_ref[bi, hd]

            s = lax.dot_general(qb, kb, (((1,), (1,)), ((), ())), preferred_element_type=F32) * jnp.exp(dmat - m_t)
            w_inter = jnp.exp(inter - m_t)
            cq = lax.dot_general(qb, c_old.astype(BF16), (((1,), (1,)), ((), ())), preferred_element_type=F32)
            num = jnp.dot(s.astype(BF16), vb, preferred_element_type=F32) + w_inter * cq
            nq = jnp.sum(q * n_old, axis=1, keepdims=True)
            den = jnp.sum(s, axis=1, keepdims=True) + w_inter * nq
            h = num / jnp.maximum(jnp.abs(den), jnp.exp(-m_t))
            hn = h * lax.rsqrt(jnp.mean(h * h, axis=1, keepdims=True) + ML_NORM_EPS)
            h_ref[rows, hd * dv:(hd + 1) * dv] = hn * ng_ref[:, hd * dv:(hd + 1) * dv]

            b_last = b_c[L - 1:L, :]
            ws_log = b_last - b_c + li_c
            m_new = jnp.maximum(b_last + m_prev, jnp.max(ws_log, axis=0, keepdims=True))
            ws = jnp.exp(ws_log - m_new)
            dec = jnp.exp(b_last + m_prev - m_new)
            upd = lax.dot_general((v * ws).astype(BF16), kb, (((0,), (0,)), ((), ())), preferred_element_type=F32)
            c_ref[bi, hd] = dec * c_old + upd
            n_ref[bi, hd] = dec * n_old + jnp.sum(ws * k, axis=0, keepdims=True)
            m_ref[bi, hd] = jnp.broadcast_to(m_new, m_ref.shape[2:])

    @pl.when(ci == pl.num_programs(1) - 1)
    def _():
        cT_ref[...] = c_ref[...]
        nT_ref[...] = n_ref[...]
        mT_ref[...] = m_ref[...]


def _mlstm_group(qkvo, gates, b_gates, norm_g, c0, n0, m0, group, h_prev_out):
    r0, bsz, t = group
    m_rows = qkvo.shape[0]
    nheads = ML_HEADS
    dk = c0.shape[-1]
    dv = c0.shape[-2]
    lanes = V7X_LANES
    chunk = min(t, lanes)
    bb = 1 if chunk >= V7X_SUBLANES else 2 * V7X_SUBLANES // chunk
    rows = bb * chunk
    assert t % chunk == 0 and bsz % bb == 0 and r0 % rows == 0
    nc = t // chunk
    nsteps = (bsz // bb) * nc
    rb0 = r0 // rows
    g_grp = gates[r0:r0 + bsz * t]
    g3t = g_grp.reshape(nsteps, rows, 2 * nheads).transpose(0, 2, 1)
    n0r = n0.astype(F32).reshape(bsz, nheads, 1, dk)
    m0r = jnp.broadcast_to(m0.astype(F32).reshape(bsz, nheads, 1, 1), (bsz, nheads, 1, lanes))
    wq, wv = nheads * dk, nheads * dv
    rowblk = lambda b, c: rb0 + b * nc + c
    seq = lambda w, cb: pl.BlockSpec((rows, w), functools.partial(lambda b, c, cb_: (rowblk(b, c), cb_), cb_=cb))
    st_c = pl.BlockSpec((bb, nheads, dv, dk), lambda b, c: (b, 0, 0, 0))
    st_n = pl.BlockSpec((bb, nheads, 1, dk), lambda b, c: (b, 0, 0, 0))
    st_m = pl.BlockSpec((bb, nheads, 1, lanes), lambda b, c: (b, 0, 0, 0))
    est = 2 * 3 * _nbytes((rows, 2 * wq + 2 * wv), F32) + 5 * bb * _nbytes((nheads, dv, dk), F32) + (8 << 20)
    aliased = h_prev_out is not None
    kern = functools.partial(_mlstm_kernel, nheads=nheads, dk=dk, dv=dv, chunk=chunk, bb=bb, aliased=aliased)
    in_specs = [seq(wq, 0), seq(wq, 1), seq(wv, wq * 2 // wv), pl.BlockSpec((rows, 2 * nheads), lambda b, c: (rowblk(b, c), 0)),
                pl.BlockSpec((None, 2 * nheads, rows), lambda b, c: (b * nc + c, 0, 0)),
                pl.BlockSpec((1, 2 * nheads), lambda b, c: (0, 0)),
                pl.BlockSpec((2 * nheads, 1), lambda b, c: (0, 0)),
                pl.BlockSpec((1, wv), lambda b, c: (0, 0)),
                st_c, st_n, st_m]
    args = [qkvo, qkvo, qkvo, gates, g3t, _row(b_gates), b_gates.reshape(-1, 1), _row(norm_g),
            c0.astype(F32), n0r, m0r]
    io_alias = {}
    if aliased:
        in_specs.append(pl.BlockSpec(memory_space=pl.ANY))
        args.append(h_prev_out)
        io_alias = {len(args) - 1: 0}
    h, c_fin, n_fin, m_fin = pl.pallas_call(
        kern,
        grid=(bsz // bb, nc),
        in_specs=in_specs,
        out_specs=[pl.BlockSpec((rows, wv), lambda b, c: (rowblk(b, c), 0)), st_c, st_n, st_m],
        out_shape=[jax.ShapeDtypeStruct((m_rows, wv), F32),
                   jax.ShapeDtypeStruct((bsz, nheads, dv, dk), F32),
                   jax.ShapeDtypeStruct((bsz, nheads, 1, dk), F32),
                   jax.ShapeDtypeStruct((bsz, nheads, 1, lanes), F32)],
        scratch_shapes=[pltpu.VMEM((bb, nheads, dv, dk), F32), pltpu.VMEM((bb, nheads, 1, dk), F32),
                        pltpu.VMEM((bb, nheads, 1, lanes), F32)],
        input_output_aliases=io_alias,
        compiler_params=_params(("parallel", "arbitrary"), est),
        name="mlstm_chunk",
    )(*args)
    return h, c_fin, n_fin.reshape(bsz, nheads, dk), m_fin[:, :, 0, 0]


def _mlstm_layer(x, groups, states, p, wb):
    d = x.shape[1]
    nheads = ML_HEADS
    dv = d // nheads
    dk = dv // 2
    n_main = 2 * nheads * dk + 2 * nheads * dv
    xn = _rmsnorm(x, p['norm_g'], BF16)
    qkvo = _mm([_full(xn)], [(wb['ml_w_main'], (), 0)], n=n_main, out_dtype=F32, name="ml_in")
    gates = _mm([_full(xn)], [(wb['ml_w_gates'], (), 0)], n=2 * nheads, out_dtype=F32, name="ml_gates")
    h, new_states = None, []
    for grp, st in zip(groups, states):
        h, c_fin, n_fin, m_fin = _mlstm_group(qkvo, gates, p['ml_b_gates'], p['ml_norm_g'], st[0], st[1], st[2], grp, h)
        new_states.append((c_fin, n_fin, m_fin))
    og_block = (2 * nheads * dk + nheads * dv) // d
    x = _mm([(qkvo, d, og_block), _full(h)], [(wb['ml_w_o'], (), 0)], n=d, out_dtype=F32,
            prologue=lambda a, v_, pw: jax.nn.sigmoid(a[0]) * a[1],
            e_blocks=[x], epilogue=lambda accs, e, ev: e[0] + accs[0], name="ml_out")
    return x, new_states


def _dwconv_kernel(u_ref, buf_ref, w_ref, *rest, ksize, t, bb, rows, halo, aliased):
    if aliased:
        rest = rest[1:]
    y_ref, ext_ref = rest
    w = w_ref[...]
    sub = V7X_SUBLANES
    lead = halo - (ksize - 1)
    n_tiles = t // rows
    taps_by_shift = [[j for j in range(ksize) if (j + lead) % sub == r] for r in range(sub)]
    ext_ref[pl.ds(halo + t, sub), :] = jnp.zeros((sub, ext_ref.shape[1]), F32)

    for b in range(bb):
        ext_ref[pl.ds(lead, ksize - 1), :] = buf_ref[b]
        ext_ref[pl.ds(halo, t), :] = u_ref[pl.ds(b * t, t), :]

        def tile(ti, carry):
            base = ti * rows if isinstance(ti, int) else pl.multiple_of(ti * rows, sub)
            acc = None
            for r in range(sub):
                part = None
                for j in taps_by_shift[r]:
                    q8 = (j + lead) - r
                    term = w[j:j + 1, :] * ext_ref[pl.ds(base + q8, rows + sub), :]
                    part = term if part is None else part + term
                if part is not None:
                    shifted = part[r:r + rows, :]
                    acc = shifted if acc is None else acc + shifted
            y_ref[pl.ds(b * t + base, rows), :] = acc
            return carry

        if n_tiles == 1:
            tile(0, 0)
        else:
            lax.fori_loop(0, n_tiles, tile, 0)


def _dwconv_group(u, buf, w, group, y_prev_out):
    r0, bsz, t = group
    m_rows, c = u.shape
    ksize = w.shape[0]
    sub = V7X_SUBLANES
    halo = _round_up(ksize - 1, sub)
    bc = 256
    bb = 1 if t >= 64 else 8
    rows = min(t, 64)
    blk = bb * t
    assert bsz % bb == 0 and t % rows == 0 and c % bc == 0 and r0 % blk == 0 and blk % sub == 0
    rb0 = r0 // blk
    aliased = y_prev_out is not None
    est = 4 * _nbytes((blk, bc), F32) + _nbytes((halo + t + sub, bc), F32) + (4 << 20)
    kern = functools.partial(_dwconv_kernel, ksize=ksize, t=t, bb=bb, rows=rows, halo=halo, aliased=aliased)
    in_specs = [pl.BlockSpec((blk, bc), lambda b, ci: (rb0 + b, ci)),
                pl.BlockSpec((bb, ksize - 1, bc), lambda b, ci: (b, 0, ci)),
                pl.BlockSpec((ksize, bc), lambda b, ci: (0, ci))]
    args = [u, buf.astype(F32), w]
    io_alias = {}
    if aliased:
        in_specs.append(pl.BlockSpec(memory_space=pl.ANY))
        args.append(y_prev_out)
        io_alias = {3: 0}
    return pl.pallas_call(
        kern,
        grid=(bsz // bb, c // bc),
        in_specs=in_specs,
        out_specs=pl.BlockSpec((blk, bc), lambda b, ci: (rb0 + b, ci)),
        out_shape=jax.ShapeDtypeStruct((m_rows, c), F32),
        scratch_shapes=[pltpu.VMEM((halo + t + sub, bc), F32)],
        input_output_aliases=io_alias,
        compiler_params=_params(("parallel", "parallel"), est),
        name="dwconv",
    )(*args)


def _conv_groups(u, w_dw, groups, bufs):
    y, new_bufs = None, []
    c = u.shape[1]
    km1 = w_dw.shape[0] - 1
    for (r0, b, t), buf in zip(groups, bufs):
        y = _dwconv_group(u, buf, w_dw, (r0, b, t), y)
        if t >= km1:
            tail = r0 + t - km1 + jnp.arange(b)[:, None] * t + jnp.arange(km1)[None, :]
            new_bufs.append(u[tail])
        else:
            new_bufs.append(jnp.concatenate([buf.astype(F32)[:, t:], u[r0:r0 + b * t].reshape(b, t, c)], axis=1))
    return y, new_bufs


def _conformer_layer(x, groups, states, p, wb):
    d = x.shape[1]
    b1 = _row(p['cf_b_pw1'])
    w1 = wb['cf_w_pw1']
    u = _mm([_full(x)], [(w1, (), 0), (w1, (), d)], n=d, out_dtype=F32, avecs=[_row(p['norm_g'])],
            prologue=lambda a, v, pw: _rms(a[0]) * v[0], evecs=[b1[:, :d], b1[:, d:]],
            epilogue=lambda accs, e, ev: (accs[0] + ev[0]) * jax.nn.sigmoid(accs[1] + ev[1]), name="cf_pw1")
    y, new_bufs = _conv_groups(u, p['cf_w_dw'], groups, [st[0] for st in states])

    def ln_silu(a, v, pw):
        z = a[0] + v[0]
        mu = jnp.mean(z, axis=-1, keepdims=True)
        var = jnp.mean(jnp.square(z - mu), axis=-1, keepdims=True)
        z = (z - mu) * lax.rsqrt(var + CF_LN_EPS) * v[1] + v[2]
        return z * jax.nn.sigmoid(z)

    x = _mm([_full(y)], [(wb['cf_w_pw2'], (), 0)], n=d, out_dtype=F32, prologue=ln_silu,
            avecs=[_row(p['cf_b_dw']), _row(p['cf_ln_g']), _row(p['cf_ln_b'])],
            e_blocks=[x], evecs=[_row(p['cf_b_pw2'])],
            epilogue=lambda accs, e, ev: e[0] + (accs[0] + ev[0]), name="cf_pw2")
    return x, [(nb,) for nb in new_bufs]


def _sconv_layer(x, groups, states, p, wb):
    d = x.shape[1]
    xn = _rmsnorm(x, p['norm_g'], BF16)
    w_in = wb['sc_w_in']
    b_gate = _mm([_full(xn)], [(w_in, (), 0)], n=d, out_dtype=F32, name="sc_b")
    ch = _mm([_full(xn)], [(w_in, (), d), (w_in, (), 2 * d)], n=d, out_dtype=F32,
             epilogue=lambda accs, e, ev: accs[0] * accs[1], name="sc_ch")
    y, new_bufs = _conv_groups(ch, p['sc_w_dw'], groups, [st[0] for st in states])
    x = _mm([_full(b_gate), _full(y)], [(wb['sc_w_out'], (), 0)], n=d, out_dtype=F32,
            prologue=lambda a, v, pw: a[0] * a[1], e_blocks=[x],
            epilogue=lambda accs, e, ev: e[0] + accs[0], name="sc_out")
    return x, [(nb,) for nb in new_bufs]


def kernel(x_prompt, x_sample, state_rwkv_wkv, state_rwkv_shift, state_mlstm_C, state_mlstm_n, state_mlstm_m,
           state_conf_conv, state_sconv_conv, norm_g, ffn_w_gu, ffn_w_down, final_norm_g,
           rw_mu, rw_w_rkv, rw_w0, rw_w1, rw_w2, rw_a0, rw_a1, rw_a2, rw_g1, rw_g2, rw_k_k, rw_k_a, rw_r_k,
           rw_ln_g, rw_ln_b, rw_w_o, ml_w_in, ml_b_gates, ml_norm_g, ml_w_o,
           cf_w_pw1, cf_b_pw1, cf_w_dw, cf_b_dw, cf_ln_g, cf_ln_b, cf_w_pw2, cf_b_pw2,
           sc_w_in, sc_w_dw, sc_w_out):
    bp, tp, d = x_prompt.shape
    bs, ts, _ = x_sample.shape
    depth = norm_g.shape[0]
    n_mixers = 4
    groups = [(0, bp, tp), (bp * tp, bs, ts)]
    x = jnp.concatenate([x_prompt.reshape(bp * tp, d), x_sample.reshape(bs * ts, d)], axis=0).astype(F32)

    dv = d // ML_HEADS
    dk = dv // 2
    cf_k = cf_w_dw.shape[1]
    sc_k = sc_w_dw.shape[1]
    zeros = lambda *s: jnp.zeros(s, F32)
    w_gu_b = ffn_w_gu.astype(BF16)
    w_down_b = ffn_w_down.astype(BF16)

    out_states = {name: ([], []) for name in ('wkv', 'shift', 'C', 'n', 'm', 'cf', 'sc')}
    for i in range(depth):
        j = i // n_mixers
        kind = i % n_mixers
        x = _ffn(x, norm_g[i, 0], w_gu_b, w_down_b, (i, 0))
        if kind == 0:
            p = dict(norm_g=norm_g[i, 1], rw_mu=rw_mu[j], rw_w0=rw_w0[j], rw_a0=rw_a0[j], rw_k_k=rw_k_k[j],
                     rw_k_a=rw_k_a[j], rw_r_k=rw_r_k[j], rw_ln_g=rw_ln_g[j], rw_ln_b=rw_ln_b[j])
            wb = dict(rw_w_rkv=rw_w_rkv[j].astype(BF16), rw_w1=rw_w1[j].astype(BF16), rw_w2=rw_w2[j].astype(BF16),
                      rw_a1=rw_a1[j].astype(BF16), rw_a2=rw_a2[j].astype(BF16), rw_g1=rw_g1[j].astype(BF16),
                      rw_g2=rw_g2[j].astype(BF16), rw_w_o=rw_w_o[j].astype(BF16))
            states = [(None, zeros(bp, d)), (state_rwkv_wkv[j], state_rwkv_shift[j].astype(F32))]
            x, new = _rwkv_layer(x, groups, states, p, wb)
            for gi in range(2):
                out_states['wkv'][gi].append(new[gi][0])
                out_states['shift'][gi].append(new[gi][1])
        elif kind == 1:
            p = dict(norm_g=norm_g[i, 1], ml_b_gates=ml_b_gates[j], ml_norm_g=ml_norm_g[j])
            n_main = 2 * ML_HEADS * dk + 2 * ML_HEADS * dv
            wb = dict(ml_w_main=ml_w_in[j][:, :n_main].astype(BF16), ml_w_gates=ml_w_in[j][:, n_main:].astype(BF16),
                      ml_w_o=ml_w_o[j].astype(BF16))
            states = [(zeros(bp, ML_HEADS, dv, dk), zeros(bp, ML_HEADS, dk), zeros(bp, ML_HEADS)),
                      (state_mlstm_C[j], state_mlstm_n[j], state_mlstm_m[j])]
            x, new = _mlstm_layer(x, groups, states, p, wb)
            for gi in range(2):
                out_states['C'][gi].append(new[gi][0])
                out_states['n'][gi].append(new[gi][1])
                out_states['m'][gi].append(new[gi][2])
        elif kind == 2:
            p = dict(norm_g=norm_g[i, 1], cf_b_pw1=cf_b_pw1[j], cf_w_dw=cf_w_dw[j], cf_b_dw=cf_b_dw[j],
                     cf_ln_g=cf_ln_g[j], cf_ln_b=cf_ln_b[j], cf_b_pw2=cf_b_pw2[j])
            wb = dict(cf_w_pw1=cf_w_pw1[j].astype(BF16), cf_w_pw2=cf_w_pw2[j].astype(BF16))
            states = [(zeros(bp, cf_k - 1, d),), (state_conf_conv[j],)]
            x, new = _conformer_layer(x, groups, states, p, wb)
            for gi in range(2):
                out_states['cf'][gi].append(new[gi][0])
        else:
            p = dict(norm_g=norm_g[i, 1], sc_w_dw=sc_w_dw[j])
            wb = dict(sc_w_in=sc_w_in[j].astype(BF16), sc_w_out=sc_w_out[j].astype(BF16))
            states = [(zeros(bp, sc_k - 1, d),), (state_sconv_conv[j],)]
            x, new = _sconv_layer(x, groups, states, p, wb)
            for gi in range(2):
                out_states['sc'][gi].append(new[gi][0])
        x = _ffn(x, norm_g[i, 2], w_gu_b, w_down_b, (i, 1))

    y_prompt = _rmsnorm(x, final_norm_g, x_prompt.dtype, 0, bp * tp).reshape(bp, tp, d)
    y_sample = _rmsnorm(x, final_norm_g, x_sample.dtype, bp * tp, bs * ts).reshape(bs, ts, d)
    order = ('wkv', 'shift', 'C', 'n', 'm', 'cf', 'sc')
    dt = x_prompt.dtype
    prompt_out = tuple(jnp.stack(out_states[name][0]).astype(dt) for name in order)
    sample_out = tuple(jnp.stack(out_states[name][1]).astype(st.dtype) for name, st in
                       zip(order, (state_rwkv_wkv, state_rwkv_shift, state_mlstm_C, state_mlstm_n, state_mlstm_m,
                                   state_conf_conv, state_sconv_conv)))
    return (y_prompt, y_sample) + prompt_out + sample_out
```

```python
import functools
import math

import jax
import jax.numpy as jnp
from jax import lax
from jax.experimental import pallas as pl
from jax.experimental.pallas import tpu as pltpu

F32 = jnp.float32
BF16 = jnp.bfloat16

NORM_EPS = 1e-6
RW_HEAD = 64
RW_GN_EPS = 64e-5
ML_HEADS = 8
ML_GATE_CAP = 15.0
ML_NORM_EPS = 1e-6
CF_LN_EPS = 1e-5

V7X_LANES = 128
V7X_SUBLANES = 8
V7X_VMEM_BYTES = 64 * 1024 * 1024
VMEM_LIMIT_CAP = V7X_VMEM_BYTES - 6 * 1024 * 1024


def _nbytes(shape, dtype):
    return math.prod(shape) * jnp.dtype(dtype).itemsize


def _params(sem, vmem_est):
    limit = int(min(max(vmem_est * 1.2 + (2 << 20), 16 << 20), VMEM_LIMIT_CAP))
    return pltpu.CompilerParams(dimension_semantics=sem, vmem_limit_bytes=limit)


def _pick_bm(m, target, also_divides=()):
    best = None
    for bm in range(16, min(m, target) + 1, 16):
        if m % bm == 0 and all(o % bm == 0 for o in also_divides):
            best = bm
    assert best is not None
    return best


def _rms(x):
    return x * lax.rsqrt(jnp.mean(x * x, axis=-1, keepdims=True) + NORM_EPS)


def _round_up(x, m):
    return -(-x // m) * m


def _mm_kernel(*refs, counts, prologue, epilogue, use_scratch):
    n_a, n_av, n_pw, n_w, n_e, n_ev = counts
    pos = 0
    groups = []
    for c in counts:
        groups.append(refs[pos:pos + c])
        pos += c
    a_refs, av_refs, pw_refs, w_refs, e_refs, ev_refs = groups
    o_ref = refs[pos]
    if use_scratch:
        lhs_ref = refs[-1]

        @pl.when(pl.program_id(1) == 0)
        def _():
            lhs_ref[...] = prologue([r[...] for r in a_refs], [r[...] for r in av_refs],
                                    [r[...] for r in pw_refs]).astype(BF16)

        lhs = lhs_ref[...]
    else:
        lhs = a_refs[0][...]
    accs = [jnp.dot(lhs, w[...].astype(BF16), preferred_element_type=F32) for w in w_refs]
    out = epilogue(accs, [r[...] for r in e_refs], [r[...] for r in ev_refs])
    o_ref[...] = out.astype(o_ref.dtype)


def _mm(a_list, w_list, *, n, out_dtype, prologue=None, epilogue=None, avecs=(), pw_list=(), e_blocks=(), evecs=(),
        bm_target=1088, bn=512, name="mm"):
    m = a_list[0][0].shape[0]
    k = w_list[0][0].shape[-2]
    bm = _pick_bm(m, bm_target)
    bn = min(bn, n)
    assert n % bn == 0
    use_scratch = prologue is not None or a_list[0][0].dtype != BF16
    if prologue is None:
        prologue = lambda a, v, pw: a[0]
    if epilogue is None:
        epilogue = lambda accs, e, ev: accs[0]

    in_specs, args, est = [], [], 0
    for arr, ka, cb in a_list:
        in_specs.append(pl.BlockSpec((bm, ka), functools.partial(lambda i, j, c: (i, c), c=cb)))
        args.append(arr)
        est += 2 * _nbytes((bm, ka), arr.dtype)
    for v in avecs:
        in_specs.append(pl.BlockSpec(v.shape, lambda i, j: (0, 0)))
        args.append(v)
    for pw in pw_list:
        in_specs.append(pl.BlockSpec(pw.shape, lambda i, j: (0, 0)))
        args.append(pw)
        est += 2 * _nbytes(pw.shape, pw.dtype)
    for arr, lead, off in w_list:
        assert off % bn == 0 and arr.shape[-2] == k
        in_specs.append(pl.BlockSpec((None,) * len(lead) + (k, bn),
                                     functools.partial(lambda i, j, l, o: l + (0, j + o), l=tuple(lead), o=off // bn)))
        args.append(arr)
        est += 2 * _nbytes((k, bn), arr.dtype) + (_nbytes((k, bn), BF16) if arr.dtype != BF16 else 0)
    for e in e_blocks:
        in_specs.append(pl.BlockSpec((bm, bn), lambda i, j: (i, j)))
        args.append(e)
        est += 2 * _nbytes((bm, bn), e.dtype)
    for v in evecs:
        in_specs.append(pl.BlockSpec((1, bn), lambda i, j: (0, j)))
        args.append(v)
    est += 2 * _nbytes((bm, bn), out_dtype) + (len(w_list) + 2) * _nbytes((bm, bn), F32)
    scratch = []
    if use_scratch:
        scratch.append(pltpu.VMEM((bm, k), BF16))
        est += _nbytes((bm, k), BF16) + 2 * _nbytes((bm, a_list[0][1]), F32)

    counts = (len(a_list), len(avecs), len(pw_list), len(w_list), len(e_blocks), len(evecs))
    kern = functools.partial(_mm_kernel, counts=counts, prologue=prologue, epilogue=epilogue, use_scratch=use_scratch)
    return pl.pallas_call(
        kern,
        grid=(m // bm, n // bn),
        in_specs=in_specs,
        out_specs=pl.BlockSpec((bm, bn), lambda i, j: (i, j)),
        out_shape=jax.ShapeDtypeStruct((m, n), out_dtype),
        scratch_shapes=scratch,
        compiler_params=_params(("parallel", "arbitrary"), est),
        name=name,
    )(*args)


def _full(arr):
    return (arr, arr.shape[1], 0)


def _row(v):
    return v.reshape(1, -1)


def _rmsnorm_kernel(x_ref, g_ref, o_ref):
    o_ref[...] = (_rms(x_ref[...]) * g_ref[...]).astype(o_ref.dtype)


def _rmsnorm(x, g, out_dtype, row0=0, nrows=None):
    d = x.shape[1]
    m = x.shape[0] if nrows is None else nrows
    bm = _pick_bm(m, 1088, also_divides=(row0,) if row0 else ())
    rb0 = row0 // bm
    est = 2 * _nbytes((bm, d), F32) * 3
    return pl.pallas_call(
        _rmsnorm_kernel,
        grid=(m // bm,),
        in_specs=[pl.BlockSpec((bm, d), lambda i: (rb0 + i, 0)), pl.BlockSpec((1, d), lambda i: (0, 0))],
        out_specs=pl.BlockSpec((bm, d), lambda i: (i, 0)),
        out_shape=jax.ShapeDtypeStruct((m, d), out_dtype),
        compiler_params=_params(("parallel",), est),
        name="rmsnorm",
    )(x, _row(g))


def _ffn(x, g, w_gu, w_down, lead):
    d_ff = w_down.shape[-2]
    h = _mm([_full(x)], [(w_gu, lead, 0), (w_gu, lead, d_ff)], n=d_ff, out_dtype=BF16, avecs=[_row(g)],
            prologue=lambda a, v, pw: _rms(a[0]) * v[0],
            epilogue=lambda accs, e, ev: accs[0] * jax.nn.sigmoid(accs[0]) * accs[1], name="ffn_up")
    return _mm([_full(h)], [(w_down, lead, 0)], n=x.shape[1], out_dtype=F32, e_blocks=[x],
               epilogue=lambda accs, e, ev: e[0] + 0.5 * accs[0], name="ffn_down")


def _rw_mix_kernel(x_ref, g_ref, mu_ref, ov_ref, h_ref, *rest, n_prompt_blocks, blocks_per_seq, ts):
    outs, carry_ref = rest[:-1], rest[-1]
    pid = pl.program_id(0)

    @pl.when(pid == 0)
    def _():
        carry_ref[...] = jnp.zeros_like(carry_ref)

    h = _rms(x_ref[...]) * g_ref[...]
    h_ref[...] = h
    bm = h.shape[0]
    rolled = pltpu.roll(h, 1, axis=0)
    row = lax.broadcasted_iota(jnp.int32, h.shape, 0)
    carry = jnp.where(pid % blocks_per_seq == 0, 0.0, carry_ref[...])
    prev_prompt = jnp.where(row == 0, carry, rolled)
    prev_sample = jnp.where(row % ts == 0, ov_ref[...], rolled)
    prev = jnp.where(pid >= n_prompt_blocks, prev_sample, prev_prompt)
    carry_ref[...] = h[bm - 1:bm, :]
    delta = prev - h
    for i, o in enumerate(outs):
        o[...] = (h + delta * mu_ref[i:i + 1, :]).astype(BF16)


def _rw_mix(x, g, mu, shift_sample, groups):
    m, d = x.shape
    (_, bp, tp), (r0s, bs, ts) = groups
    bm = _pick_bm(tp, 512, also_divides=(bs * ts,))
    assert bm % ts == 0 and r0s == bp * tp
    n_prompt_blocks = (bp * tp) // bm
    ov = jnp.zeros((bs, ts, d), F32).at[:, 0].set(shift_sample.astype(F32)).reshape(bs * ts, d)
    n_mix = mu.shape[0]
    blk = pl.BlockSpec((bm, d), lambda i: (i, 0))
    est = 2 * _nbytes((bm, d), F32) * 3 + 2 * n_mix * _nbytes((bm, d), BF16) + 6 * _nbytes((bm, d), F32)
    kern = functools.partial(_rw_mix_kernel, n_prompt_blocks=n_prompt_blocks, blocks_per_seq=tp // bm, ts=ts)
    outs = pl.pallas_call(
        kern,
        grid=(m // bm,),
        in_specs=[blk, pl.BlockSpec((1, d), lambda i: (0, 0)), pl.BlockSpec((n_mix, d), lambda i: (0, 0)),
                  pl.BlockSpec((bm, d), lambda i: (jnp.maximum(i - n_prompt_blocks, 0), 0))],
        out_specs=[blk] * (1 + n_mix),
        out_shape=[jax.ShapeDtypeStruct((m, d), F32)] + [jax.ShapeDtypeStruct((m, d), BF16)] * n_mix,
        scratch_shapes=[pltpu.VMEM((1, d), F32)],
        compiler_params=_params(("arbitrary",), est),
        name="rw_mix",
    )(x, _row(g), mu, ov)
    return outs[0], outs[1:]


def _rwkv_steps(r_ref, k_ref, v_ref, w_ref, a_ref, y_ref, param_refs, s_ref, tb, nh):
    kk_p, ka_p, rk_p, ln_g, ln_b = (q[...] for q in param_refs)
    vgrp = V7X_SUBLANES

    def step(t, carry):
        r = r_ref[t]
        k = k_ref[t]
        v = v_ref[t]
        logw = -jax.nn.softplus(-w_ref[t]) - 0.5
        d = jnp.exp(-jnp.exp(logw))
        a = jax.nn.sigmoid(a_ref[t])
        kk = k * kk_p
        kk = kk / jnp.maximum(jnp.sqrt(jnp.sum(kk * kk, axis=0, keepdims=True)), 1e-12)
        kmod = k * (1.0 + (a - 1.0) * ka_p)
        b = kk * a

        def vgroup(g, c2):
            base = pl.multiple_of(g * vgrp, vgrp)
            v8 = v_ref[t, pl.ds(base, vgrp), :]
            ys = []
            for j in range(vgrp):
                s_old = s_ref[base + j]
                sa = jnp.sum(s_old * kk, axis=0, keepdims=True)
                s_new = s_old * d - sa * b + v8[j:j + 1, :] * kmod
                s_ref[base + j] = s_new
                ys.append(jnp.sum(s_new * r, axis=0, keepdims=True))
            y_ref[t, pl.ds(base, vgrp), :] = jnp.concatenate(ys, axis=0)
            return c2

        lax.fori_loop(0, nh // vgrp, vgroup, 0)

        y = y_ref[t]
        mean = jnp.mean(y, axis=0, keepdims=True)
        var = jnp.mean(jnp.square(y - mean), axis=0, keepdims=True)
        yn = (y - mean) * lax.rsqrt(var + RW_GN_EPS) * ln_g + ln_b
        bonus = jnp.sum(r * kmod * rk_p, axis=0, keepdims=True) * v
        y_ref[t] = yn + bonus
        return carry

    lax.fori_loop(0, tb, step, 0)


def _rwkv_scan_kernel(r_ref, k_ref, v_ref, w_ref, a_ref, kk_p_ref, ka_p_ref, rk_p_ref, lg_ref, lb_ref, s0_ref,
                      y_ref, sT_ref, s_ref, *, tb, nh):
    tblk = pl.program_id(1)

    @pl.when(tblk == 0)
    def _():
        s_ref[...] = s0_ref[...]

    _rwkv_steps(r_ref, k_ref, v_ref, w_ref, a_ref, y_ref, (kk_p_ref, ka_p_ref, rk_p_ref, lg_ref, lb_ref), s_ref, tb, nh)

    @pl.when(tblk == pl.num_programs(1) - 1)
    def _():
        sT_ref[...] = s_ref[...]


def _rwkv_scan(r, k, v, w_pre, a_pre, kk_p, ka_p, rk_p, ln_g, ln_b, s0, *, tb):
    t, nh, c = r.shape
    lanes = V7X_LANES
    assert c % lanes == 0 and t % tb == 0
    seq = pl.BlockSpec((tb, nh, lanes), lambda ci, ti: (ti, 0, ci))
    par = pl.BlockSpec((nh, lanes), lambda ci, ti: (0, ci))
    st = pl.BlockSpec((nh, nh, lanes), lambda ci, ti: (0, 0, ci))
    est = 6 * 2 * _nbytes((tb, nh, lanes), F32) + 5 * _nbytes((nh, nh, lanes), F32)
    kern = functools.partial(_rwkv_scan_kernel, tb=tb, nh=nh)
    return pl.pallas_call(
        kern,
        grid=(c // lanes, t // tb),
        in_specs=[seq] * 5 + [par] * 5 + [st],
        out_specs=[seq, st],
        out_shape=[jax.ShapeDtypeStruct((t, nh, c), F32), jax.ShapeDtypeStruct((nh, nh, c), F32)],
        scratch_shapes=[pltpu.VMEM((nh, nh, lanes), F32)],
        compiler_params=_params(("parallel", "arbitrary"), est),
        name="rwkv_scan",
    )(r, k, v, w_pre, a_pre, kk_p, ka_p, rk_p, ln_g, ln_b, s0)


def _to_chains(x, bsz, t, nheads):
    return x.reshape(bsz, t, nheads, RW_HEAD).transpose(1, 3, 0, 2).reshape(t, RW_HEAD, bsz * nheads)


def _from_chains(y, bsz, t, nheads):
    return y.reshape(t, RW_HEAD, bsz, nheads).transpose(2, 0, 3, 1).reshape(bsz * t, nheads * RW_HEAD)


def _rwkv_group(r, k, v, w_pre, a_pre, wkv0, p, bsz, t):
    d_model = r.shape[1]
    nheads = d_model // RW_HEAD
    c = bsz * nheads
    chain_param = lambda q: jnp.tile(q.reshape(nheads, RW_HEAD).T, (1, bsz))
    if wkv0 is None:
        s0 = jnp.zeros((RW_HEAD, RW_HEAD, c), F32)
    else:
        s0 = wkv0.astype(F32).transpose(2, 3, 0, 1).reshape(RW_HEAD, RW_HEAD, c)
    tb = t if t <= 32 else 32
    y, s_fin = _rwkv_scan(*[_to_chains(q, bsz, t, nheads) for q in (r, k, v, w_pre, a_pre)],
                          *[chain_param(p[name]) for name in ('rw_k_k', 'rw_k_a', 'rw_r_k', 'rw_ln_g', 'rw_ln_b')],
                          s0, tb=tb)
    s_fin = s_fin.reshape(RW_HEAD, RW_HEAD, bsz, nheads).transpose(2, 3, 0, 1)
    return _from_chains(y, bsz, t, nheads), s_fin


def _rwkv_chunk_kernel(r_ref, k_ref, v_ref, w_ref, a_ref, g_ref, kkp_ref, kap_ref, rkp_ref, lg_ref, lb_ref,
                       y_ref, sT_ref, s2_ref, *, chunk, npairs, unroll):
    L = chunk
    lanes = V7X_LANES
    nh = RW_HEAD
    ci = pl.program_id(1)

    @pl.when(ci == 0)
    def _():
        s2_ref[...] = jnp.zeros_like(s2_ref)

    lane_l = lax.broadcasted_iota(jnp.int32, (L, lanes), 1)
    row_l = lax.broadcasted_iota(jnp.int32, (L, lanes), 0)
    head0 = lane_l < nh
    n2 = 2 * L
    ti = lax.broadcasted_iota(jnp.int32, (n2, n2), 0)
    si = lax.broadcasted_iota(jnp.int32, (n2, n2), 1)
    strict = si < ti
    incl = si <= ti
    sq_r = lax.broadcasted_iota(jnp.int32, (lanes, lanes), 0) // nh
    sq_c = lax.broadcasted_iota(jnp.int32, (lanes, lanes), 1) // nh
    seg_ones = jnp.where(sq_r == sq_c, 1.0, 0.0).astype(BF16)
    nt = (((1,), (1,)), ((), ()))
    tn = (((0,), (0,)), ((), ()))

    def seg_sums(xs):
        his = [x.astype(BF16) for x in xs]
        los = [(x - hi.astype(F32)).astype(BF16) for x, hi in zip(xs, his)]
        return [jnp.dot(hi, seg_ones, preferred_element_type=F32) + jnp.dot(lo, seg_ones, preferred_element_type=F32)
                for hi, lo in zip(his, los)]

    def stack(x):
        return jnp.concatenate([jnp.where(head0, x, 0.0), jnp.where(head0, 0.0, x)], axis=0)

    dot = lambda x, y: jnp.dot(x, y, preferred_element_type=F32)
    dot_nt = lambda x, y: lax.dot_general(x, y, nt, preferred_element_type=F32)
    dot_tn = lambda x, y: lax.dot_general(x, y, tn, preferred_element_type=F32)

    def wave(ps):
        idx = range(len(ps))
        ls = [pl.ds(p * lanes, lanes) for p in ps]
        r = [r_ref[:, s] for s in ls]
        v = [v_ref[:, s] for s in ls]
        logd, cl, kk0, kmod, a = [], [], [], [], []
        for i in idx:
            k_i = k_ref[:, ls[i]]
            ld = -jnp.exp(-jax.nn.softplus(-w_ref[:, ls[i]]) - 0.5)
            c = ld
            sh = 1
            while sh < L:
                c = c + jnp.where(row_l >= sh, pltpu.roll(c, sh, axis=0), 0.0)
                sh *= 2
            a_i = jax.nn.sigmoid(a_ref[:, ls[i]])
            logd.append(ld)
            cl.append(c)
            a.append(a_i)
            kk0.append(k_i * kkp_ref[:, ls[i]])
            kmod.append(k_i * (1.0 + (a_i - 1.0) * kap_ref[:, ls[i]]))
        sums = seg_sums([kk0[i] * kk0[i] for i in idx] + [r[i] * kmod[i] * rkp_ref[:, ls[i]] for i in idx])
        nrm2 = sums[:len(ps)]
        bonus = [sums[len(ps) + i] * v[i] for i in idx]
        x1, x2, vs, d_last = [], [], [], []
        for i in idx:
            kk = kk0[i] / jnp.maximum(jnp.sqrt(nrm2[i]), 1e-12)
            d_inc = jnp.exp(cl[i])
            d_inv = jnp.exp(-cl[i])
            xa = stack(kk * jnp.exp(cl[i] - logd[i]))
            xr = stack(r[i] * d_inc)
            xb = stack(kk * a[i] * d_inv)
            xk = stack(kmod[i] * d_inv)
            x1.append(jnp.concatenate([xa, xr], axis=0).astype(BF16))
            x2.append(jnp.concatenate([xb, xk], axis=0).astype(BF16))
            vs.append(stack(v[i]))
            d_last.append(d_inc[L - 1:L, :])
        s2 = [s2_ref[p] for p in ps]
        g = [dot_nt(x1[i], x2[i]) for i in idx]
        pz = [dot_nt(x1[i], s2[i].astype(BF16)) for i in idx]
        m_k = [jnp.where(strict, g[i][:n2, n2:], 0.0).astype(BF16) for i in idx]
        m_pow = [jnp.where(strict, g[i][:n2, :n2], 0.0).astype(BF16) for i in idx]
        acat = [jnp.concatenate([jnp.where(incl, g[i][n2:, :n2], 0.0), jnp.where(incl, g[i][n2:, n2:], 0.0)],
                                axis=1).astype(BF16) for i in idx]
        vsb = [vs[i].astype(BF16) for i in idx]
        rhs = [pz[i][:n2] + dot(m_k[i], vsb[i]) for i in idx]
        u = [rhs[i] - dot(m_pow[i], rhs[i].astype(BF16)) for i in idx]
        pw = 2
        while pw < L:
            m_pow = [dot(m_pow[i], m_pow[i]).astype(BF16) for i in idx]
            u = [u[i] + dot(m_pow[i], u[i].astype(BF16)) for i in idx]
            pw *= 2
        wcat = [jnp.concatenate([-u[i], vs[i]], axis=0).astype(BF16) for i in idx]
        ys = [pz[i][n2:] + dot(acat[i], wcat[i]) for i in idx]
        upd = [dot_tn(wcat[i], x2[i]) for i in idx]
        for i in idx:
            s2_ref[ps[i]] = (s2[i] + upd[i]) * d_last[i]
        y = [ys[i][:L] + ys[i][L:] for i in idx]
        mean = [z * (1.0 / nh) for z in seg_sums(y)]
        yc = [y[i] - mean[i] for i in idx]
        var = [z * (1.0 / nh) for z in seg_sums([yc[i] * yc[i] for i in idx])]
        for i in idx:
            yn = yc[i] * lax.rsqrt(var[i] + RW_GN_EPS) * lg_ref[:, ls[i]] + lb_ref[:, ls[i]]
            y_ref[:, ls[i]] = ((yn + bonus[i]) * g_ref[:, ls[i]]).astype(y_ref.dtype)

    for p0 in range(0, npairs, unroll):
        wave(list(range(p0, p0 + unroll)))

    @pl.when(ci == pl.num_programs(1) - 1)
    def _():
        for p in range(npairs):
            blk = s2_ref[p]
            sT_ref[2 * p] = blk[:nh, :nh]
            sT_ref[2 * p + 1] = blk[nh:, nh:]


def _rwkv_chunked(arrs, params, row0, bsz, t):
    m, d = arrs[0].shape
    lanes = V7X_LANES
    chunk = min(t, 64)
    npairs = d // lanes
    assert t % chunk == 0 and row0 % chunk == 0 and d % lanes == 0 and lanes == 2 * RW_HEAD
    nc = t // chunk
    rb0 = row0 // chunk
    seq = pl.BlockSpec((chunk, d), lambda b, c: (rb0 + b * nc + c, 0))
    vec = pl.BlockSpec((1, d), lambda b, c: (0, 0))
    est = 7 * 2 * _nbytes((chunk, d), F32) + 3 * _nbytes((npairs, lanes, lanes), F32) + (8 << 20)
    kern = functools.partial(_rwkv_chunk_kernel, chunk=chunk, npairs=npairs, unroll=16)
    return pl.pallas_call(
        kern,
        grid=(bsz, nc),
        in_specs=[seq] * 6 + [vec] * 5,
        out_specs=[seq, pl.BlockSpec((None, 2 * npairs, RW_HEAD, RW_HEAD), lambda b, c: (b, 0, 0, 0))],
        out_shape=[jax.ShapeDtypeStruct((m, d), BF16),
                   jax.ShapeDtypeStruct((bsz, 2 * npairs, RW_HEAD, RW_HEAD), F32)],
        scratch_shapes=[pltpu.VMEM((npairs, lanes, lanes), F32)],
        compiler_params=_params(("parallel", "arbitrary"), est),
        name="rwkv_chunk",
    )(*arrs, *[_row(q) for q in params])


def _rwkv_layer(x, groups, states, p, wb):
    d = x.shape[1]
    h, (xr, xw, xk, xv, xa, xg) = _rw_mix(x, p['norm_g'], p['rw_mu'], states[1][1], groups)
    w_rkv = wb['rw_w_rkv']
    r = _mm([_full(xr)], [(w_rkv, (0,), 0)], n=d, out_dtype=F32, name="rw_r")
    k = _mm([_full(xk)], [(w_rkv, (1,), 0)], n=d, out_dtype=F32, name="rw_k")
    v = _mm([_full(xv)], [(w_rkv, (2,), 0)], n=d, out_dtype=F32, name="rw_v")
    add_vec = lambda accs, e, ev: ev[0] + accs[0]
    lora = lambda act: (lambda a, v_, pw: act(jnp.dot(a[0], pw[0], preferred_element_type=F32)))
    w_pre = _mm([_full(xw)], [(wb['rw_w2'], (), 0)], n=d, out_dtype=F32, pw_list=[wb['rw_w1']],
                prologue=lora(jnp.tanh), evecs=[_row(p['rw_w0'])], epilogue=add_vec, name="rw_w")
    a_pre = _mm([_full(xa)], [(wb['rw_a2'], (), 0)], n=d, out_dtype=F32, pw_list=[wb['rw_a1']],
                prologue=lora(lambda z: z), evecs=[_row(p['rw_a0'])], epilogue=add_vec, name="rw_a")
    g = _mm([_full(xg)], [(wb['rw_g2'], (), 0)], n=d, out_dtype=F32, pw_list=[wb['rw_g1']],
            prologue=lora(jax.nn.sigmoid), name="rw_g")

    y, new_states = None, []
    for (r0, b, t), st in zip(groups, states):
        sl = slice(r0, r0 + b * t)
        if st[0] is None:
            assert y is None
            y, s_fin = _rwkv_chunked((r, k, v, w_pre, a_pre, g),
                                     [p[name] for name in ('rw_k_k', 'rw_k_a', 'rw_r_k', 'rw_ln_g', 'rw_ln_b')], r0, b, t)
        else:
            y_g, s_fin = _rwkv_group(r[sl], k[sl], v[sl], w_pre[sl], a_pre[sl], st[0], p, b, t)
            y = lax.dynamic_update_slice(y, (y_g * g[sl]).astype(y.dtype), (r0, 0))
        new_states.append((s_fin, lax.slice(h, (r0 + t - 1, 0), (r0 + b * t, d), (t, 1))))
    x = _mm([_full(y)], [(wb['rw_w_o'], (), 0)], n=d, out_dtype=F32, e_blocks=[x],
            epilogue=lambda accs, e, ev: e[0] + accs[0], name="rw_out")
    return x, new_states


def _mlstm_kernel(q_ref, k_ref, v_ref, gt_ref, gtt_ref, bg_ref, bgt_ref, ng_ref, c0_ref, n0_ref, m0_ref, *rest,
                  nheads, dk, dv, chunk, bb, aliased):
    if aliased:
        rest = rest[1:]
    h_ref, cT_ref, nT_ref, mT_ref, c_ref, n_ref, m_ref = rest
    ci = pl.program_id(1)

    @pl.when(ci == 0)
    def _():
        c_ref[...] = c0_ref[...]
        n_ref[...] = n0_ref[...]
        m_ref[...] = m0_ref[...]

    L = chunk
    cap = lambda z: ML_GATE_CAP * jnp.tanh(z / ML_GATE_CAP)
    gates_c_all = cap(gt_ref[...] + bg_ref[...])
    gates_r_all = cap(gtt_ref[...] + bgt_ref[...])
    row_id = lax.broadcasted_iota(jnp.int32, (L, L), 0)
    col_id = lax.broadcasted_iota(jnp.int32, (L, L), 1)
    causal = col_id <= row_id
    k_scale = dk ** -0.5

    nt = (((1,), (1,)), ((), ()))
    tn = (((0,), (0,)), ((), ()))
    chains = [(bi, hd) for bi in range(bb) for hd in range(nheads)]
    idx = range(len(chains))
    rows = [slice(bi * L, (bi + 1) * L) for bi, _ in chains]
    li_c, b_c, m_prev, dmat, inter, m_t = [], [], [], [], [], []
    for (bi, hd), rw in zip(chains, rows):
        gates_c = gates_c_all[rw, :]
        gates_r = gates_r_all[:, rw]
        li_r = gates_r[hd:hd + 1, :]
        lf_c = jax.nn.log_sigmoid(gates_c[:, nheads + hd:nheads + hd + 1])
        lf_r = jax.nn.log_sigmoid(gates_r[nheads + hd:nheads + hd + 1, :])
        bc = jnp.sum(jnp.where(causal, lf_r, 0.0), axis=1, keepdims=True)
        br = jnp.sum(jnp.where(row_id <= col_id, lf_c, 0.0), axis=0, keepdims=True)
        mp = m_ref[bi, hd][:, :1]
        dm = jnp.where(causal, bc - br + li_r, -jnp.inf)
        it = bc + mp
        li_c.append(gates_c[:, hd:hd + 1])
        b_c.append(bc)
        m_prev.append(mp)
        dmat.append(dm)
        inter.append(it)
        m_t.append(jnp.maximum(jnp.max(dm, axis=1, keepdims=True), it))

    q = [q_ref[rw, hd * dk:(hd + 1) * dk] for (_, hd), rw in zip(chains, rows)]
    k = [k_ref[rw, hd * dk:(hd + 1) * dk] * k_scale for (_, hd), rw in zip(chains, rows)]
    v = [v_ref[rw, hd * dv:(hd + 1) * dv] for (_, hd), rw in zip(chains, rows)]
    qb = [z.astype(BF16) for z in q]
    kb = [z.astype(BF16) for z in k]
    c_old = [c_ref[bi, hd] for bi, hd in chains]
    n_old = [n_ref[bi, hd] for bi, hd in chains]
    qk = [lax.dot_general(qb[i], kb[i], nt, preferred_element_type=F32) for i in idx]
    cq = [lax.dot_general(qb[i], c_old[i].astype(BF16), nt, preferred_element_type=F32) for i in idx]
    s = [qk[i] * jnp.exp(dmat[i] - m_t[i]) for i in idx]
    sv = [jnp.dot(s[i].astype(BF16), v[i].astype(BF16), preferred_element_type=F32) for i in idx]
    m_new, ws = [], []
    for i in idx:
        b_last = b_c[i][L - 1:L, :]
        ws_log = b_last - b_c[i] + li_c[i]
        mn = jnp.maximum(b_last + m_prev[i], jnp.max(ws_log, axis=0, keepdims=True))
        m_new.append(mn)
        ws.append(jnp.exp(ws_log - mn))
    upd = [lax.dot_general((v[i] * ws[i]).astype(BF16), kb[i], tn, preferred_element_type=F32) for i in idx]
    for i, ((bi, hd), rw) in enumerate(zip(chains, rows)):
        w_inter = jnp.exp(inter[i] - m_t[i])
        num = sv[i] + w_inter * cq[i]
        nq = jnp.sum(q[i] * n_old[i], axis=1, keepdims=True)
        den = jnp.sum(s[i], axis=1, keepdims=True) + w_inter * nq
        h = num / jnp.maximum(jnp.abs(den), jnp.exp(-m_t[i]))
        hn = h * lax.rsqrt(jnp.mean(h * h, axis=1, keepdims=True) + ML_NORM_EPS)
        h_ref[rw, hd * dv:(hd + 1) * dv] = hn * ng_ref[:, hd * dv:(hd + 1) * dv]
        dec = jnp.exp(b_c[i][L - 1:L, :] + m_prev[i] - m_new[i])
        c_ref[bi, hd] = dec * c_old[i] + upd[i]
        n_ref[bi, hd] = dec * n_old[i] + jnp.sum(ws[i] * k[i], axis=0, keepdims=True)
        m_ref[bi, hd] = jnp.broadcast_to(m_new[i], m_ref.shape[2:])

    @pl.when(ci == pl.num_programs(1) - 1)
    def _():
        cT_ref[...] = c_ref[...]
        nT_ref[...] = n_ref[...]
        mT_ref[...] = m_ref[...]


def _mlstm_group(qkvo, gates, b_gates, norm_g, c0, n0, m0, group, h_prev_out):
    r0, bsz, t = group
    m_rows = qkvo.shape[0]
    nheads = ML_HEADS
    dk = c0.shape[-1]
    dv = c0.shape[-2]
    lanes = V7X_LANES
    chunk = min(t, lanes)
    bb = 1 if chunk >= V7X_SUBLANES else 2 * V7X_SUBLANES // chunk
    rows = bb * chunk
    assert t % chunk == 0 and bsz % bb == 0 and r0 % rows == 0
    nc = t // chunk
    nsteps = (bsz // bb) * nc
    rb0 = r0 // rows
    g_grp = gates[r0:r0 + bsz * t]
    g3t = g_grp.reshape(nsteps, rows, 2 * nheads).transpose(0, 2, 1)
    n0r = n0.astype(F32).reshape(bsz, nheads, 1, dk)
    m0r = jnp.broadcast_to(m0.astype(F32).reshape(bsz, nheads, 1, 1), (bsz, nheads, 1, lanes))
    wq, wv = nheads * dk, nheads * dv
    rowblk = lambda b, c: rb0 + b * nc + c
    seq = lambda w, cb: pl.BlockSpec((rows, w), functools.partial(lambda b, c, cb_: (rowblk(b, c), cb_), cb_=cb))
    st_c = pl.BlockSpec((bb, nheads, dv, dk), lambda b, c: (b, 0, 0, 0))
    st_n = pl.BlockSpec((bb, nheads, 1, dk), lambda b, c: (b, 0, 0, 0))
    st_m = pl.BlockSpec((bb, nheads, 1, lanes), lambda b, c: (b, 0, 0, 0))
    est = 2 * 3 * _nbytes((rows, 2 * wq + 2 * wv), F32) + 5 * bb * _nbytes((nheads, dv, dk), F32) + (8 << 20)
    aliased = h_prev_out is not None
    kern = functools.partial(_mlstm_kernel, nheads=nheads, dk=dk, dv=dv, chunk=chunk, bb=bb, aliased=aliased)
    in_specs = [seq(wq, 0), seq(wq, 1), seq(wv, wq * 2 // wv), pl.BlockSpec((rows, 2 * nheads), lambda b, c: (rowblk(b, c), 0)),
                pl.BlockSpec((None, 2 * nheads, rows), lambda b, c: (b * nc + c, 0, 0)),
                pl.BlockSpec((1, 2 * nheads), lambda b, c: (0, 0)),
                pl.BlockSpec((2 * nheads, 1), lambda b, c: (0, 0)),
                pl.BlockSpec((1, wv), lambda b, c: (0, 0)),
                st_c, st_n, st_m]
    args = [qkvo, qkvo, qkvo, gates, g3t, _row(b_gates), b_gates.reshape(-1, 1), _row(norm_g),
            c0.astype(F32), n0r, m0r]
    io_alias = {}
    if aliased:
        in_specs.append(pl.BlockSpec(memory_space=pl.ANY))
        args.append(h_prev_out)
        io_alias = {len(args) - 1: 0}
    h, c_fin, n_fin, m_fin = pl.pallas_call(
        kern,
        grid=(bsz // bb, nc),
        in_specs=in_specs,
        out_specs=[pl.BlockSpec((rows, wv), lambda b, c: (rowblk(b, c), 0)), st_c, st_n, st_m],
        out_shape=[jax.ShapeDtypeStruct((m_rows, wv), F32),
                   jax.ShapeDtypeStruct((bsz, nheads, dv, dk), F32),
                   jax.ShapeDtypeStruct((bsz, nheads, 1, dk), F32),
                   jax.ShapeDtypeStruct((bsz, nheads, 1, lanes), F32)],
        scratch_shapes=[pltpu.VMEM((bb, nheads, dv, dk), F32), pltpu.VMEM((bb, nheads, 1, dk), F32),
                        pltpu.VMEM((bb, nheads, 1, lanes), F32)],
        input_output_aliases=io_alias,
        compiler_params=_params(("parallel", "arbitrary"), est),
        name="mlstm_chunk",
    )(*args)
    return h, c_fin, n_fin.reshape(bsz, nheads, dk), m_fin[:, :, 0, 0]


def _mlstm_layer(x, groups, states, p, wb):
    d = x.shape[1]
    nheads = ML_HEADS
    dv = d // nheads
    dk = dv // 2
    n_main = 2 * nheads * dk + 2 * nheads * dv
    xn = _rmsnorm(x, p['norm_g'], BF16)
    qkvo = _mm([_full(xn)], [(wb['ml_w_main'], (), 0)], n=n_main, out_dtype=F32, name="ml_in")
    gates = _mm([_full(xn)], [(wb['ml_w_gates'], (), 0)], n=2 * nheads, out_dtype=F32, name="ml_gates")
    h, new_states = None, []
    for grp, st in zip(groups, states):
        h, c_fin, n_fin, m_fin = _mlstm_group(qkvo, gates, p['ml_b_gates'], p['ml_norm_g'], st[0], st[1], st[2], grp, h)
        new_states.append((c_fin, n_fin, m_fin))
    og_block = (2 * nheads * dk + nheads * dv) // d
    x = _mm([(qkvo, d, og_block), _full(h)], [(wb['ml_w_o'], (), 0)], n=d, out_dtype=F32,
            prologue=lambda a, v_, pw: jax.nn.sigmoid(a[0]) * a[1],
            e_blocks=[x], epilogue=lambda accs, e, ev: e[0] + accs[0], name="ml_out")
    return x, new_states


def _dwconv_kernel(u_ref, buf_ref, w_ref, *rest, ksize, t, bb, rows, halo, aliased):
    if aliased:
        rest = rest[1:]
    y_ref, ext_ref = rest
    w = w_ref[...]
    sub = V7X_SUBLANES
    lead = halo - (ksize - 1)
    n_tiles = t // rows
    taps_by_shift = [[j for j in range(ksize) if (j + lead) % sub == r] for r in range(sub)]
    ext_ref[pl.ds(halo + t, sub), :] = jnp.zeros((sub, ext_ref.shape[1]), F32)

    for b in range(bb):
        ext_ref[pl.ds(lead, ksize - 1), :] = buf_ref[b]
        ext_ref[pl.ds(halo, t), :] = u_ref[pl.ds(b * t, t), :]

        def tile(ti, carry):
            base = ti * rows if isinstance(ti, int) else pl.multiple_of(ti * rows, sub)
            acc = None
            for r in range(sub):
                part = None
                for j in taps_by_shift[r]:
                    q8 = (j + lead) - r
                    term = w[j:j + 1, :] * ext_ref[pl.ds(base + q8, rows + sub), :]
                    part = term if part is None else part + term
                if part is not None:
                    shifted = part[r:r + rows, :]
                    acc = shifted if acc is None else acc + shifted
            y_ref[pl.ds(b * t + base, rows), :] = acc
            return carry

        if n_tiles == 1:
            tile(0, 0)
        else:
            lax.fori_loop(0, n_tiles, tile, 0)


def _dwconv_group(u, buf, w, group, y_prev_out):
    r0, bsz, t = group
    m_rows, c = u.shape
    ksize = w.shape[0]
    sub = V7X_SUBLANES
    halo = _round_up(ksize - 1, sub)
    bc = 256
    bb = 1 if t >= 64 else 8
    rows = min(t, 64)
    blk = bb * t
    assert bsz % bb == 0 and t % rows == 0 and c % bc == 0 and r0 % blk == 0 and blk % sub == 0
    rb0 = r0 // blk
    aliased = y_prev_out is not None
    est = 4 * _nbytes((blk, bc), F32) + _nbytes((halo + t + sub, bc), F32) + (4 << 20)
    kern = functools.partial(_dwconv_kernel, ksize=ksize, t=t, bb=bb, rows=rows, halo=halo, aliased=aliased)
    in_specs = [pl.BlockSpec((blk, bc), lambda b, ci: (rb0 + b, ci)),
                pl.BlockSpec((bb, ksize - 1, bc), lambda b, ci: (b, 0, ci)),
                pl.BlockSpec((ksize, bc), lambda b, ci: (0, ci))]
    args = [u, buf.astype(F32), w]
    io_alias = {}
    if aliased:
        in_specs.append(pl.BlockSpec(memory_space=pl.ANY))
        args.append(y_prev_out)
        io_alias = {3: 0}
    return pl.pallas_call(
        kern,
        grid=(bsz // bb, c // bc),
        in_specs=in_specs,
        out_specs=pl.BlockSpec((blk, bc), lambda b, ci: (rb0 + b, ci)),
        out_shape=jax.ShapeDtypeStruct((m_rows, c), F32),
        scratch_shapes=[pltpu.VMEM((halo + t + sub, bc), F32)],
        input_output_aliases=io_alias,
        compiler_params=_params(("parallel", "parallel"), est),
        name="dwconv",
    )(*args)


def _conv_groups(u, w_dw, groups, bufs):
    y, new_bufs = None, []
    c = u.shape[1]
    km1 = w_dw.shape[0] - 1
    for (r0, b, t), buf in zip(groups, bufs):
        y = _dwconv_group(u, buf, w_dw, (r0, b, t), y)
        if t >= km1:
            tail = r0 + t - km1 + jnp.arange(b)[:, None] * t + jnp.arange(km1)[None, :]
            new_bufs.append(u[tail])
        else:
            new_bufs.append(jnp.concatenate([buf.astype(F32)[:, t:], u[r0:r0 + b * t].reshape(b, t, c)], axis=1))
    return y, new_bufs


def _conformer_layer(x, groups, states, p, wb):
    d = x.shape[1]
    b1 = _row(p['cf_b_pw1'])
    w1 = wb['cf_w_pw1']
    u = _mm([_full(x)], [(w1, (), 0), (w1, (), d)], n=d, out_dtype=F32, avecs=[_row(p['norm_g'])],
            prologue=lambda a, v, pw: _rms(a[0]) * v[0], evecs=[b1[:, :d], b1[:, d:]],
            epilogue=lambda accs, e, ev: (accs[0] + ev[0]) * jax.nn.sigmoid(accs[1] + ev[1]), name="cf_pw1")
    y, new_bufs = _conv_groups(u, p['cf_w_dw'], groups, [st[0] for st in states])

    def ln_silu(a, v, pw):
        z = a[0] + v[0]
        mu = jnp.mean(z, axis=-1, keepdims=True)
        var = jnp.mean(jnp.square(z - mu), axis=-1, keepdims=True)
        z = (z - mu) * lax.rsqrt(var + CF_LN_EPS) * v[1] + v[2]
        return z * jax.nn.sigmoid(z)

    x = _mm([_full(y)], [(wb['cf_w_pw2'], (), 0)], n=d, out_dtype=F32, prologue=ln_silu,
            avecs=[_row(p['cf_b_dw']), _row(p['cf_ln_g']), _row(p['cf_ln_b'])],
            e_blocks=[x], evecs=[_row(p['cf_b_pw2'])],
            epilogue=lambda accs, e, ev: e[0] + (accs[0] + ev[0]), name="cf_pw2")
    return x, [(nb,) for nb in new_bufs]


def _sconv_layer(x, groups, states, p, wb):
    d = x.shape[1]
    xn = _rmsnorm(x, p['norm_g'], BF16)
    w_in = wb['sc_w_in']
    b_gate = _mm([_full(xn)], [(w_in, (), 0)], n=d, out_dtype=F32, name="sc_b")
    ch = _mm([_full(xn)], [(w_in, (), d), (w_in, (), 2 * d)], n=d, out_dtype=F32,
             epilogue=lambda accs, e, ev: accs[0] * accs[1], name="sc_ch")
    y, new_bufs = _conv_groups(ch, p['sc_w_dw'], groups, [st[0] for st in states])
    x = _mm([_full(b_gate), _full(y)], [(wb['sc_w_out'], (), 0)], n=d, out_dtype=F32,
            prologue=lambda a, v, pw: a[0] * a[1], e_blocks=[x],
            epilogue=lambda accs, e, ev: e[0] + accs[0], name="sc_out")
    return x, [(nb,) for nb in new_bufs]


def kernel(x_prompt, x_sample, state_rwkv_wkv, state_rwkv_shift, state_mlstm_C, state_mlstm_n, state_mlstm_m,
           state_conf_conv, state_sconv_conv, norm_g, ffn_w_gu, ffn_w_down, final_norm_g,
           rw_mu, rw_w_rkv, rw_w0, rw_w1, rw_w2, rw_a0, rw_a1, rw_a2, rw_g1, rw_g2, rw_k_k, rw_k_a, rw_r_k,
           rw_ln_g, rw_ln_b, rw_w_o, ml_w_in, ml_b_gates, ml_norm_g, ml_w_o,
           cf_w_pw1, cf_b_pw1, cf_w_dw, cf_b_dw, cf_ln_g, cf_ln_b, cf_w_pw2, cf_b_pw2,
           sc_w_in, sc_w_dw, sc_w_out):
    bp, tp, d = x_prompt.shape
    bs, ts, _ = x_sample.shape
    depth = norm_g.shape[0]
    n_mixers = 4
    groups = [(0, bp, tp), (bp * tp, bs, ts)]
    x = jnp.concatenate([x_prompt.reshape(bp * tp, d), x_sample.reshape(bs * ts, d)], axis=0).astype(F32)

    dv = d // ML_HEADS
    dk = dv // 2
    cf_k = cf_w_dw.shape[1]
    sc_k = sc_w_dw.shape[1]
    zeros = lambda *s: jnp.zeros(s, F32)
    w_gu_b = ffn_w_gu
    w_down_b = ffn_w_down.astype(BF16)

    out_states = {name: ([], []) for name in ('wkv', 'shift', 'C', 'n', 'm', 'cf', 'sc')}
    for i in range(depth):
        j = i // n_mixers
        kind = i % n_mixers
        x = _ffn(x, norm_g[i, 0], w_gu_b, w_down_b, (i, 0))
        if kind == 0:
            p = dict(norm_g=norm_g[i, 1], rw_mu=rw_mu[j], rw_w0=rw_w0[j], rw_a0=rw_a0[j], rw_k_k=rw_k_k[j],
                     rw_k_a=rw_k_a[j], rw_r_k=rw_r_k[j], rw_ln_g=rw_ln_g[j], rw_ln_b=rw_ln_b[j])
            wb = dict(rw_w_rkv=rw_w_rkv[j].astype(BF16), rw_w1=rw_w1[j].astype(BF16), rw_w2=rw_w2[j].astype(BF16),
                      rw_a1=rw_a1[j].astype(BF16), rw_a2=rw_a2[j].astype(BF16), rw_g1=rw_g1[j].astype(BF16),
                      rw_g2=rw_g2[j].astype(BF16), rw_w_o=rw_w_o[j].astype(BF16))
            states = [(None, zeros(bp, d)), (state_rwkv_wkv[j], state_rwkv_shift[j].astype(F32))]
            x, new = _rwkv_layer(x, groups, states, p, wb)
            for gi in range(2):
                out_states['wkv'][gi].append(new[gi][0])
                out_states['shift'][gi].append(new[gi][1])
        elif kind == 1:
            p = dict(norm_g=norm_g[i, 1], ml_b_gates=ml_b_gates[j], ml_norm_g=ml_norm_g[j])
            n_main = 2 * ML_HEADS * dk + 2 * ML_HEADS * dv
            wb = dict(ml_w_main=ml_w_in[j][:, :n_main].astype(BF16), ml_w_gates=ml_w_in[j][:, n_main:].astype(BF16),
                      ml_w_o=ml_w_o[j].astype(BF16))
            states = [(zeros(bp, ML_HEADS, dv, dk), zeros(bp, ML_HEADS, dk), zeros(bp, ML_HEADS)),
                      (state_mlstm_C[j], state_mlstm_n[j], state_mlstm_m[j])]
            x, new = _mlstm_layer(x, groups, states, p, wb)
            for gi in range(2):
                out_states['C'][gi].append(new[gi][0])
                out_states['n'][gi].append(new[gi][1])
                out_states['m'][gi].append(new[gi][2])
        elif kind == 2:
            p = dict(norm_g=norm_g[i, 1], cf_b_pw1=cf_b_pw1[j], cf_w_dw=cf_w_dw[j], cf_b_dw=cf_b_dw[j],
                     cf_ln_g=cf_ln_g[j], cf_ln_b=cf_ln_b[j], cf_b_pw2=cf_b_pw2[j])
            wb = dict(cf_w_pw1=cf_w_pw1[j].astype(BF16), cf_w_pw2=cf_w_pw2[j].astype(BF16))
            states = [(zeros(bp, cf_k - 1, d),), (state_conf_conv[j],)]
            x, new = _conformer_layer(x, groups, states, p, wb)
            for gi in range(2):
                out_states['cf'][gi].append(new[gi][0])
        else:
            p = dict(norm_g=norm_g[i, 1], sc_w_dw=sc_w_dw[j])
            wb = dict(sc_w_in=sc_w_in[j].astype(BF16), sc_w_out=sc_w_out[j].astype(BF16))
            states = [(zeros(bp, sc_k - 1, d),), (state_sconv_conv[j],)]
            x, new = _sconv_layer(x, groups, states, p, wb)
            for gi in range(2):
                out_states['sc'][gi].append(new[gi][0])
        x = _ffn(x, norm_g[i, 2], w_gu_b, w_down_b, (i, 1))

    y_prompt = _rmsnorm(x, final_norm_g, x_prompt.dtype, 0, bp * tp).reshape(bp, tp, d)
    y_sample = _rmsnorm(x, final_norm_g, x_sample.dtype, bp * tp, bs * ts).reshape(bs, ts, d)
    order = ('wkv', 'shift', 'C', 'n', 'm', 'cf', 'sc')
    dt = x_prompt.dtype
    prompt_out = tuple(jnp.stack(out_states[name][0]).astype(dt) for name in order)
    sample_out = tuple(jnp.stack(out_states[name][1]).astype(st.dtype) for name, st in
                       zip(order, (state_rwkv_wkv, state_rwkv_shift, state_mlstm_C, state_mlstm_n, state_mlstm_m,
                                   state_conf_conv, state_sconv_conv)))
    return (y_prompt, y_sample) + prompt_out + sample_out
```

```python
import functools
import math

import jax
import jax.numpy as jnp
from jax import lax
from jax.experimental import pallas as pl
from jax.experimental.pallas import tpu as pltpu

F32 = jnp.float32
BF16 = jnp.bfloat16

NORM_EPS = 1e-6
RW_HEAD = 64
RW_GN_EPS = 64e-5
ML_HEADS = 8
ML_GATE_CAP = 15.0
ML_NORM_EPS = 1e-6
CF_LN_EPS = 1e-5

V7X_LANES = 128
V7X_SUBLANES = 8
V7X_VMEM_BYTES = 64 * 1024 * 1024
VMEM_LIMIT_CAP = V7X_VMEM_BYTES - 6 * 1024 * 1024


def _nbytes(shape, dtype):
    return math.prod(shape) * jnp.dtype(dtype).itemsize


def _params(sem, vmem_est):
    limit = int(min(max(vmem_est * 1.2 + (2 << 20), 16 << 20), VMEM_LIMIT_CAP))
    return pltpu.CompilerParams(dimension_semantics=sem, vmem_limit_bytes=limit)


def _pick_bm(m, target, also_divides=()):
    best = None
    for bm in range(16, min(m, target) + 1, 16):
        if m % bm == 0 and all(o % bm == 0 for o in also_divides):
            best = bm
    assert best is not None
    return best


def _rms(x):
    return x * lax.rsqrt(jnp.mean(x * x, axis=-1, keepdims=True) + NORM_EPS)


def _round_up(x, m):
    return -(-x // m) * m


def _mm_kernel(*refs, counts, prologue, epilogue, use_scratch, has_side):
    n_a, n_av, n_pw, n_w, n_e, n_ev = counts
    pos = 0
    groups = []
    for c in counts:
        groups.append(refs[pos:pos + c])
        pos += c
    a_refs, av_refs, pw_refs, w_refs, e_refs, ev_refs = groups
    if has_side:
        side_in, o_ref, side_out = refs[pos:pos + 3]
        side_out[...] = side_in[...].astype(side_out.dtype)
    else:
        o_ref = refs[pos]
    if use_scratch:
        lhs_ref = refs[-1]

        @pl.when(pl.program_id(1) == 0)
        def _():
            lhs_ref[...] = prologue([r[...] for r in a_refs], [r[...] for r in av_refs],
                                    [r[...] for r in pw_refs]).astype(BF16)

        lhs = lhs_ref[...]
    else:
        lhs = a_refs[0][...]
    accs = [jnp.dot(lhs, w[...].astype(BF16), preferred_element_type=F32) for w in w_refs]
    out = epilogue(accs, [r[...] for r in e_refs], [r[...] for r in ev_refs])
    o_ref[...] = out.astype(o_ref.dtype)


def _mm(a_list, w_list, *, n, out_dtype, prologue=None, epilogue=None, avecs=(), pw_list=(), e_blocks=(), evecs=(),
        side_cast=None, bm_target=1088, bn=512, name="mm"):
    m = a_list[0][0].shape[0]
    k = w_list[0][0].shape[-2]
    bm = _pick_bm(m, bm_target)
    bn = min(bn, n)
    assert n % bn == 0
    use_scratch = prologue is not None or a_list[0][0].dtype != BF16
    if prologue is None:
        prologue = lambda a, v, pw: a[0]
    if epilogue is None:
        epilogue = lambda accs, e, ev: accs[0]

    in_specs, args, est = [], [], 0
    for arr, ka, cb in a_list:
        in_specs.append(pl.BlockSpec((bm, ka), functools.partial(lambda i, j, c: (i, c), c=cb)))
        args.append(arr)
        est += 2 * _nbytes((bm, ka), arr.dtype)
    for v in avecs:
        in_specs.append(pl.BlockSpec(v.shape, lambda i, j: (0, 0)))
        args.append(v)
    for pw in pw_list:
        in_specs.append(pl.BlockSpec(pw.shape, lambda i, j: (0, 0)))
        args.append(pw)
        est += 2 * _nbytes(pw.shape, pw.dtype)
    for arr, lead, off in w_list:
        assert off % bn == 0 and arr.shape[-2] == k
        in_specs.append(pl.BlockSpec((None,) * len(lead) + (k, bn),
                                     functools.partial(lambda i, j, l, o: l + (0, j + o), l=tuple(lead), o=off // bn)))
        args.append(arr)
        est += 2 * _nbytes((k, bn), arr.dtype) + (_nbytes((k, bn), BF16) if arr.dtype != BF16 else 0)
    for e in e_blocks:
        in_specs.append(pl.BlockSpec((bm, bn), lambda i, j: (i, j)))
        args.append(e)
        est += 2 * _nbytes((bm, bn), e.dtype)
    for v in evecs:
        in_specs.append(pl.BlockSpec((1, bn), lambda i, j: (0, j)))
        args.append(v)
    est += 2 * _nbytes((bm, bn), out_dtype) + (len(w_list) + 2) * _nbytes((bm, bn), F32)
    scratch = []
    if use_scratch:
        scratch.append(pltpu.VMEM((bm, k), BF16))
        est += _nbytes((bm, k), BF16) + 2 * _nbytes((bm, a_list[0][1]), F32)

    grid = (m // bm, n // bn)
    out_specs = pl.BlockSpec((bm, bn), lambda i, j: (i, j))
    out_shape = jax.ShapeDtypeStruct((m, n), out_dtype)
    if side_cast is not None:
        s_arr, s_lead = side_cast
        s_rows, s_cols = s_arr.shape[-2:]
        steps = grid[0] * grid[1]
        assert s_rows % (steps * 2 * V7X_SUBLANES) == 0
        rps = s_rows // steps
        nj = grid[1]
        in_specs.append(pl.BlockSpec((None,) * len(s_lead) + (rps, s_cols),
                                     functools.partial(lambda i, j, l: l + (i * nj + j, 0), l=tuple(s_lead))))
        args.append(s_arr)
        out_specs = [out_specs, pl.BlockSpec((rps, s_cols), lambda i, j: (i * nj + j, 0))]
        out_shape = [out_shape, jax.ShapeDtypeStruct((s_rows, s_cols), BF16)]
        est += 2 * _nbytes((rps, s_cols), F32) + 2 * _nbytes((rps, s_cols), BF16)

    counts = (len(a_list), len(avecs), len(pw_list), len(w_list), len(e_blocks), len(evecs))
    kern = functools.partial(_mm_kernel, counts=counts, prologue=prologue, epilogue=epilogue, use_scratch=use_scratch,
                             has_side=side_cast is not None)
    return pl.pallas_call(
        kern,
        grid=grid,
        in_specs=in_specs,
        out_specs=out_specs,
        out_shape=out_shape,
        scratch_shapes=scratch,
        compiler_params=_params(("parallel", "arbitrary"), est),
        name=name,
    )(*args)


def _full(arr):
    return (arr, arr.shape[1], 0)


def _row(v):
    return v.reshape(1, -1)


def _rmsnorm_kernel(x_ref, g_ref, o_ref):
    o_ref[...] = (_rms(x_ref[...]) * g_ref[...]).astype(o_ref.dtype)


def _rmsnorm(x, g, out_dtype, row0=0, nrows=None):
    d = x.shape[1]
    m = x.shape[0] if nrows is None else nrows
    bm = _pick_bm(m, 1088, also_divides=(row0,) if row0 else ())
    rb0 = row0 // bm
    est = 2 * _nbytes((bm, d), F32) * 3
    return pl.pallas_call(
        _rmsnorm_kernel,
        grid=(m // bm,),
        in_specs=[pl.BlockSpec((bm, d), lambda i: (rb0 + i, 0)), pl.BlockSpec((1, d), lambda i: (0, 0))],
        out_specs=pl.BlockSpec((bm, d), lambda i: (i, 0)),
        out_shape=jax.ShapeDtypeStruct((m, d), out_dtype),
        compiler_params=_params(("parallel",), est),
        name="rmsnorm",
    )(x, _row(g))


def _ffn(x, g, w_gu, w_down, lead):
    d_ff = w_down.shape[-2]
    h, w_down_b = _mm([_full(x)], [(w_gu, lead, 0), (w_gu, lead, d_ff)], n=d_ff, out_dtype=BF16, avecs=[_row(g)],
                      prologue=lambda a, v, pw: _rms(a[0]) * v[0],
                      epilogue=lambda accs, e, ev: accs[0] * jax.nn.sigmoid(accs[0]) * accs[1],
                      side_cast=(w_down, lead), name="ffn_up")
    return _mm([_full(h)], [(w_down_b, (), 0)], n=x.shape[1], out_dtype=F32, e_blocks=[x],
               epilogue=lambda accs, e, ev: e[0] + 0.5 * accs[0], name="ffn_down")


def _rw_mix_kernel(x_ref, g_ref, mu_ref, ov_ref, h_ref, *rest, n_prompt_blocks, blocks_per_seq, ts):
    outs, carry_ref = rest[:-1], rest[-1]
    pid = pl.program_id(0)

    @pl.when(pid == 0)
    def _():
        carry_ref[...] = jnp.zeros_like(carry_ref)

    h = _rms(x_ref[...]) * g_ref[...]
    h_ref[...] = h
    bm = h.shape[0]
    rolled = pltpu.roll(h, 1, axis=0)
    row = lax.broadcasted_iota(jnp.int32, h.shape, 0)
    carry = jnp.where(pid % blocks_per_seq == 0, 0.0, carry_ref[...])
    prev_prompt = jnp.where(row == 0, carry, rolled)
    prev_sample = jnp.where(row % ts == 0, ov_ref[...], rolled)
    prev = jnp.where(pid >= n_prompt_blocks, prev_sample, prev_prompt)
    carry_ref[...] = h[bm - 1:bm, :]
    delta = prev - h
    for i, o in enumerate(outs):
        o[...] = (h + delta * mu_ref[i:i + 1, :]).astype(BF16)


def _rw_mix(x, g, mu, shift_sample, groups):
    m, d = x.shape
    (_, bp, tp), (r0s, bs, ts) = groups
    bm = _pick_bm(tp, 512, also_divides=(bs * ts,))
    assert bm % ts == 0 and r0s == bp * tp
    n_prompt_blocks = (bp * tp) // bm
    ov = jnp.zeros((bs, ts, d), F32).at[:, 0].set(shift_sample.astype(F32)).reshape(bs * ts, d)
    n_mix = mu.shape[0]
    blk = pl.BlockSpec((bm, d), lambda i: (i, 0))
    est = 2 * _nbytes((bm, d), F32) * 3 + 2 * n_mix * _nbytes((bm, d), BF16) + 6 * _nbytes((bm, d), F32)
    kern = functools.partial(_rw_mix_kernel, n_prompt_blocks=n_prompt_blocks, blocks_per_seq=tp // bm, ts=ts)
    outs = pl.pallas_call(
        kern,
        grid=(m // bm,),
        in_specs=[blk, pl.BlockSpec((1, d), lambda i: (0, 0)), pl.BlockSpec((n_mix, d), lambda i: (0, 0)),
                  pl.BlockSpec((bm, d), lambda i: (jnp.maximum(i - n_prompt_blocks, 0), 0))],
        out_specs=[blk] * (1 + n_mix),
        out_shape=[jax.ShapeDtypeStruct((m, d), F32)] + [jax.ShapeDtypeStruct((m, d), BF16)] * n_mix,
        scratch_shapes=[pltpu.VMEM((1, d), F32)],
        compiler_params=_params(("arbitrary",), est),
        name="rw_mix",
    )(x, _row(g), mu, ov)
    return outs[0], outs[1:]


def _rwkv_steps(r_ref, k_ref, v_ref, w_ref, a_ref, y_ref, param_refs, s_ref, tb, nh):
    kk_p, ka_p, rk_p, ln_g, ln_b = (q[...] for q in param_refs)
    vgrp = V7X_SUBLANES

    def step(t, carry):
        r = r_ref[t]
        k = k_ref[t]
        v = v_ref[t]
        logw = -jax.nn.softplus(-w_ref[t]) - 0.5
        d = jnp.exp(-jnp.exp(logw))
        a = jax.nn.sigmoid(a_ref[t])
        kk = k * kk_p
        kk = kk / jnp.maximum(jnp.sqrt(jnp.sum(kk * kk, axis=0, keepdims=True)), 1e-12)
        kmod = k * (1.0 + (a - 1.0) * ka_p)
        b = kk * a

        def vgroup(g, c2):
            base = pl.multiple_of(g * vgrp, vgrp)
            v8 = v_ref[t, pl.ds(base, vgrp), :]
            ys = []
            for j in range(vgrp):
                s_old = s_ref[base + j]
                sa = jnp.sum(s_old * kk, axis=0, keepdims=True)
                s_new = s_old * d - sa * b + v8[j:j + 1, :] * kmod
                s_ref[base + j] = s_new
                ys.append(jnp.sum(s_new * r, axis=0, keepdims=True))
            y_ref[t, pl.ds(base, vgrp), :] = jnp.concatenate(ys, axis=0)
            return c2

        lax.fori_loop(0, nh // vgrp, vgroup, 0)

        y = y_ref[t]
        mean = jnp.mean(y, axis=0, keepdims=True)
        var = jnp.mean(jnp.square(y - mean), axis=0, keepdims=True)
        yn = (y - mean) * lax.rsqrt(var + RW_GN_EPS) * ln_g + ln_b
        bonus = jnp.sum(r * kmod * rk_p, axis=0, keepdims=True) * v
        y_ref[t] = yn + bonus
        return carry

    lax.fori_loop(0, tb, step, 0)


def _rwkv_scan_kernel(r_ref, k_ref, v_ref, w_ref, a_ref, kk_p_ref, ka_p_ref, rk_p_ref, lg_ref, lb_ref, s0_ref,
                      y_ref, sT_ref, s_ref, *, tb, nh):
    tblk = pl.program_id(1)

    @pl.when(tblk == 0)
    def _():
        s_ref[...] = s0_ref[...]

    _rwkv_steps(r_ref, k_ref, v_ref, w_ref, a_ref, y_ref, (kk_p_ref, ka_p_ref, rk_p_ref, lg_ref, lb_ref), s_ref, tb, nh)

    @pl.when(tblk == pl.num_programs(1) - 1)
    def _():
        sT_ref[...] = s_ref[...]


def _rwkv_scan(r, k, v, w_pre, a_pre, kk_p, ka_p, rk_p, ln_g, ln_b, s0, *, tb):
    t, nh, c = r.shape
    lanes = V7X_LANES
    assert c % lanes == 0 and t % tb == 0
    seq = pl.BlockSpec((tb, nh, lanes), lambda ci, ti: (ti, 0, ci))
    par = pl.BlockSpec((nh, lanes), lambda ci, ti: (0, ci))
    st = pl.BlockSpec((nh, nh, lanes), lambda ci, ti: (0, 0, ci))
    est = 6 * 2 * _nbytes((tb, nh, lanes), F32) + 5 * _nbytes((nh, nh, lanes), F32)
    kern = functools.partial(_rwkv_scan_kernel, tb=tb, nh=nh)
    return pl.pallas_call(
        kern,
        grid=(c // lanes, t // tb),
        in_specs=[seq] * 5 + [par] * 5 + [st],
        out_specs=[seq, st],
        out_shape=[jax.ShapeDtypeStruct((t, nh, c), F32), jax.ShapeDtypeStruct((nh, nh, c), F32)],
        scratch_shapes=[pltpu.VMEM((nh, nh, lanes), F32)],
        compiler_params=_params(("parallel", "arbitrary"), est),
        name="rwkv_scan",
    )(r, k, v, w_pre, a_pre, kk_p, ka_p, rk_p, ln_g, ln_b, s0)


def _to_chains(x, bsz, t, nheads):
    return x.reshape(bsz, t, nheads, RW_HEAD).transpose(1, 3, 0, 2).reshape(t, RW_HEAD, bsz * nheads)


def _from_chains(y, bsz, t, nheads):
    return y.reshape(t, RW_HEAD, bsz, nheads).transpose(2, 0, 3, 1).reshape(bsz * t, nheads * RW_HEAD)


def _rwkv_group(r, k, v, w_pre, a_pre, wkv0, p, bsz, t):
    d_model = r.shape[1]
    nheads = d_model // RW_HEAD
    c = bsz * nheads
    chain_param = lambda q: jnp.tile(q.reshape(nheads, RW_HEAD).T, (1, bsz))
    if wkv0 is None:
        s0 = jnp.zeros((RW_HEAD, RW_HEAD, c), F32)
    else:
        s0 = wkv0.astype(F32).transpose(2, 3, 0, 1).reshape(RW_HEAD, RW_HEAD, c)
    tb = t if t <= 32 else 32
    y, s_fin = _rwkv_scan(*[_to_chains(q, bsz, t, nheads) for q in (r, k, v, w_pre, a_pre)],
                          *[chain_param(p[name]) for name in ('rw_k_k', 'rw_k_a', 'rw_r_k', 'rw_ln_g', 'rw_ln_b')],
                          s0, tb=tb)
    s_fin = s_fin.reshape(RW_HEAD, RW_HEAD, bsz, nheads).transpose(2, 3, 0, 1)
    return _from_chains(y, bsz, t, nheads), s_fin


def _rwkv_chunk_kernel(r_ref, k_ref, v_ref, w_ref, a_ref, g_ref, kkp_ref, kap_ref, rkp_ref, lg_ref, lb_ref,
                       y_ref, sT_ref, s2_ref, *, chunk, npairs, unroll):
    L = chunk
    lanes = V7X_LANES
    nh = RW_HEAD
    ci = pl.program_id(1)

    @pl.when(ci == 0)
    def _():
        s2_ref[...] = jnp.zeros_like(s2_ref)

    lane_l = lax.broadcasted_iota(jnp.int32, (L, lanes), 1)
    row_l = lax.broadcasted_iota(jnp.int32, (L, lanes), 0)
    head0 = lane_l < nh
    n2 = 2 * L
    ti = lax.broadcasted_iota(jnp.int32, (n2, n2), 0)
    si = lax.broadcasted_iota(jnp.int32, (n2, n2), 1)
    strict = si < ti
    incl = si <= ti
    sq_r = lax.broadcasted_iota(jnp.int32, (lanes, lanes), 0) // nh
    sq_c = lax.broadcasted_iota(jnp.int32, (lanes, lanes), 1) // nh
    seg_ones = jnp.where(sq_r == sq_c, 1.0, 0.0).astype(BF16)
    nt = (((1,), (1,)), ((), ()))
    tn = (((0,), (0,)), ((), ()))

    def seg_sums(xs):
        his = [x.astype(BF16) for x in xs]
        los = [(x - hi.astype(F32)).astype(BF16) for x, hi in zip(xs, his)]
        return [jnp.dot(hi, seg_ones, preferred_element_type=F32) + jnp.dot(lo, seg_ones, preferred_element_type=F32)
                for hi, lo in zip(his, los)]

    def stack(x):
        return jnp.concatenate([jnp.where(head0, x, 0.0), jnp.where(head0, 0.0, x)], axis=0)

    dot = lambda x, y: jnp.dot(x, y, preferred_element_type=F32)
    dot_nt = lambda x, y: lax.dot_general(x, y, nt, preferred_element_type=F32)
    dot_tn = lambda x, y: lax.dot_general(x, y, tn, preferred_element_type=F32)

    def wave(ps):
        idx = range(len(ps))
        ls = [pl.ds(p * lanes, lanes) for p in ps]
        r = [r_ref[:, s] for s in ls]
        v = [v_ref[:, s] for s in ls]
        logd, cl, kk0, kmod, a = [], [], [], [], []
        for i in idx:
            k_i = k_ref[:, ls[i]]
            ld = -jnp.exp(-jax.nn.softplus(-w_ref[:, ls[i]]) - 0.5)
            c = ld
            sh = 1
            while sh < L:
                c = c + jnp.where(row_l >= sh, pltpu.roll(c, sh, axis=0), 0.0)
                sh *= 2
            a_i = jax.nn.sigmoid(a_ref[:, ls[i]])
            logd.append(ld)
            cl.append(c)
            a.append(a_i)
            kk0.append(k_i * kkp_ref[:, ls[i]])
            kmod.append(k_i * (1.0 + (a_i - 1.0) * kap_ref[:, ls[i]]))
        sums = seg_sums([kk0[i] * kk0[i] for i in idx] + [r[i] * kmod[i] * rkp_ref[:, ls[i]] for i in idx])
        nrm2 = sums[:len(ps)]
        bonus = [sums[len(ps) + i] * v[i] for i in idx]
        x1, x2, vs, d_last = [], [], [], []
        for i in idx:
            kk = kk0[i] / jnp.maximum(jnp.sqrt(nrm2[i]), 1e-12)
            d_inc = jnp.exp(cl[i])
            d_inv = jnp.exp(-cl[i])
            xa = stack(kk * jnp.exp(cl[i] - logd[i]))
            xr = stack(r[i] * d_inc)
            xb = stack(kk * a[i] * d_inv)
            xk = stack(kmod[i] * d_inv)
            x1.append(jnp.concatenate([xa, xr], axis=0).astype(BF16))
            x2.append(jnp.concatenate([xb, xk], axis=0).astype(BF16))
            vs.append(stack(v[i]))
            d_last.append(d_inc[L - 1:L, :])
        s2 = [s2_ref[p] for p in ps]
        g = [dot_nt(x1[i], x2[i]) for i in idx]
        pz = [dot_nt(x1[i], s2[i].astype(BF16)) for i in idx]
        m_k = [jnp.where(strict, g[i][:n2, n2:], 0.0).astype(BF16) for i in idx]
        m_pow = [jnp.where(strict, g[i][:n2, :n2], 0.0).astype(BF16) for i in idx]
        acat = [jnp.concatenate([jnp.where(incl, g[i][n2:, :n2], 0.0), jnp.where(incl, g[i][n2:, n2:], 0.0)],
                                axis=1).astype(BF16) for i in idx]
        vsb = [vs[i].astype(BF16) for i in idx]
        rhs = [pz[i][:n2] + dot(m_k[i], vsb[i]) for i in idx]
        u = [rhs[i] - dot(m_pow[i], rhs[i].astype(BF16)) for i in idx]
        pw = 2
        while pw < L:
            m_pow = [dot(m_pow[i], m_pow[i]).astype(BF16) for i in idx]
            u = [u[i] + dot(m_pow[i], u[i].astype(BF16)) for i in idx]
            pw *= 2
        wcat = [jnp.concatenate([-u[i], vs[i]], axis=0).astype(BF16) for i in idx]
        ys = [pz[i][n2:] + dot(acat[i], wcat[i]) for i in idx]
        upd = [dot_tn(wcat[i], x2[i]) for i in idx]
        for i in idx:
            s2_ref[ps[i]] = (s2[i] + upd[i]) * d_last[i]
        y = [ys[i][:L] + ys[i][L:] for i in idx]
        mean = [z * (1.0 / nh) for z in seg_sums(y)]
        yc = [y[i] - mean[i] for i in idx]
        var = [z * (1.0 / nh) for z in seg_sums([yc[i] * yc[i] for i in idx])]
        for i in idx:
            yn = yc[i] * lax.rsqrt(var[i] + RW_GN_EPS) * lg_ref[:, ls[i]] + lb_ref[:, ls[i]]
            y_ref[:, ls[i]] = ((yn + bonus[i]) * g_ref[:, ls[i]]).astype(y_ref.dtype)

    for p0 in range(0, npairs, unroll):
        wave(list(range(p0, p0 + unroll)))

    @pl.when(ci == pl.num_programs(1) - 1)
    def _():
        for p in range(npairs):
            blk = s2_ref[p]
            sT_ref[2 * p] = blk[:nh, :nh]
            sT_ref[2 * p + 1] = blk[nh:, nh:]


def _rwkv_chunked(arrs, params, row0, bsz, t):
    m, d = arrs[0].shape
    lanes = V7X_LANES
    chunk = min(t, 64)
    npairs = d // lanes
    assert t % chunk == 0 and row0 % chunk == 0 and d % lanes == 0 and lanes == 2 * RW_HEAD
    nc = t // chunk
    rb0 = row0 // chunk
    seq = pl.BlockSpec((chunk, d), lambda b, c: (rb0 + b * nc + c, 0))
    vec = pl.BlockSpec((1, d), lambda b, c: (0, 0))
    est = 7 * 2 * _nbytes((chunk, d), F32) + 3 * _nbytes((npairs, lanes, lanes), F32) + (8 << 20)
    kern = functools.partial(_rwkv_chunk_kernel, chunk=chunk, npairs=npairs, unroll=16)
    return pl.pallas_call(
        kern,
        grid=(bsz, nc),
        in_specs=[seq] * 6 + [vec] * 5,
        out_specs=[seq, pl.BlockSpec((None, 2 * npairs, RW_HEAD, RW_HEAD), lambda b, c: (b, 0, 0, 0))],
        out_shape=[jax.ShapeDtypeStruct((m, d), BF16),
                   jax.ShapeDtypeStruct((bsz, 2 * npairs, RW_HEAD, RW_HEAD), F32)],
        scratch_shapes=[pltpu.VMEM((npairs, lanes, lanes), F32)],
        compiler_params=_params(("parallel", "arbitrary"), est),
        name="rwkv_chunk",
    )(*arrs, *[_row(q) for q in params])


def _rwkv_layer(x, groups, states, p, wb):
    d = x.shape[1]
    h, (xr, xw, xk, xv, xa, xg) = _rw_mix(x, p['norm_g'], p['rw_mu'], states[1][1], groups)
    w_rkv = wb['rw_w_rkv']
    r = _mm([_full(xr)], [(w_rkv, (0,), 0)], n=d, out_dtype=F32, name="rw_r")
    k = _mm([_full(xk)], [(w_rkv, (1,), 0)], n=d, out_dtype=F32, name="rw_k")
    v = _mm([_full(xv)], [(w_rkv, (2,), 0)], n=d, out_dtype=F32, name="rw_v")
    add_vec = lambda accs, e, ev: ev[0] + accs[0]
    lora = lambda act: (lambda a, v_, pw: act(jnp.dot(a[0], pw[0], preferred_element_type=F32)))
    w_pre = _mm([_full(xw)], [(wb['rw_w2'], (), 0)], n=d, out_dtype=F32, pw_list=[wb['rw_w1']],
                prologue=lora(jnp.tanh), evecs=[_row(p['rw_w0'])], epilogue=add_vec, name="rw_w")
    a_pre = _mm([_full(xa)], [(wb['rw_a2'], (), 0)], n=d, out_dtype=F32, pw_list=[wb['rw_a1']],
                prologue=lora(lambda z: z), evecs=[_row(p['rw_a0'])], epilogue=add_vec, name="rw_a")
    g = _mm([_full(xg)], [(wb['rw_g2'], (), 0)], n=d, out_dtype=F32, pw_list=[wb['rw_g1']],
            prologue=lora(jax.nn.sigmoid), name="rw_g")

    y, new_states = None, []
    for (r0, b, t), st in zip(groups, states):
        sl = slice(r0, r0 + b * t)
        if st[0] is None:
            assert y is None
            y, s_fin = _rwkv_chunked((r, k, v, w_pre, a_pre, g),
                                     [p[name] for name in ('rw_k_k', 'rw_k_a', 'rw_r_k', 'rw_ln_g', 'rw_ln_b')], r0, b, t)
        else:
            y_g, s_fin = _rwkv_group(r[sl], k[sl], v[sl], w_pre[sl], a_pre[sl], st[0], p, b, t)
            y = lax.dynamic_update_slice(y, (y_g * g[sl]).astype(y.dtype), (r0, 0))
        new_states.append((s_fin, lax.slice(h, (r0 + t - 1, 0), (r0 + b * t, d), (t, 1))))
    x = _mm([_full(y)], [(wb['rw_w_o'], (), 0)], n=d, out_dtype=F32, e_blocks=[x],
            epilogue=lambda accs, e, ev: e[0] + accs[0], name="rw_out")
    return x, new_states


def _mlstm_kernel(q_ref, k_ref, v_ref, og_ref, gt_ref, gtt_ref, bg_ref, bgt_ref, ng_ref, c0_ref, n0_ref, m0_ref, *rest,
                  nheads, dk, dv, chunk, bb, aliased):
    if aliased:
        rest = rest[1:]
    h_ref, cT_ref, nT_ref, mT_ref, c_ref, n_ref, m_ref = rest
    ci = pl.program_id(1)

    @pl.when(ci == 0)
    def _():
        c_ref[...] = c0_ref[...]
        n_ref[...] = n0_ref[...]
        m_ref[...] = m0_ref[...]

    L = chunk
    cap = lambda z: ML_GATE_CAP * jnp.tanh(z / ML_GATE_CAP)
    gates_c_all = cap(gt_ref[...] + bg_ref[...])
    gates_r_all = cap(gtt_ref[...] + bgt_ref[...])
    row_id = lax.broadcasted_iota(jnp.int32, (L, L), 0)
    col_id = lax.broadcasted_iota(jnp.int32, (L, L), 1)
    causal = col_id <= row_id
    k_scale = dk ** -0.5

    nt = (((1,), (1,)), ((), ()))
    tn = (((0,), (0,)), ((), ()))
    chains = [(bi, hd) for bi in range(bb) for hd in range(nheads)]
    idx = range(len(chains))
    rows = [slice(bi * L, (bi + 1) * L) for bi, _ in chains]
    li_c, b_c, m_prev, dmat, inter, m_t = [], [], [], [], [], []
    for (bi, hd), rw in zip(chains, rows):
        gates_c = gates_c_all[rw, :]
        gates_r = gates_r_all[:, rw]
        li_r = gates_r[hd:hd + 1, :]
        lf_c = jax.nn.log_sigmoid(gates_c[:, nheads + hd:nheads + hd + 1])
        lf_r = jax.nn.log_sigmoid(gates_r[nheads + hd:nheads + hd + 1, :])
        bc = jnp.sum(jnp.where(causal, lf_r, 0.0), axis=1, keepdims=True)
        br = jnp.sum(jnp.where(row_id <= col_id, lf_c, 0.0), axis=0, keepdims=True)
        mp = m_ref[bi, hd][:, :1]
        dm = jnp.where(causal, bc - br + li_r, -jnp.inf)
        it = bc + mp
        li_c.append(gates_c[:, hd:hd + 1])
        b_c.append(bc)
        m_prev.append(mp)
        dmat.append(dm)
        inter.append(it)
        m_t.append(jnp.maximum(jnp.max(dm, axis=1, keepdims=True), it))

    q = [q_ref[rw, hd * dk:(hd + 1) * dk] for (_, hd), rw in zip(chains, rows)]
    k = [k_ref[rw, hd * dk:(hd + 1) * dk] * k_scale for (_, hd), rw in zip(chains, rows)]
    v = [v_ref[rw, hd * dv:(hd + 1) * dv] for (_, hd), rw in zip(chains, rows)]
    qb = [z.astype(BF16) for z in q]
    kb = [z.astype(BF16) for z in k]
    c_old = [c_ref[bi, hd] for bi, hd in chains]
    n_old = [n_ref[bi, hd] for bi, hd in chains]
    qk = [lax.dot_general(qb[i], kb[i], nt, preferred_element_type=F32) for i in idx]
    cq = [lax.dot_general(qb[i], c_old[i].astype(BF16), nt, preferred_element_type=F32) for i in idx]
    s = [qk[i] * jnp.exp(dmat[i] - m_t[i]) for i in idx]
    sv = [jnp.dot(s[i].astype(BF16), v[i].astype(BF16), preferred_element_type=F32) for i in idx]
    m_new, ws = [], []
    for i in idx:
        b_last = b_c[i][L - 1:L, :]
        ws_log = b_last - b_c[i] + li_c[i]
        mn = jnp.maximum(b_last + m_prev[i], jnp.max(ws_log, axis=0, keepdims=True))
        m_new.append(mn)
        ws.append(jnp.exp(ws_log - mn))
    upd = [lax.dot_general((v[i] * ws[i]).astype(BF16), kb[i], tn, preferred_element_type=F32) for i in idx]
    h_out = [[None] * bb for _ in range(nheads)]
    for i, ((bi, hd), rw) in enumerate(zip(chains, rows)):
        w_inter = jnp.exp(inter[i] - m_t[i])
        num = sv[i] + w_inter * cq[i]
        nq = jnp.sum(q[i] * n_old[i], axis=1, keepdims=True)
        den = jnp.sum(s[i], axis=1, keepdims=True) + w_inter * nq
        h = num / jnp.maximum(jnp.abs(den), jnp.exp(-m_t[i]))
        hn = h * lax.rsqrt(jnp.mean(h * h, axis=1, keepdims=True) + ML_NORM_EPS)
        h_out[hd][bi] = hn * ng_ref[:, hd * dv:(hd + 1) * dv]
        dec = jnp.exp(b_c[i][L - 1:L, :] + m_prev[i] - m_new[i])
        c_ref[bi, hd] = dec * c_old[i] + upd[i]
        n_ref[bi, hd] = dec * n_old[i] + jnp.sum(ws[i] * k[i], axis=0, keepdims=True)
        m_ref[bi, hd] = jnp.broadcast_to(m_new[i], m_ref.shape[2:])
    for hd in range(nheads):
        cols = slice(hd * dv, (hd + 1) * dv)
        h_all = h_out[hd][0] if bb == 1 else jnp.concatenate(h_out[hd], axis=0)
        h_ref[:, cols] = (jax.nn.sigmoid(og_ref[:, cols]) * h_all).astype(h_ref.dtype)

    @pl.when(ci == pl.num_programs(1) - 1)
    def _():
        cT_ref[...] = c_ref[...]
        nT_ref[...] = n_ref[...]
        mT_ref[...] = m_ref[...]


def _mlstm_group(qkvo, gates, b_gates, norm_g, c0, n0, m0, group, h_prev_out):
    r0, bsz, t = group
    m_rows = qkvo.shape[0]
    nheads = ML_HEADS
    dk = c0.shape[-1]
    dv = c0.shape[-2]
    lanes = V7X_LANES
    chunk = min(t, lanes)
    bb = 1 if chunk >= V7X_SUBLANES else 2 * V7X_SUBLANES // chunk
    rows = bb * chunk
    assert t % chunk == 0 and bsz % bb == 0 and r0 % rows == 0
    nc = t // chunk
    nsteps = (bsz // bb) * nc
    rb0 = r0 // rows
    g_grp = gates[r0:r0 + bsz * t]
    g3t = g_grp.reshape(nsteps, rows, 2 * nheads).transpose(0, 2, 1)
    n0r = n0.astype(F32).reshape(bsz, nheads, 1, dk)
    m0r = jnp.broadcast_to(m0.astype(F32).reshape(bsz, nheads, 1, 1), (bsz, nheads, 1, lanes))
    wq, wv = nheads * dk, nheads * dv
    rowblk = lambda b, c: rb0 + b * nc + c
    seq = lambda w, cb: pl.BlockSpec((rows, w), functools.partial(lambda b, c, cb_: (rowblk(b, c), cb_), cb_=cb))
    st_c = pl.BlockSpec((bb, nheads, dv, dk), lambda b, c: (b, 0, 0, 0))
    st_n = pl.BlockSpec((bb, nheads, 1, dk), lambda b, c: (b, 0, 0, 0))
    st_m = pl.BlockSpec((bb, nheads, 1, lanes), lambda b, c: (b, 0, 0, 0))
    est = 2 * 3 * _nbytes((rows, 2 * wq + 2 * wv), F32) + 5 * bb * _nbytes((nheads, dv, dk), F32) + (8 << 20)
    aliased = h_prev_out is not None
    kern = functools.partial(_mlstm_kernel, nheads=nheads, dk=dk, dv=dv, chunk=chunk, bb=bb, aliased=aliased)
    in_specs = [seq(wq, 0), seq(wq, 1), seq(wv, wq * 2 // wv), seq(wv, wq * 2 // wv + 1),
                pl.BlockSpec((rows, 2 * nheads), lambda b, c: (rowblk(b, c), 0)),
                pl.BlockSpec((None, 2 * nheads, rows), lambda b, c: (b * nc + c, 0, 0)),
                pl.BlockSpec((1, 2 * nheads), lambda b, c: (0, 0)),
                pl.BlockSpec((2 * nheads, 1), lambda b, c: (0, 0)),
                pl.BlockSpec((1, wv), lambda b, c: (0, 0)),
                st_c, st_n, st_m]
    args = [qkvo, qkvo, qkvo, qkvo, gates, g3t, _row(b_gates), b_gates.reshape(-1, 1), _row(norm_g),
            c0.astype(F32), n0r, m0r]
    io_alias = {}
    if aliased:
        in_specs.append(pl.BlockSpec(memory_space=pl.ANY))
        args.append(h_prev_out)
        io_alias = {len(args) - 1: 0}
    h, c_fin, n_fin, m_fin = pl.pallas_call(
        kern,
        grid=(bsz // bb, nc),
        in_specs=in_specs,
        out_specs=[pl.BlockSpec((rows, wv), lambda b, c: (rowblk(b, c), 0)), st_c, st_n, st_m],
        out_shape=[jax.ShapeDtypeStruct((m_rows, wv), BF16),
                   jax.ShapeDtypeStruct((bsz, nheads, dv, dk), F32),
                   jax.ShapeDtypeStruct((bsz, nheads, 1, dk), F32),
                   jax.ShapeDtypeStruct((bsz, nheads, 1, lanes), F32)],
        scratch_shapes=[pltpu.VMEM((bb, nheads, dv, dk), F32), pltpu.VMEM((bb, nheads, 1, dk), F32),
                        pltpu.VMEM((bb, nheads, 1, lanes), F32)],
        input_output_aliases=io_alias,
        compiler_params=_params(("parallel", "arbitrary"), est),
        name="mlstm_chunk",
    )(*args)
    return h, c_fin, n_fin.reshape(bsz, nheads, dk), m_fin[:, :, 0, 0]


def _mlstm_layer(x, groups, states, p, wb):
    d = x.shape[1]
    nheads = ML_HEADS
    dv = d // nheads
    dk = dv // 2
    n_main = 2 * nheads * dk + 2 * nheads * dv
    xn = _rmsnorm(x, p['norm_g'], BF16)
    qkvo = _mm([_full(xn)], [(wb['ml_w_main'], (), 0)], n=n_main, out_dtype=F32, name="ml_in")
    gates = _mm([_full(xn)], [(wb['ml_w_gates'], (), 0)], n=2 * nheads, out_dtype=F32, name="ml_gates")
    h, new_states = None, []
    for grp, st in zip(groups, states):
        h, c_fin, n_fin, m_fin = _mlstm_group(qkvo, gates, p['ml_b_gates'], p['ml_norm_g'], st[0], st[1], st[2], grp, h)
        new_states.append((c_fin, n_fin, m_fin))
    x = _mm([_full(h)], [(wb['ml_w_o'], (), 0)], n=d, out_dtype=F32,
            e_blocks=[x], epilogue=lambda accs, e, ev: e[0] + accs[0], name="ml_out")
    return x, new_states


def _dwconv_kernel(u_ref, buf_ref, w_ref, *rest, ksize, t, bb, rows, halo, aliased):
    if aliased:
        rest = rest[1:]
    y_ref, ext_ref = rest
    w = w_ref[...]
    sub = V7X_SUBLANES
    lead = halo - (ksize - 1)
    n_tiles = t // rows
    taps_by_shift = [[j for j in range(ksize) if (j + lead) % sub == r] for r in range(sub)]
    ext_ref[pl.ds(halo + t, sub), :] = jnp.zeros((sub, ext_ref.shape[1]), F32)

    for b in range(bb):
        ext_ref[pl.ds(lead, ksize - 1), :] = buf_ref[b]
        ext_ref[pl.ds(halo, t), :] = u_ref[pl.ds(b * t, t), :]

        def tile(ti, carry):
            base = ti * rows if isinstance(ti, int) else pl.multiple_of(ti * rows, sub)
            acc = None
            for r in range(sub):
                part = None
                for j in taps_by_shift[r]:
                    q8 = (j + lead) - r
                    term = w[j:j + 1, :] * ext_ref[pl.ds(base + q8, rows + sub), :]
                    part = term if part is None else part + term
                if part is not None:
                    shifted = part[r:r + rows, :]
                    acc = shifted if acc is None else acc + shifted
            y_ref[pl.ds(b * t + base, rows), :] = acc
            return carry

        if n_tiles == 1:
            tile(0, 0)
        else:
            lax.fori_loop(0, n_tiles, tile, 0)


def _dwconv_group(u, buf, w, group, y_prev_out):
    r0, bsz, t = group
    m_rows, c = u.shape
    ksize = w.shape[0]
    sub = V7X_SUBLANES
    halo = _round_up(ksize - 1, sub)
    bc = 256
    bb = 1 if t >= 64 else 8
    rows = min(t, 64)
    blk = bb * t
    assert bsz % bb == 0 and t % rows == 0 and c % bc == 0 and r0 % blk == 0 and blk % sub == 0
    rb0 = r0 // blk
    aliased = y_prev_out is not None
    est = 4 * _nbytes((blk, bc), F32) + _nbytes((halo + t + sub, bc), F32) + (4 << 20)
    kern = functools.partial(_dwconv_kernel, ksize=ksize, t=t, bb=bb, rows=rows, halo=halo, aliased=aliased)
    in_specs = [pl.BlockSpec((blk, bc), lambda b, ci: (rb0 + b, ci)),
                pl.BlockSpec((bb, ksize - 1, bc), lambda b, ci: (b, 0, ci)),
                pl.BlockSpec((ksize, bc), lambda b, ci: (0, ci))]
    args = [u, buf.astype(F32), w]
    io_alias = {}
    if aliased:
        in_specs.append(pl.BlockSpec(memory_space=pl.ANY))
        args.append(y_prev_out)
        io_alias = {3: 0}
    return pl.pallas_call(
        kern,
        grid=(bsz // bb, c // bc),
        in_specs=in_specs,
        out_specs=pl.BlockSpec((blk, bc), lambda b, ci: (rb0 + b, ci)),
        out_shape=jax.ShapeDtypeStruct((m_rows, c), F32),
        scratch_shapes=[pltpu.VMEM((halo + t + sub, bc), F32)],
        input_output_aliases=io_alias,
        compiler_params=_params(("parallel", "parallel"), est),
        name="dwconv",
    )(*args)


def _conv_groups(u, w_dw, groups, bufs):
    y, new_bufs = None, []
    c = u.shape[1]
    km1 = w_dw.shape[0] - 1
    for (r0, b, t), buf in zip(groups, bufs):
        y = _dwconv_group(u, buf, w_dw, (r0, b, t), y)
        if t >= km1:
            tail = r0 + t - km1 + jnp.arange(b)[:, None] * t + jnp.arange(km1)[None, :]
            new_bufs.append(u[tail])
        else:
            new_bufs.append(jnp.concatenate([buf.astype(F32)[:, t:], u[r0:r0 + b * t].reshape(b, t, c)], axis=1))
    return y, new_bufs


def _conformer_layer(x, groups, states, p, wb):
    d = x.shape[1]
    b1 = _row(p['cf_b_pw1'])
    w1 = wb['cf_w_pw1']
    u = _mm([_full(x)], [(w1, (), 0), (w1, (), d)], n=d, out_dtype=F32, avecs=[_row(p['norm_g'])],
            prologue=lambda a, v, pw: _rms(a[0]) * v[0], evecs=[b1[:, :d], b1[:, d:]],
            epilogue=lambda accs, e, ev: (accs[0] + ev[0]) * jax.nn.sigmoid(accs[1] + ev[1]), name="cf_pw1")
    y, new_bufs = _conv_groups(u, p['cf_w_dw'], groups, [st[0] for st in states])

    def ln_silu(a, v, pw):
        z = a[0] + v[0]
        mu = jnp.mean(z, axis=-1, keepdims=True)
        var = jnp.mean(jnp.square(z - mu), axis=-1, keepdims=True)
        z = (z - mu) * lax.rsqrt(var + CF_LN_EPS) * v[1] + v[2]
        return z * jax.nn.sigmoid(z)

    x = _mm([_full(y)], [(wb['cf_w_pw2'], (), 0)], n=d, out_dtype=F32, prologue=ln_silu,
            avecs=[_row(p['cf_b_dw']), _row(p['cf_ln_g']), _row(p['cf_ln_b'])],
            e_blocks=[x], evecs=[_row(p['cf_b_pw2'])],
            epilogue=lambda accs, e, ev: e[0] + (accs[0] + ev[0]), name="cf_pw2")
    return x, [(nb,) for nb in new_bufs]


def _sconv_layer(x, groups, states, p, wb):
    d = x.shape[1]
    xn = _rmsnorm(x, p['norm_g'], BF16)
    w_in = wb['sc_w_in']
    b_gate = _mm([_full(xn)], [(w_in, (), 0)], n=d, out_dtype=F32, name="sc_b")
    ch = _mm([_full(xn)], [(w_in, (), d), (w_in, (), 2 * d)], n=d, out_dtype=F32,
             epilogue=lambda accs, e, ev: accs[0] * accs[1], name="sc_ch")
    y, new_bufs = _conv_groups(ch, p['sc_w_dw'], groups, [st[0] for st in states])
    x = _mm([_full(b_gate), _full(y)], [(wb['sc_w_out'], (), 0)], n=d, out_dtype=F32,
            prologue=lambda a, v, pw: a[0] * a[1], e_blocks=[x],
            epilogue=lambda accs, e, ev: e[0] + accs[0], name="sc_out")
    return x, [(nb,) for nb in new_bufs]


def kernel(x_prompt, x_sample, state_rwkv_wkv, state_rwkv_shift, state_mlstm_C, state_mlstm_n, state_mlstm_m,
           state_conf_conv, state_sconv_conv, norm_g, ffn_w_gu, ffn_w_down, final_norm_g,
           rw_mu, rw_w_rkv, rw_w0, rw_w1, rw_w2, rw_a0, rw_a1, rw_a2, rw_g1, rw_g2, rw_k_k, rw_k_a, rw_r_k,
           rw_ln_g, rw_ln_b, rw_w_o, ml_w_in, ml_b_gates, ml_norm_g, ml_w_o,
           cf_w_pw1, cf_b_pw1, cf_w_dw, cf_b_dw, cf_ln_g, cf_ln_b, cf_w_pw2, cf_b_pw2,
           sc_w_in, sc_w_dw, sc_w_out):
    bp, tp, d = x_prompt.shape
    bs, ts, _ = x_sample.shape
    depth = norm_g.shape[0]
    n_mixers = 4
    groups = [(0, bp, tp), (bp * tp, bs, ts)]
    x = jnp.concatenate([x_prompt.reshape(bp * tp, d), x_sample.reshape(bs * ts, d)], axis=0).astype(F32)

    dv = d // ML_HEADS
    dk = dv // 2
    cf_k = cf_w_dw.shape[1]
    sc_k = sc_w_dw.shape[1]
    zeros = lambda *s: jnp.zeros(s, F32)

    out_states = {name: ([], []) for name in ('wkv', 'shift', 'C', 'n', 'm', 'cf', 'sc')}
    for i in range(depth):
        j = i // n_mixers
        kind = i % n_mixers
        x = _ffn(x, norm_g[i, 0], ffn_w_gu, ffn_w_down, (i, 0))
        if kind == 0:
            p = dict(norm_g=norm_g[i, 1], rw_mu=rw_mu[j], rw_w0=rw_w0[j], rw_a0=rw_a0[j], rw_k_k=rw_k_k[j],
                     rw_k_a=rw_k_a[j], rw_r_k=rw_r_k[j], rw_ln_g=rw_ln_g[j], rw_ln_b=rw_ln_b[j])
            wb = dict(rw_w_rkv=rw_w_rkv[j].astype(BF16), rw_w1=rw_w1[j].astype(BF16), rw_w2=rw_w2[j].astype(BF16),
                      rw_a1=rw_a1[j].astype(BF16), rw_a2=rw_a2[j].astype(BF16), rw_g1=rw_g1[j].astype(BF16),
                      rw_g2=rw_g2[j].astype(BF16), rw_w_o=rw_w_o[j].astype(BF16))
            states = [(None, zeros(bp, d)), (state_rwkv_wkv[j], state_rwkv_shift[j].astype(F32))]
            x, new = _rwkv_layer(x, groups, states, p, wb)
            for gi in range(2):
                out_states['wkv'][gi].append(new[gi][0])
                out_states['shift'][gi].append(new[gi][1])
        elif kind == 1:
            p = dict(norm_g=norm_g[i, 1], ml_b_gates=ml_b_gates[j], ml_norm_g=ml_norm_g[j])
            n_main = 2 * ML_HEADS * dk + 2 * ML_HEADS * dv
            wb = dict(ml_w_main=ml_w_in[j][:, :n_main].astype(BF16), ml_w_gates=ml_w_in[j][:, n_main:].astype(BF16),
                      ml_w_o=ml_w_o[j].astype(BF16))
            states = [(zeros(bp, ML_HEADS, dv, dk), zeros(bp, ML_HEADS, dk), zeros(bp, ML_HEADS)),
                      (state_mlstm_C[j], state_mlstm_n[j], state_mlstm_m[j])]
            x, new = _mlstm_layer(x, groups, states, p, wb)
            for gi in range(2):
                out_states['C'][gi].append(new[gi][0])
                out_states['n'][gi].append(new[gi][1])
                out_states['m'][gi].append(new[gi][2])
        elif kind == 2:
            p = dict(norm_g=norm_g[i, 1], cf_b_pw1=cf_b_pw1[j], cf_w_dw=cf_w_dw[j], cf_b_dw=cf_b_dw[j],
                     cf_ln_g=cf_ln_g[j], cf_ln_b=cf_ln_b[j], cf_b_pw2=cf_b_pw2[j])
            wb = dict(cf_w_pw1=cf_w_pw1[j].astype(BF16), cf_w_pw2=cf_w_pw2[j].astype(BF16))
            states = [(zeros(bp, cf_k - 1, d),), (state_conf_conv[j],)]
            x, new = _conformer_layer(x, groups, states, p, wb)
            for gi in range(2):
                out_states['cf'][gi].append(new[gi][0])
        else:
            p = dict(norm_g=norm_g[i, 1], sc_w_dw=sc_w_dw[j])
            wb = dict(sc_w_in=sc_w_in[j].astype(BF16), sc_w_out=sc_w_out[j].astype(BF16))
            states = [(zeros(bp, sc_k - 1, d),), (state_sconv_conv[j],)]
            x, new = _sconv_layer(x, groups, states, p, wb)
            for gi in range(2):
                out_states['sc'][gi].append(new[gi][0])
        x = _ffn(x, norm_g[i, 2], ffn_w_gu, ffn_w_down, (i, 1))

    y_prompt = _rmsnorm(x, final_norm_g, x_prompt.dtype, 0, bp * tp).reshape(bp, tp, d)
    y_sample = _rmsnorm(x, final_norm_g, x_sample.dtype, bp * tp, bs * ts).reshape(bs, ts, d)
    order = ('wkv', 'shift', 'C', 'n', 'm', 'cf', 'sc')
    dt = x_prompt.dtype
    prompt_out = tuple(jnp.stack(out_states[name][0]).astype(dt) for name in order)
    sample_out = tuple(jnp.stack(out_states[name][1]).astype(st.dtype) for name, st in
                       zip(order, (state_rwkv_wkv, state_rwkv_shift, state_mlstm_C, state_mlstm_n, state_mlstm_m,
                                   state_conf_conv, state_sconv_conv)))
    return (y_prompt, y_sample) + prompt_out + sample_out
```

```python
import functools
import math

import jax
import jax.numpy as jnp
from jax import lax
from jax.experimental import pallas as pl
from jax.experimental.pallas import tpu as pltpu

F32 = jnp.float32
BF16 = jnp.bfloat16

NORM_EPS = 1e-6
RW_HEAD = 64
RW_GN_EPS = 64e-5
ML_HEADS = 8
ML_GATE_CAP = 15.0
ML_NORM_EPS = 1e-6
CF_LN_EPS = 1e-5

V7X_LANES = 128
V7X_SUBLANES = 8
V7X_VMEM_BYTES = 64 * 1024 * 1024
VMEM_LIMIT_CAP = V7X_VMEM_BYTES - 6 * 1024 * 1024


def _nbytes(shape, dtype):
    return math.prod(shape) * jnp.dtype(dtype).itemsize


def _params(sem, vmem_est):
    limit = int(min(max(vmem_est * 1.2 + (2 << 20), 16 << 20), VMEM_LIMIT_CAP))
    return pltpu.CompilerParams(dimension_semantics=sem, vmem_limit_bytes=limit)


def _pick_bm(m, target, also_divides=()):
    best = None
    for bm in range(16, min(m, target) + 1, 16):
        if m % bm == 0 and all(o % bm == 0 for o in also_divides):
            best = bm
    assert best is not None
    return best


def _rms(x):
    return x * lax.rsqrt(jnp.mean(x * x, axis=-1, keepdims=True) + NORM_EPS)


def _round_up(x, m):
    return -(-x // m) * m


def _mm_kernel(*refs, counts, prologue, epilogue, use_scratch, has_side):
    n_a, n_av, n_pw, n_w, n_e, n_ev = counts
    pos = 0
    groups = []
    for c in counts:
        groups.append(refs[pos:pos + c])
        pos += c
    a_refs, av_refs, pw_refs, w_refs, e_refs, ev_refs = groups
    if has_side:
        side_in, o_ref, side_out = refs[pos:pos + 3]
        side_out[...] = side_in[...].astype(side_out.dtype)
    else:
        o_ref = refs[pos]
    if use_scratch:
        lhs_ref = refs[-1]

        @pl.when(pl.program_id(1) == 0)
        def _():
            lhs_ref[...] = prologue([r[...] for r in a_refs], [r[...] for r in av_refs],
                                    [r[...] for r in pw_refs]).astype(BF16)

        lhs = lhs_ref[...]
    else:
        lhs = a_refs[0][...]
    accs = [jnp.dot(lhs, w[...].astype(BF16), preferred_element_type=F32) for w in w_refs]
    out = epilogue(accs, [r[...] for r in e_refs], [r[...] for r in ev_refs])
    o_ref[...] = out.astype(o_ref.dtype)


def _mm(a_list, w_list, *, n, out_dtype, prologue=None, epilogue=None, avecs=(), pw_list=(), e_blocks=(), evecs=(),
        side_cast=None, bm_target=1088, bn=512, name="mm"):
    m = a_list[0][0].shape[0]
    k = w_list[0][0].shape[-2]
    bm = _pick_bm(m, bm_target)
    bn = min(bn, n)
    assert n % bn == 0
    use_scratch = prologue is not None or a_list[0][0].dtype != BF16
    if prologue is None:
        prologue = lambda a, v, pw: a[0]
    if epilogue is None:
        epilogue = lambda accs, e, ev: accs[0]

    in_specs, args, est = [], [], 0
    for arr, ka, cb in a_list:
        in_specs.append(pl.BlockSpec((bm, ka), functools.partial(lambda i, j, c: (i, c), c=cb)))
        args.append(arr)
        est += 2 * _nbytes((bm, ka), arr.dtype)
    for v in avecs:
        in_specs.append(pl.BlockSpec(v.shape, lambda i, j: (0, 0)))
        args.append(v)
    for pw in pw_list:
        in_specs.append(pl.BlockSpec(pw.shape, lambda i, j: (0, 0)))
        args.append(pw)
        est += 2 * _nbytes(pw.shape, pw.dtype)
    for arr, lead, off in w_list:
        assert off % bn == 0 and arr.shape[-2] == k
        in_specs.append(pl.BlockSpec((None,) * len(lead) + (k, bn),
                                     functools.partial(lambda i, j, l, o: l + (0, j + o), l=tuple(lead), o=off // bn)))
        args.append(arr)
        est += 2 * _nbytes((k, bn), arr.dtype) + (_nbytes((k, bn), BF16) if arr.dtype != BF16 else 0)
    for e in e_blocks:
        in_specs.append(pl.BlockSpec((bm, bn), lambda i, j: (i, j)))
        args.append(e)
        est += 2 * _nbytes((bm, bn), e.dtype)
    for v in evecs:
        in_specs.append(pl.BlockSpec((1, bn), lambda i, j: (0, j)))
        args.append(v)
    est += 2 * _nbytes((bm, bn), out_dtype) + (len(w_list) + 2) * _nbytes((bm, bn), F32)
    scratch = []
    if use_scratch:
        scratch.append(pltpu.VMEM((bm, k), BF16))
        est += _nbytes((bm, k), BF16) + 2 * _nbytes((bm, a_list[0][1]), F32)

    grid = (m // bm, n // bn)
    out_specs = pl.BlockSpec((bm, bn), lambda i, j: (i, j))
    out_shape = jax.ShapeDtypeStruct((m, n), out_dtype)
    if side_cast is not None:
        s_arr, s_lead = side_cast
        s_rows, s_cols = s_arr.shape[-2:]
        steps = grid[0] * grid[1]
        assert s_rows % (steps * 2 * V7X_SUBLANES) == 0
        rps = s_rows // steps
        nj = grid[1]
        in_specs.append(pl.BlockSpec((None,) * len(s_lead) + (rps, s_cols),
                                     functools.partial(lambda i, j, l: l + (i * nj + j, 0), l=tuple(s_lead))))
        args.append(s_arr)
        out_specs = [out_specs, pl.BlockSpec((rps, s_cols), lambda i, j: (i * nj + j, 0))]
        out_shape = [out_shape, jax.ShapeDtypeStruct((s_rows, s_cols), BF16)]
        est += 2 * _nbytes((rps, s_cols), F32) + 2 * _nbytes((rps, s_cols), BF16)

    counts = (len(a_list), len(avecs), len(pw_list), len(w_list), len(e_blocks), len(evecs))
    kern = functools.partial(_mm_kernel, counts=counts, prologue=prologue, epilogue=epilogue, use_scratch=use_scratch,
                             has_side=side_cast is not None)
    return pl.pallas_call(
        kern,
        grid=grid,
        in_specs=in_specs,
        out_specs=out_specs,
        out_shape=out_shape,
        scratch_shapes=scratch,
        compiler_params=_params(("parallel", "arbitrary"), est),
        name=name,
    )(*args)


def _full(arr):
    return (arr, arr.shape[1], 0)


def _row(v):
    return v.reshape(1, -1)


def _rmsnorm_kernel(x_ref, g_ref, o_ref):
    o_ref[...] = (_rms(x_ref[...]) * g_ref[...]).astype(o_ref.dtype)


def _rmsnorm(x, g, out_dtype, row0=0, nrows=None):
    d = x.shape[1]
    m = x.shape[0] if nrows is None else nrows
    bm = _pick_bm(m, 1088, also_divides=(row0,) if row0 else ())
    rb0 = row0 // bm
    est = 2 * _nbytes((bm, d), F32) * 3
    return pl.pallas_call(
        _rmsnorm_kernel,
        grid=(m // bm,),
        in_specs=[pl.BlockSpec((bm, d), lambda i: (rb0 + i, 0)), pl.BlockSpec((1, d), lambda i: (0, 0))],
        out_specs=pl.BlockSpec((bm, d), lambda i: (i, 0)),
        out_shape=jax.ShapeDtypeStruct((m, d), out_dtype),
        compiler_params=_params(("parallel",), est),
        name="rmsnorm",
    )(x, _row(g))


def _ffn(x, g, w_gu, w_down, lead):
    d_ff = w_down.shape[-2]
    h, w_down_b = _mm([_full(x)], [(w_gu, lead, 0), (w_gu, lead, d_ff)], n=d_ff, out_dtype=BF16, avecs=[_row(g)],
                      prologue=lambda a, v, pw: _rms(a[0]) * v[0],
                      epilogue=lambda accs, e, ev: accs[0] * jax.nn.sigmoid(accs[0]) * accs[1],
                      side_cast=(w_down, lead), name="ffn_up")
    return _mm([_full(h)], [(w_down_b, (), 0)], n=x.shape[1], out_dtype=F32, e_blocks=[x],
               epilogue=lambda accs, e, ev: e[0] + 0.5 * accs[0], name="ffn_down")


def _rw_mix_kernel(x_ref, g_ref, mu_ref, ov_ref, h_ref, *rest, n_prompt_blocks, blocks_per_seq, ts):
    outs, carry_ref = rest[:-1], rest[-1]
    pid = pl.program_id(0)

    @pl.when(pid == 0)
    def _():
        carry_ref[...] = jnp.zeros_like(carry_ref)

    h = _rms(x_ref[...]) * g_ref[...]
    h_ref[...] = h
    bm = h.shape[0]
    rolled = pltpu.roll(h, 1, axis=0)
    row = lax.broadcasted_iota(jnp.int32, h.shape, 0)
    carry = jnp.where(pid % blocks_per_seq == 0, 0.0, carry_ref[...])
    prev_prompt = jnp.where(row == 0, carry, rolled)
    prev_sample = jnp.where(row % ts == 0, ov_ref[...], rolled)
    prev = jnp.where(pid >= n_prompt_blocks, prev_sample, prev_prompt)
    carry_ref[...] = h[bm - 1:bm, :]
    delta = prev - h
    for i, o in enumerate(outs):
        o[...] = (h + delta * mu_ref[i:i + 1, :]).astype(BF16)


def _rw_mix(x, g, mu, shift_sample, groups):
    m, d = x.shape
    (_, bp, tp), (r0s, bs, ts) = groups
    bm = _pick_bm(tp, 512, also_divides=(bs * ts,))
    assert bm % ts == 0 and r0s == bp * tp
    n_prompt_blocks = (bp * tp) // bm
    ov = jnp.zeros((bs, ts, d), F32).at[:, 0].set(shift_sample.astype(F32)).reshape(bs * ts, d)
    n_mix = mu.shape[0]
    blk = pl.BlockSpec((bm, d), lambda i: (i, 0))
    est = 2 * _nbytes((bm, d), F32) * 3 + 2 * n_mix * _nbytes((bm, d), BF16) + 6 * _nbytes((bm, d), F32)
    kern = functools.partial(_rw_mix_kernel, n_prompt_blocks=n_prompt_blocks, blocks_per_seq=tp // bm, ts=ts)
    outs = pl.pallas_call(
        kern,
        grid=(m // bm,),
        in_specs=[blk, pl.BlockSpec((1, d), lambda i: (0, 0)), pl.BlockSpec((n_mix, d), lambda i: (0, 0)),
                  pl.BlockSpec((bm, d), lambda i: (jnp.maximum(i - n_prompt_blocks, 0), 0))],
        out_specs=[blk] * (1 + n_mix),
        out_shape=[jax.ShapeDtypeStruct((m, d), F32)] + [jax.ShapeDtypeStruct((m, d), BF16)] * n_mix,
        scratch_shapes=[pltpu.VMEM((1, d), F32)],
        compiler_params=_params(("arbitrary",), est),
        name="rw_mix",
    )(x, _row(g), mu, ov)
    return outs[0], outs[1:]


def _rwkv_steps(r_ref, k_ref, v_ref, w_ref, a_ref, y_ref, param_refs, s_ref, tb, nh):
    kk_p, ka_p, rk_p, ln_g, ln_b = (q[...] for q in param_refs)
    vgrp = V7X_SUBLANES

    def step(t, carry):
        r = r_ref[t]
        k = k_ref[t]
        v = v_ref[t]
        logw = -jax.nn.softplus(-w_ref[t]) - 0.5
        d = jnp.exp(-jnp.exp(logw))
        a = jax.nn.sigmoid(a_ref[t])
        kk = k * kk_p
        kk = kk / jnp.maximum(jnp.sqrt(jnp.sum(kk * kk, axis=0, keepdims=True)), 1e-12)
        kmod = k * (1.0 + (a - 1.0) * ka_p)
        b = kk * a

        def vgroup(g, c2):
            base = pl.multiple_of(g * vgrp, vgrp)
            v8 = v_ref[t, pl.ds(base, vgrp), :]
            ys = []
            for j in range(vgrp):
                s_old = s_ref[base + j]
                sa = jnp.sum(s_old * kk, axis=0, keepdims=True)
                s_new = s_old * d - sa * b + v8[j:j + 1, :] * kmod
                s_ref[base + j] = s_new
                ys.append(jnp.sum(s_new * r, axis=0, keepdims=True))
            y_ref[t, pl.ds(base, vgrp), :] = jnp.concatenate(ys, axis=0)
            return c2

        lax.fori_loop(0, nh // vgrp, vgroup, 0)

        y = y_ref[t]
        mean = jnp.mean(y, axis=0, keepdims=True)
        var = jnp.mean(jnp.square(y - mean), axis=0, keepdims=True)
        yn = (y - mean) * lax.rsqrt(var + RW_GN_EPS) * ln_g + ln_b
        bonus = jnp.sum(r * kmod * rk_p, axis=0, keepdims=True) * v
        y_ref[t] = yn + bonus
        return carry

    lax.fori_loop(0, tb, step, 0)


def _rwkv_scan_kernel(r_ref, k_ref, v_ref, w_ref, a_ref, kk_p_ref, ka_p_ref, rk_p_ref, lg_ref, lb_ref, s0_ref,
                      y_ref, sT_ref, s_ref, *, tb, nh):
    tblk = pl.program_id(1)

    @pl.when(tblk == 0)
    def _():
        s_ref[...] = s0_ref[...]

    _rwkv_steps(r_ref, k_ref, v_ref, w_ref, a_ref, y_ref, (kk_p_ref, ka_p_ref, rk_p_ref, lg_ref, lb_ref), s_ref, tb, nh)

    @pl.when(tblk == pl.num_programs(1) - 1)
    def _():
        sT_ref[...] = s_ref[...]


def _rwkv_scan(r, k, v, w_pre, a_pre, kk_p, ka_p, rk_p, ln_g, ln_b, s0, *, tb):
    t, nh, c = r.shape
    lanes = V7X_LANES
    assert c % lanes == 0 and t % tb == 0
    seq = pl.BlockSpec((tb, nh, lanes), lambda ci, ti: (ti, 0, ci))
    par = pl.BlockSpec((nh, lanes), lambda ci, ti: (0, ci))
    st = pl.BlockSpec((nh, nh, lanes), lambda ci, ti: (0, 0, ci))
    est = 6 * 2 * _nbytes((tb, nh, lanes), F32) + 5 * _nbytes((nh, nh, lanes), F32)
    kern = functools.partial(_rwkv_scan_kernel, tb=tb, nh=nh)
    return pl.pallas_call(
        kern,
        grid=(c // lanes, t // tb),
        in_specs=[seq] * 5 + [par] * 5 + [st],
        out_specs=[seq, st],
        out_shape=[jax.ShapeDtypeStruct((t, nh, c), F32), jax.ShapeDtypeStruct((nh, nh, c), F32)],
        scratch_shapes=[pltpu.VMEM((nh, nh, lanes), F32)],
        compiler_params=_params(("parallel", "arbitrary"), est),
        name="rwkv_scan",
    )(r, k, v, w_pre, a_pre, kk_p, ka_p, rk_p, ln_g, ln_b, s0)


def _to_chains(x, bsz, t, nheads):
    return x.reshape(bsz, t, nheads, RW_HEAD).transpose(1, 3, 0, 2).reshape(t, RW_HEAD, bsz * nheads)


def _from_chains(y, bsz, t, nheads):
    return y.reshape(t, RW_HEAD, bsz, nheads).transpose(2, 0, 3, 1).reshape(bsz * t, nheads * RW_HEAD)


def _rwkv_group(r, k, v, w_pre, a_pre, wkv0, p, bsz, t):
    d_model = r.shape[1]
    nheads = d_model // RW_HEAD
    c = bsz * nheads
    chain_param = lambda q: jnp.tile(q.reshape(nheads, RW_HEAD).T, (1, bsz))
    if wkv0 is None:
        s0 = jnp.zeros((RW_HEAD, RW_HEAD, c), F32)
    else:
        s0 = wkv0.astype(F32).transpose(2, 3, 0, 1).reshape(RW_HEAD, RW_HEAD, c)
    tb = t if t <= 32 else 32
    y, s_fin = _rwkv_scan(*[_to_chains(q, bsz, t, nheads) for q in (r, k, v, w_pre, a_pre)],
                          *[chain_param(p[name]) for name in ('rw_k_k', 'rw_k_a', 'rw_r_k', 'rw_ln_g', 'rw_ln_b')],
                          s0, tb=tb)
    s_fin = s_fin.reshape(RW_HEAD, RW_HEAD, bsz, nheads).transpose(2, 3, 0, 1)
    return _from_chains(y, bsz, t, nheads), s_fin


def _rwkv_chunk_kernel(r_ref, k_ref, v_ref, low_ref, loa_ref, log_ref, w2_ref, a2_ref, g2_ref, w0_ref, a0_ref,
                       kkp_ref, kap_ref, rkp_ref, lg_ref, lb_ref,
                       y_ref, sT_ref, s2_ref, w_ref, a_ref, g_ref, *, chunk, npairs, unroll):
    L = chunk
    lanes = V7X_LANES
    nh = RW_HEAD
    ci = pl.program_id(1)

    @pl.when(ci == 0)
    def _():
        s2_ref[...] = jnp.zeros_like(s2_ref)

    w_ref[...] = w0_ref[...] + jnp.dot(low_ref[...], w2_ref[...], preferred_element_type=F32)
    a_ref[...] = a0_ref[...] + jnp.dot(loa_ref[...], a2_ref[...], preferred_element_type=F32)
    g_ref[...] = jnp.dot(log_ref[...], g2_ref[...], preferred_element_type=F32)

    lane_l = lax.broadcasted_iota(jnp.int32, (L, lanes), 1)
    row_l = lax.broadcasted_iota(jnp.int32, (L, lanes), 0)
    head0 = lane_l < nh
    n2 = 2 * L
    ti = lax.broadcasted_iota(jnp.int32, (n2, n2), 0)
    si = lax.broadcasted_iota(jnp.int32, (n2, n2), 1)
    strict = si < ti
    incl = si <= ti
    sq_r = lax.broadcasted_iota(jnp.int32, (lanes, lanes), 0) // nh
    sq_c = lax.broadcasted_iota(jnp.int32, (lanes, lanes), 1) // nh
    seg_ones = jnp.where(sq_r == sq_c, 1.0, 0.0).astype(BF16)
    nt = (((1,), (1,)), ((), ()))
    tn = (((0,), (0,)), ((), ()))

    def seg_sums(xs):
        his = [x.astype(BF16) for x in xs]
        los = [(x - hi.astype(F32)).astype(BF16) for x, hi in zip(xs, his)]
        return [jnp.dot(hi, seg_ones, preferred_element_type=F32) + jnp.dot(lo, seg_ones, preferred_element_type=F32)
                for hi, lo in zip(his, los)]

    def stack(x):
        return jnp.concatenate([jnp.where(head0, x, 0.0), jnp.where(head0, 0.0, x)], axis=0)

    dot = lambda x, y: jnp.dot(x, y, preferred_element_type=F32)
    dot_nt = lambda x, y: lax.dot_general(x, y, nt, preferred_element_type=F32)
    dot_tn = lambda x, y: lax.dot_general(x, y, tn, preferred_element_type=F32)

    def wave(ps):
        idx = range(len(ps))
        ls = [pl.ds(p * lanes, lanes) for p in ps]
        r = [r_ref[:, s] for s in ls]
        v = [v_ref[:, s] for s in ls]
        logd, cl, kk0, kmod, a = [], [], [], [], []
        for i in idx:
            k_i = k_ref[:, ls[i]]
            ld = -jnp.exp(-jax.nn.softplus(-w_ref[:, ls[i]]) - 0.5)
            c = ld
            sh = 1
            while sh < L:
                c = c + jnp.where(row_l >= sh, pltpu.roll(c, sh, axis=0), 0.0)
                sh *= 2
            a_i = jax.nn.sigmoid(a_ref[:, ls[i]])
            logd.append(ld)
            cl.append(c)
            a.append(a_i)
            kk0.append(k_i * kkp_ref[:, ls[i]])
            kmod.append(k_i * (1.0 + (a_i - 1.0) * kap_ref[:, ls[i]]))
        sums = seg_sums([kk0[i] * kk0[i] for i in idx] + [r[i] * kmod[i] * rkp_ref[:, ls[i]] for i in idx])
        nrm2 = sums[:len(ps)]
        bonus = [sums[len(ps) + i] * v[i] for i in idx]
        x1, x2, vs, d_last = [], [], [], []
        for i in idx:
            kk = kk0[i] / jnp.maximum(jnp.sqrt(nrm2[i]), 1e-12)
            d_inc = jnp.exp(cl[i])
            d_inv = jnp.exp(-cl[i])
            xa = stack(kk * jnp.exp(cl[i] - logd[i]))
            xr = stack(r[i] * d_inc)
            xb = stack(kk * a[i] * d_inv)
            xk = stack(kmod[i] * d_inv)
            x1.append(jnp.concatenate([xa, xr], axis=0).astype(BF16))
            x2.append(jnp.concatenate([xb, xk], axis=0).astype(BF16))
            vs.append(stack(v[i]))
            d_last.append(d_inc[L - 1:L, :])
        s2 = [s2_ref[p] for p in ps]
        g = [dot_nt(x1[i], x2[i]) for i in idx]
        pz = [dot_nt(x1[i], s2[i].astype(BF16)) for i in idx]
        m_k = [jnp.where(strict, g[i][:n2, n2:], 0.0).astype(BF16) for i in idx]
        m_pow = [jnp.where(strict, g[i][:n2, :n2], 0.0).astype(BF16) for i in idx]
        acat = [jnp.concatenate([jnp.where(incl, g[i][n2:, :n2], 0.0), jnp.where(incl, g[i][n2:, n2:], 0.0)],
                                axis=1).astype(BF16) for i in idx]
        vsb = [vs[i].astype(BF16) for i in idx]
        rhs = [pz[i][:n2] + dot(m_k[i], vsb[i]) for i in idx]
        u = [rhs[i] - dot(m_pow[i], rhs[i].astype(BF16)) for i in idx]
        pw = 2
        while pw < L:
            m_pow = [dot(m_pow[i], m_pow[i]).astype(BF16) for i in idx]
            u = [u[i] + dot(m_pow[i], u[i].astype(BF16)) for i in idx]
            pw *= 2
        wcat = [jnp.concatenate([-u[i], vs[i]], axis=0).astype(BF16) for i in idx]
        ys = [pz[i][n2:] + dot(acat[i], wcat[i]) for i in idx]
        upd = [dot_tn(wcat[i], x2[i]) for i in idx]
        for i in idx:
            s2_ref[ps[i]] = (s2[i] + upd[i]) * d_last[i]
        y = [ys[i][:L] + ys[i][L:] for i in idx]
        mean = [z * (1.0 / nh) for z in seg_sums(y)]
        yc = [y[i] - mean[i] for i in idx]
        var = [z * (1.0 / nh) for z in seg_sums([yc[i] * yc[i] for i in idx])]
        for i in idx:
            yn = yc[i] * lax.rsqrt(var[i] + RW_GN_EPS) * lg_ref[:, ls[i]] + lb_ref[:, ls[i]]
            y_ref[:, ls[i]] = ((yn + bonus[i]) * g_ref[:, ls[i]]).astype(y_ref.dtype)

    for p0 in range(0, npairs, unroll):
        wave(list(range(p0, p0 + unroll)))

    @pl.when(ci == pl.num_programs(1) - 1)
    def _():
        for p in range(npairs):
            blk = s2_ref[p]
            sT_ref[2 * p] = blk[:nh, :nh]
            sT_ref[2 * p + 1] = blk[nh:, nh:]


def _rwkv_chunked(arrs, lows, lora_w, lora_b, params, row0, bsz, t):
    m, d = arrs[0].shape
    lanes = V7X_LANES
    chunk = min(t, 64)
    npairs = d // lanes
    assert t % chunk == 0 and row0 % chunk == 0 and d % lanes == 0 and lanes == 2 * RW_HEAD
    nc = t // chunk
    rb0 = row0 // chunk
    rowblk = lambda b, c: (rb0 + b * nc + c, 0)
    seq = pl.BlockSpec((chunk, d), rowblk)
    vec = pl.BlockSpec((1, d), lambda b, c: (0, 0))
    whole = lambda a: pl.BlockSpec(a.shape, lambda b, c: (0, 0))
    est = 4 * 2 * _nbytes((chunk, d), F32) + 3 * _nbytes((chunk, d), F32) + 3 * _nbytes((npairs, lanes, lanes), F32) \
        + 2 * sum(_nbytes(w.shape, BF16) for w in lora_w) + (8 << 20)
    kern = functools.partial(_rwkv_chunk_kernel, chunk=chunk, npairs=npairs, unroll=16)
    return pl.pallas_call(
        kern,
        grid=(bsz, nc),
        in_specs=[seq] * 3 + [pl.BlockSpec((chunk, lo.shape[1]), rowblk) for lo in lows] + [whole(w) for w in lora_w]
        + [vec] * 7,
        out_specs=[seq, pl.BlockSpec((None, 2 * npairs, RW_HEAD, RW_HEAD), lambda b, c: (b, 0, 0, 0))],
        out_shape=[jax.ShapeDtypeStruct((m, d), BF16),
                   jax.ShapeDtypeStruct((bsz, 2 * npairs, RW_HEAD, RW_HEAD), F32)],
        scratch_shapes=[pltpu.VMEM((npairs, lanes, lanes), F32)] + [pltpu.VMEM((chunk, d), F32)] * 3,
        compiler_params=_params(("parallel", "arbitrary"), est),
        name="rwkv_chunk",
    )(*arrs, *lows, *lora_w, *lora_b, *[_row(q) for q in params])


def _rwkv_layer(x, groups, states, p, wb):
    d = x.shape[1]
    h, (xr, xw, xk, xv, xa, xg) = _rw_mix(x, p['norm_g'], p['rw_mu'], states[1][1], groups)
    w_rkv = wb['rw_w_rkv']
    r = _mm([_full(xr)], [(w_rkv, (0,), 0)], n=d, out_dtype=F32, name="rw_r")
    k = _mm([_full(xk)], [(w_rkv, (1,), 0)], n=d, out_dtype=F32, name="rw_k")
    v = _mm([_full(xv)], [(w_rkv, (2,), 0)], n=d, out_dtype=F32, name="rw_v")
    lo_w = _mm([_full(xw)], [(wb['rw_w1'], (), 0)], n=wb['rw_w1'].shape[1], out_dtype=BF16,
               epilogue=lambda accs, e, ev: jnp.tanh(accs[0]), name="rw_w1")
    lo_a = _mm([_full(xa)], [(wb['rw_a1'], (), 0)], n=wb['rw_a1'].shape[1], out_dtype=BF16, name="rw_a1")
    lo_g = _mm([_full(xg)], [(wb['rw_g1'], (), 0)], n=wb['rw_g1'].shape[1], out_dtype=BF16,
               epilogue=lambda accs, e, ev: jax.nn.sigmoid(accs[0]), name="rw_g1")
    lora_w = (wb['rw_w2'], wb['rw_a2'], wb['rw_g2'])
    lora_b = (_row(p['rw_w0']), _row(p['rw_a0']))
    add_vec = lambda accs, e, ev: ev[0] + accs[0]

    y, new_states = None, []
    for (r0, b, t), st in zip(groups, states):
        sl = slice(r0, r0 + b * t)
        if st[0] is None:
            assert y is None
            y, s_fin = _rwkv_chunked((r, k, v), (lo_w, lo_a, lo_g), lora_w, lora_b,
                                     [p[name] for name in ('rw_k_k', 'rw_k_a', 'rw_r_k', 'rw_ln_g', 'rw_ln_b')], r0, b, t)
        else:
            w_pre = _mm([_full(lo_w[sl])], [(lora_w[0], (), 0)], n=d, out_dtype=F32, evecs=[lora_b[0]],
                        epilogue=add_vec, name="rw_w2")
            a_pre = _mm([_full(lo_a[sl])], [(lora_w[1], (), 0)], n=d, out_dtype=F32, evecs=[lora_b[1]],
                        epilogue=add_vec, name="rw_a2")
            g = _mm([_full(lo_g[sl])], [(lora_w[2], (), 0)], n=d, out_dtype=F32, name="rw_g2")
            y_g, s_fin = _rwkv_group(r[sl], k[sl], v[sl], w_pre, a_pre, st[0], p, b, t)
            y = lax.dynamic_update_slice(y, (y_g * g).astype(y.dtype), (r0, 0))
        new_states.append((s_fin, lax.slice(h, (r0 + t - 1, 0), (r0 + b * t, d), (t, 1))))
    x = _mm([_full(y)], [(wb['rw_w_o'], (), 0)], n=d, out_dtype=F32, e_blocks=[x],
            epilogue=lambda accs, e, ev: e[0] + accs[0], name="rw_out")
    return x, new_states


def _mlstm_kernel(q_ref, k_ref, v_ref, og_ref, gt_ref, gtt_ref, bg_ref, bgt_ref, ng_ref, c0_ref, n0_ref, m0_ref, *rest,
                  nheads, dk, dv, chunk, bb, aliased):
    if aliased:
        rest = rest[1:]
    h_ref, cT_ref, nT_ref, mT_ref, c_ref, n_ref, m_ref = rest
    ci = pl.program_id(1)

    @pl.when(ci == 0)
    def _():
        c_ref[...] = c0_ref[...]
        n_ref[...] = n0_ref[...]
        m_ref[...] = m0_ref[...]

    L = chunk
    cap = lambda z: ML_GATE_CAP * jnp.tanh(z / ML_GATE_CAP)
    gates_c_all = cap(gt_ref[...] + bg_ref[...])
    gates_r_all = cap(gtt_ref[...] + bgt_ref[...])
    row_id = lax.broadcasted_iota(jnp.int32, (L, L), 0)
    col_id = lax.broadcasted_iota(jnp.int32, (L, L), 1)
    causal = col_id <= row_id
    k_scale = dk ** -0.5

    nt = (((1,), (1,)), ((), ()))
    tn = (((0,), (0,)), ((), ()))
    chains = [(bi, hd) for bi in range(bb) for hd in range(nheads)]
    idx = range(len(chains))
    rows = [slice(bi * L, (bi + 1) * L) for bi, _ in chains]
    li_c, b_c, m_prev, dmat, inter, m_t = [], [], [], [], [], []
    for (bi, hd), rw in zip(chains, rows):
        gates_c = gates_c_all[rw, :]
        gates_r = gates_r_all[:, rw]
        li_r = gates_r[hd:hd + 1, :]
        lf_c = jax.nn.log_sigmoid(gates_c[:, nheads + hd:nheads + hd + 1])
        lf_r = jax.nn.log_sigmoid(gates_r[nheads + hd:nheads + hd + 1, :])
        bc = jnp.sum(jnp.where(causal, lf_r, 0.0), axis=1, keepdims=True)
        br = jnp.sum(jnp.where(row_id <= col_id, lf_c, 0.0), axis=0, keepdims=True)
        mp = m_ref[bi, hd][:, :1]
        dm = jnp.where(causal, bc - br + li_r, -jnp.inf)
        it = bc + mp
        li_c.append(gates_c[:, hd:hd + 1])
        b_c.append(bc)
        m_prev.append(mp)
        dmat.append(dm)
        inter.append(it)
        m_t.append(jnp.maximum(jnp.max(dm, axis=1, keepdims=True), it))

    q = [q_ref[rw, hd * dk:(hd + 1) * dk] for (_, hd), rw in zip(chains, rows)]
    k = [k_ref[rw, hd * dk:(hd + 1) * dk] * k_scale for (_, hd), rw in zip(chains, rows)]
    v = [v_ref[rw, hd * dv:(hd + 1) * dv] for (_, hd), rw in zip(chains, rows)]
    qb = [z.astype(BF16) for z in q]
    kb = [z.astype(BF16) for z in k]
    c_old = [c_ref[bi, hd] for bi, hd in chains]
    n_old = [n_ref[bi, hd] for bi, hd in chains]
    qk = [lax.dot_general(qb[i], kb[i], nt, preferred_element_type=F32) for i in idx]
    cq = [lax.dot_general(qb[i], c_old[i].astype(BF16), nt, preferred_element_type=F32) for i in idx]
    s = [qk[i] * jnp.exp(dmat[i] - m_t[i]) for i in idx]
    sv = [jnp.dot(s[i].astype(BF16), v[i].astype(BF16), preferred_element_type=F32) for i in idx]
    m_new, ws = [], []
    for i in idx:
        b_last = b_c[i][L - 1:L, :]
        ws_log = b_last - b_c[i] + li_c[i]
        mn = jnp.maximum(b_last + m_prev[i], jnp.max(ws_log, axis=0, keepdims=True))
        m_new.append(mn)
        ws.append(jnp.exp(ws_log - mn))
    upd = [lax.dot_general((v[i] * ws[i]).astype(BF16), kb[i], tn, preferred_element_type=F32) for i in idx]
    h_out = [[None] * bb for _ in range(nheads)]
    for i, ((bi, hd), rw) in enumerate(zip(chains, rows)):
        w_inter = jnp.exp(inter[i] - m_t[i])
        num = sv[i] + w_inter * cq[i]
        nq = jnp.sum(q[i] * n_old[i], axis=1, keepdims=True)
        den = jnp.sum(s[i], axis=1, keepdims=True) + w_inter * nq
        h = num / jnp.maximum(jnp.abs(den), jnp.exp(-m_t[i]))
        hn = h * lax.rsqrt(jnp.mean(h * h, axis=1, keepdims=True) + ML_NORM_EPS)
        h_out[hd][bi] = hn * ng_ref[:, hd * dv:(hd + 1) * dv]
        dec = jnp.exp(b_c[i][L - 1:L, :] + m_prev[i] - m_new[i])
        c_ref[bi, hd] = dec * c_old[i] + upd[i]
        n_ref[bi, hd] = dec * n_old[i] + jnp.sum(ws[i] * k[i], axis=0, keepdims=True)
        m_ref[bi, hd] = jnp.broadcast_to(m_new[i], m_ref.shape[2:])
    for hd in range(nheads):
        cols = slice(hd * dv, (hd + 1) * dv)
        h_all = h_out[hd][0] if bb == 1 else jnp.concatenate(h_out[hd], axis=0)
        h_ref[:, cols] = (jax.nn.sigmoid(og_ref[:, cols]) * h_all).astype(h_ref.dtype)

    @pl.when(ci == pl.num_programs(1) - 1)
    def _():
        cT_ref[...] = c_ref[...]
        nT_ref[...] = n_ref[...]
        mT_ref[...] = m_ref[...]


def _mlstm_group(qkvo, gates, b_gates, norm_g, c0, n0, m0, group, h_prev_out):
    r0, bsz, t = group
    m_rows = qkvo.shape[0]
    nheads = ML_HEADS
    dk = c0.shape[-1]
    dv = c0.shape[-2]
    lanes = V7X_LANES
    chunk = min(t, lanes)
    bb = 1 if chunk >= V7X_SUBLANES else 2 * V7X_SUBLANES // chunk
    rows = bb * chunk
    assert t % chunk == 0 and bsz % bb == 0 and r0 % rows == 0
    nc = t // chunk
    nsteps = (bsz // bb) * nc
    rb0 = r0 // rows
    g_grp = gates[r0:r0 + bsz * t]
    g3t = g_grp.reshape(nsteps, rows, 2 * nheads).transpose(0, 2, 1)
    n0r = n0.astype(F32).reshape(bsz, nheads, 1, dk)
    m0r = jnp.broadcast_to(m0.astype(F32).reshape(bsz, nheads, 1, 1), (bsz, nheads, 1, lanes))
    wq, wv = nheads * dk, nheads * dv
    rowblk = lambda b, c: rb0 + b * nc + c
    seq = lambda w, cb: pl.BlockSpec((rows, w), functools.partial(lambda b, c, cb_: (rowblk(b, c), cb_), cb_=cb))
    st_c = pl.BlockSpec((bb, nheads, dv, dk), lambda b, c: (b, 0, 0, 0))
    st_n = pl.BlockSpec((bb, nheads, 1, dk), lambda b, c: (b, 0, 0, 0))
    st_m = pl.BlockSpec((bb, nheads, 1, lanes), lambda b, c: (b, 0, 0, 0))
    est = 2 * 3 * _nbytes((rows, 2 * wq + 2 * wv), F32) + 5 * bb * _nbytes((nheads, dv, dk), F32) + (8 << 20)
    aliased = h_prev_out is not None
    kern = functools.partial(_mlstm_kernel, nheads=nheads, dk=dk, dv=dv, chunk=chunk, bb=bb, aliased=aliased)
    in_specs = [seq(wq, 0), seq(wq, 1), seq(wv, wq * 2 // wv), seq(wv, wq * 2 // wv + 1),
                pl.BlockSpec((rows, 2 * nheads), lambda b, c: (rowblk(b, c), 0)),
                pl.BlockSpec((None, 2 * nheads, rows), lambda b, c: (b * nc + c, 0, 0)),
                pl.BlockSpec((1, 2 * nheads), lambda b, c: (0, 0)),
                pl.BlockSpec((2 * nheads, 1), lambda b, c: (0, 0)),
                pl.BlockSpec((1, wv), lambda b, c: (0, 0)),
                st_c, st_n, st_m]
    args = [qkvo, qkvo, qkvo, qkvo, gates, g3t, _row(b_gates), b_gates.reshape(-1, 1), _row(norm_g),
            c0.astype(F32), n0r, m0r]
    io_alias = {}
    if aliased:
        in_specs.append(pl.BlockSpec(memory_space=pl.ANY))
        args.append(h_prev_out)
        io_alias = {len(args) - 1: 0}
    h, c_fin, n_fin, m_fin = pl.pallas_call(
        kern,
        grid=(bsz // bb, nc),
        in_specs=in_specs,
        out_specs=[pl.BlockSpec((rows, wv), lambda b, c: (rowblk(b, c), 0)), st_c, st_n, st_m],
        out_shape=[jax.ShapeDtypeStruct((m_rows, wv), BF16),
                   jax.ShapeDtypeStruct((bsz, nheads, dv, dk), F32),
                   jax.ShapeDtypeStruct((bsz, nheads, 1, dk), F32),
                   jax.ShapeDtypeStruct((bsz, nheads, 1, lanes), F32)],
        scratch_shapes=[pltpu.VMEM((bb, nheads, dv, dk), F32), pltpu.VMEM((bb, nheads, 1, dk), F32),
                        pltpu.VMEM((bb, nheads, 1, lanes), F32)],
        input_output_aliases=io_alias,
        compiler_params=_params(("parallel", "arbitrary"), est),
        name="mlstm_chunk",
    )(*args)
    return h, c_fin, n_fin.reshape(bsz, nheads, dk), m_fin[:, :, 0, 0]


def _mlstm_layer(x, groups, states, p, wb):
    d = x.shape[1]
    nheads = ML_HEADS
    dv = d // nheads
    dk = dv // 2
    n_main = 2 * nheads * dk + 2 * nheads * dv
    xn = _rmsnorm(x, p['norm_g'], BF16)
    qkvo = _mm([_full(xn)], [(wb['ml_w_main'], (), 0)], n=n_main, out_dtype=F32, name="ml_in")
    gates = _mm([_full(xn)], [(wb['ml_w_gates'], (), 0)], n=2 * nheads, out_dtype=F32, name="ml_gates")
    h, new_states = None, []
    for grp, st in zip(groups, states):
        h, c_fin, n_fin, m_fin = _mlstm_group(qkvo, gates, p['ml_b_gates'], p['ml_norm_g'], st[0], st[1], st[2], grp, h)
        new_states.append((c_fin, n_fin, m_fin))
    x = _mm([_full(h)], [(wb['ml_w_o'], (), 0)], n=d, out_dtype=F32,
            e_blocks=[x], epilogue=lambda accs, e, ev: e[0] + accs[0], name="ml_out")
    return x, new_states


def _dwconv_kernel(u_ref, buf_ref, w_ref, *rest, ksize, t, bb, rows, halo, aliased):
    if aliased:
        rest = rest[1:]
    y_ref, ext_ref = rest
    w = w_ref[...]
    sub = V7X_SUBLANES
    lead = halo - (ksize - 1)
    n_tiles = t // rows
    taps_by_shift = [[j for j in range(ksize) if (j + lead) % sub == r] for r in range(sub)]
    ext_ref[pl.ds(halo + t, sub), :] = jnp.zeros((sub, ext_ref.shape[1]), F32)

    for b in range(bb):
        ext_ref[pl.ds(lead, ksize - 1), :] = buf_ref[b]
        ext_ref[pl.ds(halo, t), :] = u_ref[pl.ds(b * t, t), :]

        def tile(ti, carry):
            base = ti * rows if isinstance(ti, int) else pl.multiple_of(ti * rows, sub)
            acc = None
            for r in range(sub):
                part = None
                for j in taps_by_shift[r]:
                    q8 = (j + lead) - r
                    term = w[j:j + 1, :] * ext_ref[pl.ds(base + q8, rows + sub), :]
                    part = term if part is None else part + term
                if part is not None:
                    shifted = part[r:r + rows, :]
                    acc = shifted if acc is None else acc + shifted
            y_ref[pl.ds(b * t + base, rows), :] = acc
            return carry

        if n_tiles == 1:
            tile(0, 0)
        else:
            lax.fori_loop(0, n_tiles, tile, 0)


def _dwconv_group(u, buf, w, group, y_prev_out):
    r0, bsz, t = group
    m_rows, c = u.shape
    ksize = w.shape[0]
    sub = V7X_SUBLANES
    halo = _round_up(ksize - 1, sub)
    bc = 256
    bb = 1 if t >= 64 else 8
    rows = min(t, 64)
    blk = bb * t
    assert bsz % bb == 0 and t % rows == 0 and c % bc == 0 and r0 % blk == 0 and blk % sub == 0
    rb0 = r0 // blk
    aliased = y_prev_out is not None
    est = 4 * _nbytes((blk, bc), F32) + _nbytes((halo + t + sub, bc), F32) + (4 << 20)
    kern = functools.partial(_dwconv_kernel, ksize=ksize, t=t, bb=bb, rows=rows, halo=halo, aliased=aliased)
    in_specs = [pl.BlockSpec((blk, bc), lambda b, ci: (rb0 + b, ci)),
                pl.BlockSpec((bb, ksize - 1, bc), lambda b, ci: (b, 0, ci)),
                pl.BlockSpec((ksize, bc), lambda b, ci: (0, ci))]
    args = [u, buf.astype(F32), w]
    io_alias = {}
    if aliased:
        in_specs.append(pl.BlockSpec(memory_space=pl.ANY))
        args.append(y_prev_out)
        io_alias = {3: 0}
    return pl.pallas_call(
        kern,
        grid=(bsz // bb, c // bc),
        in_specs=in_specs,
        out_specs=pl.BlockSpec((blk, bc), lambda b, ci: (rb0 + b, ci)),
        out_shape=jax.ShapeDtypeStruct((m_rows, c), F32),
        scratch_shapes=[pltpu.VMEM((halo + t + sub, bc), F32)],
        input_output_aliases=io_alias,
        compiler_params=_params(("parallel", "parallel"), est),
        name="dwconv",
    )(*args)


def _conv_groups(u, w_dw, groups, bufs):
    y, new_bufs = None, []
    c = u.shape[1]
    km1 = w_dw.shape[0] - 1
    for (r0, b, t), buf in zip(groups, bufs):
        y = _dwconv_group(u, buf, w_dw, (r0, b, t), y)
        if t >= km1:
            tail = r0 + t - km1 + jnp.arange(b)[:, None] * t + jnp.arange(km1)[None, :]
            new_bufs.append(u[tail])
        else:
            new_bufs.append(jnp.concatenate([buf.astype(F32)[:, t:], u[r0:r0 + b * t].reshape(b, t, c)], axis=1))
    return y, new_bufs


def _conformer_layer(x, groups, states, p, wb):
    d = x.shape[1]
    b1 = _row(p['cf_b_pw1'])
    w1 = wb['cf_w_pw1']
    u = _mm([_full(x)], [(w1, (), 0), (w1, (), d)], n=d, out_dtype=F32, avecs=[_row(p['norm_g'])],
            prologue=lambda a, v, pw: _rms(a[0]) * v[0], evecs=[b1[:, :d], b1[:, d:]],
            epilogue=lambda accs, e, ev: (accs[0] + ev[0]) * jax.nn.sigmoid(accs[1] + ev[1]), name="cf_pw1")
    y, new_bufs = _conv_groups(u, p['cf_w_dw'], groups, [st[0] for st in states])

    def ln_silu(a, v, pw):
        z = a[0] + v[0]
        mu = jnp.mean(z, axis=-1, keepdims=True)
        var = jnp.mean(jnp.square(z - mu), axis=-1, keepdims=True)
        z = (z - mu) * lax.rsqrt(var + CF_LN_EPS) * v[1] + v[2]
        return z * jax.nn.sigmoid(z)

    x = _mm([_full(y)], [(wb['cf_w_pw2'], (), 0)], n=d, out_dtype=F32, prologue=ln_silu,
            avecs=[_row(p['cf_b_dw']), _row(p['cf_ln_g']), _row(p['cf_ln_b'])],
            e_blocks=[x], evecs=[_row(p['cf_b_pw2'])],
            epilogue=lambda accs, e, ev: e[0] + (accs[0] + ev[0]), name="cf_pw2")
    return x, [(nb,) for nb in new_bufs]


def _sconv_layer(x, groups, states, p, wb):
    d = x.shape[1]
    xn = _rmsnorm(x, p['norm_g'], BF16)
    w_in = wb['sc_w_in']
    b_gate = _mm([_full(xn)], [(w_in, (), 0)], n=d, out_dtype=F32, name="sc_b")
    ch = _mm([_full(xn)], [(w_in, (), d), (w_in, (), 2 * d)], n=d, out_dtype=F32,
             epilogue=lambda accs, e, ev: accs[0] * accs[1], name="sc_ch")
    y, new_bufs = _conv_groups(ch, p['sc_w_dw'], groups, [st[0] for st in states])
    x = _mm([_full(b_gate), _full(y)], [(wb['sc_w_out'], (), 0)], n=d, out_dtype=F32,
            prologue=lambda a, v, pw: a[0] * a[1], e_blocks=[x],
            epilogue=lambda accs, e, ev: e[0] + accs[0], name="sc_out")
    return x, [(nb,) for nb in new_bufs]


def kernel(x_prompt, x_sample, state_rwkv_wkv, state_rwkv_shift, state_mlstm_C, state_mlstm_n, state_mlstm_m,
           state_conf_conv, state_sconv_conv, norm_g, ffn_w_gu, ffn_w_down, final_norm_g,
           rw_mu, rw_w_rkv, rw_w0, rw_w1, rw_w2, rw_a0, rw_a1, rw_a2, rw_g1, rw_g2, rw_k_k, rw_k_a, rw_r_k,
           rw_ln_g, rw_ln_b, rw_w_o, ml_w_in, ml_b_gates, ml_norm_g, ml_w_o,
           cf_w_pw1, cf_b_pw1, cf_w_dw, cf_b_dw, cf_ln_g, cf_ln_b, cf_w_pw2, cf_b_pw2,
           sc_w_in, sc_w_dw, sc_w_out):
    bp, tp, d = x_prompt.shape
    bs, ts, _ = x_sample.shape
    depth = norm_g.shape[0]
    n_mixers = 4
    groups = [(0, bp, tp), (bp * tp, bs, ts)]
    x = jnp.concatenate([x_prompt.reshape(bp * tp, d), x_sample.reshape(bs * ts, d)], axis=0).astype(F32)

    dv = d // ML_HEADS
    dk = dv // 2
    cf_k = cf_w_dw.shape[1]
    sc_k = sc_w_dw.shape[1]
    zeros = lambda *s: jnp.zeros(s, F32)

    out_states = {name: ([], []) for name in ('wkv', 'shift', 'C', 'n', 'm', 'cf', 'sc')}
    for i in range(depth):
        j = i // n_mixers
        kind = i % n_mixers
        x = _ffn(x, norm_g[i, 0], ffn_w_gu, ffn_w_down, (i, 0))
        if kind == 0:
            p = dict(norm_g=norm_g[i, 1], rw_mu=rw_mu[j], rw_w0=rw_w0[j], rw_a0=rw_a0[j], rw_k_k=rw_k_k[j],
                     rw_k_a=rw_k_a[j], rw_r_k=rw_r_k[j], rw_ln_g=rw_ln_g[j], rw_ln_b=rw_ln_b[j])
            wb = dict(rw_w_rkv=rw_w_rkv[j].astype(BF16), rw_w1=rw_w1[j].astype(BF16), rw_w2=rw_w2[j].astype(BF16),
                      rw_a1=rw_a1[j].astype(BF16), rw_a2=rw_a2[j].astype(BF16), rw_g1=rw_g1[j].astype(BF16),
                      rw_g2=rw_g2[j].astype(BF16), rw_w_o=rw_w_o[j].astype(BF16))
            states = [(None, zeros(bp, d)), (state_rwkv_wkv[j], state_rwkv_shift[j].astype(F32))]
            x, new = _rwkv_layer(x, groups, states, p, wb)
            for gi in range(2):
                out_states['wkv'][gi].append(new[gi][0])
                out_states['shift'][gi].append(new[gi][1])
        elif kind == 1:
            p = dict(norm_g=norm_g[i, 1], ml_b_gates=ml_b_gates[j], ml_norm_g=ml_norm_g[j])
            n_main = 2 * ML_HEADS * dk + 2 * ML_HEADS * dv
            wb = dict(ml_w_main=ml_w_in[j][:, :n_main].astype(BF16), ml_w_gates=ml_w_in[j][:, n_main:].astype(BF16),
                      ml_w_o=ml_w_o[j].astype(BF16))
            states = [(zeros(bp, ML_HEADS, dv, dk), zeros(bp, ML_HEADS, dk), zeros(bp, ML_HEADS)),
                      (state_mlstm_C[j], state_mlstm_n[j], state_mlstm_m[j])]
            x, new = _mlstm_layer(x, groups, states, p, wb)
            for gi in range(2):
                out_states['C'][gi].append(new[gi][0])
                out_states['n'][gi].append(new[gi][1])
                out_states['m'][gi].append(new[gi][2])
        elif kind == 2:
            p = dict(norm_g=norm_g[i, 1], cf_b_pw1=cf_b_pw1[j], cf_w_dw=cf_w_dw[j], cf_b_dw=cf_b_dw[j],
                     cf_ln_g=cf_ln_g[j], cf_ln_b=cf_ln_b[j], cf_b_pw2=cf_b_pw2[j])
            wb = dict(cf_w_pw1=cf_w_pw1[j].astype(BF16), cf_w_pw2=cf_w_pw2[j].astype(BF16))
            states = [(zeros(bp, cf_k - 1, d),), (state_conf_conv[j],)]
            x, new = _conformer_layer(x, groups, states, p, wb)
            for gi in range(2):
                out_states['cf'][gi].append(new[gi][0])
        else:
            p = dict(norm_g=norm_g[i, 1], sc_w_dw=sc_w_dw[j])
            wb = dict(sc_w_in=sc_w_in[j].astype(BF16), sc_w_out=sc_w_out[j].astype(BF16))
            states = [(zeros(bp, sc_k - 1, d),), (state_sconv_conv[j],)]
            x, new = _sconv_layer(x, groups, states, p, wb)
            for gi in range(2):
                out_states['sc'][gi].append(new[gi][0])
        x = _ffn(x, norm_g[i, 2], ffn_w_gu, ffn_w_down, (i, 1))

    y_prompt = _rmsnorm(x, final_norm_g, x_prompt.dtype, 0, bp * tp).reshape(bp, tp, d)
    y_sample = _rmsnorm(x, final_norm_g, x_sample.dtype, bp * tp, bs * ts).reshape(bs, ts, d)
    order = ('wkv', 'shift', 'C', 'n', 'm', 'cf', 'sc')
    dt = x_prompt.dtype
    prompt_out = tuple(jnp.stack(out_states[name][0]).astype(dt) for name in order)
    sample_out = tuple(jnp.stack(out_states[name][1]).astype(st.dtype) for name, st in
                       zip(order, (state_rwkv_wkv, state_rwkv_shift, state_mlstm_C, state_mlstm_n, state_mlstm_m,
                                   state_conf_conv, state_sconv_conv)))
    return (y_prompt, y_sample) + prompt_out + sample_out
```

```python
import functools
import math

import jax
import jax.numpy as jnp
from jax import lax
from jax.experimental import pallas as pl
from jax.experimental.pallas import tpu as pltpu

F32 = jnp.float32
BF16 = jnp.bfloat16

NORM_EPS = 1e-6
RW_HEAD = 64
RW_GN_EPS = 64e-5
ML_HEADS = 8
ML_GATE_CAP = 15.0
ML_NORM_EPS = 1e-6
CF_LN_EPS = 1e-5

V7X_LANES = 128
V7X_SUBLANES = 8
V7X_VMEM_BYTES = 64 * 1024 * 1024
VMEM_LIMIT_CAP = V7X_VMEM_BYTES - 6 * 1024 * 1024


def _nbytes(shape, dtype):
    return math.prod(shape) * jnp.dtype(dtype).itemsize


def _params(sem, vmem_est):
    limit = int(min(max(vmem_est * 1.2 + (2 << 20), 16 << 20), VMEM_LIMIT_CAP))
    return pltpu.CompilerParams(dimension_semantics=sem, vmem_limit_bytes=limit)


def _pick_bm(m, target, also_divides=()):
    best = None
    for bm in range(16, min(m, target) + 1, 16):
        if m % bm == 0 and all(o % bm == 0 for o in also_divides):
            best = bm
    assert best is not None
    return best


def _rms(x):
    return x * lax.rsqrt(jnp.mean(x * x, axis=-1, keepdims=True) + NORM_EPS)


def _round_up(x, m):
    return -(-x // m) * m


def _mm_kernel(*refs, counts, prologue, epilogue, use_scratch, has_side):
    n_a, n_av, n_pw, n_w, n_e, n_ev = counts
    pos = 0
    groups = []
    for c in counts:
        groups.append(refs[pos:pos + c])
        pos += c
    a_refs, av_refs, pw_refs, w_refs, e_refs, ev_refs = groups
    if has_side:
        side_in, o_ref, side_out = refs[pos:pos + 3]
        side_out[...] = side_in[...].astype(side_out.dtype)
    else:
        o_ref = refs[pos]
    if use_scratch:
        lhs_ref = refs[-1]

        @pl.when(pl.program_id(1) == 0)
        def _():
            lhs_ref[...] = prologue([r[...] for r in a_refs], [r[...] for r in av_refs],
                                    [r[...] for r in pw_refs]).astype(BF16)

        lhs = lhs_ref[...]
    else:
        lhs = a_refs[0][...]
    accs = [jnp.dot(lhs, w[...].astype(BF16), preferred_element_type=F32) for w in w_refs]
    out = epilogue(accs, [r[...] for r in e_refs], [r[...] for r in ev_refs])
    o_ref[...] = out.astype(o_ref.dtype)


def _mm(a_list, w_list, *, n, out_dtype, prologue=None, epilogue=None, avecs=(), pw_list=(), e_blocks=(), evecs=(),
        side_cast=None, bm_target=1088, bn=512, name="mm"):
    m = a_list[0][0].shape[0]
    k = w_list[0][0].shape[-2]
    bm = _pick_bm(m, bm_target)
    bn = min(bn, n)
    assert n % bn == 0
    use_scratch = prologue is not None or a_list[0][0].dtype != BF16
    if prologue is None:
        prologue = lambda a, v, pw: a[0]
    if epilogue is None:
        epilogue = lambda accs, e, ev: accs[0]

    in_specs, args, est = [], [], 0
    for arr, ka, cb in a_list:
        in_specs.append(pl.BlockSpec((bm, ka), functools.partial(lambda i, j, c: (i, c), c=cb)))
        args.append(arr)
        est += 2 * _nbytes((bm, ka), arr.dtype)
    for v in avecs:
        in_specs.append(pl.BlockSpec(v.shape, lambda i, j: (0, 0)))
        args.append(v)
    for pw in pw_list:
        in_specs.append(pl.BlockSpec(pw.shape, lambda i, j: (0, 0)))
        args.append(pw)
        est += 2 * _nbytes(pw.shape, pw.dtype)
    for arr, lead, off in w_list:
        assert off % bn == 0 and arr.shape[-2] == k
        in_specs.append(pl.BlockSpec((None,) * len(lead) + (k, bn),
                                     functools.partial(lambda i, j, l, o: l + (0, j + o), l=tuple(lead), o=off // bn)))
        args.append(arr)
        est += 2 * _nbytes((k, bn), arr.dtype) + (_nbytes((k, bn), BF16) if arr.dtype != BF16 else 0)
    for e in e_blocks:
        in_specs.append(pl.BlockSpec((bm, bn), lambda i, j: (i, j)))
        args.append(e)
        est += 2 * _nbytes((bm, bn), e.dtype)
    for v in evecs:
        in_specs.append(pl.BlockSpec((1, bn), lambda i, j: (0, j)))
        args.append(v)
    est += 2 * _nbytes((bm, bn), out_dtype) + (len(w_list) + 2) * _nbytes((bm, bn), F32)
    scratch = []
    if use_scratch:
        scratch.append(pltpu.VMEM((bm, k), BF16))
        est += _nbytes((bm, k), BF16) + 2 * _nbytes((bm, a_list[0][1]), F32)

    grid = (m // bm, n // bn)
    out_specs = pl.BlockSpec((bm, bn), lambda i, j: (i, j))
    out_shape = jax.ShapeDtypeStruct((m, n), out_dtype)
    if side_cast is not None:
        s_arr, s_lead = side_cast
        s_rows, s_cols = s_arr.shape[-2:]
        steps = grid[0] * grid[1]
        assert s_rows % (steps * 2 * V7X_SUBLANES) == 0
        rps = s_rows // steps
        nj = grid[1]
        in_specs.append(pl.BlockSpec((None,) * len(s_lead) + (rps, s_cols),
                                     functools.partial(lambda i, j, l: l + (i * nj + j, 0), l=tuple(s_lead))))
        args.append(s_arr)
        out_specs = [out_specs, pl.BlockSpec((rps, s_cols), lambda i, j: (i * nj + j, 0))]
        out_shape = [out_shape, jax.ShapeDtypeStruct((s_rows, s_cols), BF16)]
        est += 2 * _nbytes((rps, s_cols), F32) + 2 * _nbytes((rps, s_cols), BF16)

    counts = (len(a_list), len(avecs), len(pw_list), len(w_list), len(e_blocks), len(evecs))
    kern = functools.partial(_mm_kernel, counts=counts, prologue=prologue, epilogue=epilogue, use_scratch=use_scratch,
                             has_side=side_cast is not None)
    return pl.pallas_call(
        kern,
        grid=grid,
        in_specs=in_specs,
        out_specs=out_specs,
        out_shape=out_shape,
        scratch_shapes=scratch,
        compiler_params=_params(("parallel", "arbitrary"), est),
        name=name,
    )(*args)


def _full(arr):
    return (arr, arr.shape[1], 0)


def _row(v):
    return v.reshape(1, -1)


def _rmsnorm_kernel(x_ref, g_ref, o_ref):
    o_ref[...] = (_rms(x_ref[...]) * g_ref[...]).astype(o_ref.dtype)


def _rmsnorm(x, g, out_dtype, row0=0, nrows=None):
    d = x.shape[1]
    m = x.shape[0] if nrows is None else nrows
    bm = _pick_bm(m, 1088, also_divides=(row0,) if row0 else ())
    rb0 = row0 // bm
    est = 2 * _nbytes((bm, d), F32) * 3
    return pl.pallas_call(
        _rmsnorm_kernel,
        grid=(m // bm,),
        in_specs=[pl.BlockSpec((bm, d), lambda i: (rb0 + i, 0)), pl.BlockSpec((1, d), lambda i: (0, 0))],
        out_specs=pl.BlockSpec((bm, d), lambda i: (i, 0)),
        out_shape=jax.ShapeDtypeStruct((m, d), out_dtype),
        compiler_params=_params(("parallel",), est),
        name="rmsnorm",
    )(x, _row(g))


def _ffn(x, g, w_gu, w_down, lead):
    d_ff = w_down.shape[-2]
    h, w_down_b = _mm([_full(x)], [(w_gu, lead, 0), (w_gu, lead, d_ff)], n=d_ff, out_dtype=BF16, avecs=[_row(g)],
                      prologue=lambda a, v, pw: _rms(a[0]) * v[0],
                      epilogue=lambda accs, e, ev: accs[0] * jax.nn.sigmoid(accs[0]) * accs[1],
                      side_cast=(w_down, lead), name="ffn_up")
    return _mm([_full(h)], [(w_down_b, (), 0)], n=x.shape[1], out_dtype=F32, e_blocks=[x],
               epilogue=lambda accs, e, ev: e[0] + 0.5 * accs[0], name="ffn_down")


def _rw_mix_kernel(x_ref, g_ref, mu_ref, ov_ref, h_ref, *rest, n_prompt_blocks, blocks_per_seq, ts):
    outs, carry_ref = rest[:-1], rest[-1]
    pid = pl.program_id(0)

    @pl.when(pid == 0)
    def _():
        carry_ref[...] = jnp.zeros_like(carry_ref)

    h = _rms(x_ref[...]) * g_ref[...]
    h_ref[...] = h
    bm = h.shape[0]
    rolled = pltpu.roll(h, 1, axis=0)
    row = lax.broadcasted_iota(jnp.int32, h.shape, 0)
    carry = jnp.where(pid % blocks_per_seq == 0, 0.0, carry_ref[...])
    prev_prompt = jnp.where(row == 0, carry, rolled)
    prev_sample = jnp.where(row % ts == 0, ov_ref[...], rolled)
    prev = jnp.where(pid >= n_prompt_blocks, prev_sample, prev_prompt)
    carry_ref[...] = h[bm - 1:bm, :]
    delta = prev - h
    for i, o in enumerate(outs):
        o[...] = (h + delta * mu_ref[i:i + 1, :]).astype(BF16)


def _rw_mix(x, g, mu, shift_sample, groups):
    m, d = x.shape
    (_, bp, tp), (r0s, bs, ts) = groups
    bm = _pick_bm(tp, 512, also_divides=(bs * ts,))
    assert bm % ts == 0 and r0s == bp * tp
    n_prompt_blocks = (bp * tp) // bm
    ov = jnp.zeros((bs, ts, d), F32).at[:, 0].set(shift_sample.astype(F32)).reshape(bs * ts, d)
    n_mix = mu.shape[0]
    blk = pl.BlockSpec((bm, d), lambda i: (i, 0))
    est = 2 * _nbytes((bm, d), F32) * 3 + 2 * n_mix * _nbytes((bm, d), BF16) + 6 * _nbytes((bm, d), F32)
    kern = functools.partial(_rw_mix_kernel, n_prompt_blocks=n_prompt_blocks, blocks_per_seq=tp // bm, ts=ts)
    outs = pl.pallas_call(
        kern,
        grid=(m // bm,),
        in_specs=[blk, pl.BlockSpec((1, d), lambda i: (0, 0)), pl.BlockSpec((n_mix, d), lambda i: (0, 0)),
                  pl.BlockSpec((bm, d), lambda i: (jnp.maximum(i - n_prompt_blocks, 0), 0))],
        out_specs=[blk] * (1 + n_mix),
        out_shape=[jax.ShapeDtypeStruct((m, d), F32)] + [jax.ShapeDtypeStruct((m, d), BF16)] * n_mix,
        scratch_shapes=[pltpu.VMEM((1, d), F32)],
        compiler_params=_params(("arbitrary",), est),
        name="rw_mix",
    )(x, _row(g), mu, ov)
    return outs[0], outs[1:]


def _rwkv_steps(r_ref, k_ref, v_ref, w_ref, a_ref, y_ref, param_refs, s_ref, tb, nh):
    kk_p, ka_p, rk_p, ln_g, ln_b = (q[...] for q in param_refs)
    vgrp = V7X_SUBLANES

    def step(t, carry):
        r = r_ref[t]
        k = k_ref[t]
        v = v_ref[t]
        logw = -jax.nn.softplus(-w_ref[t]) - 0.5
        d = jnp.exp(-jnp.exp(logw))
        a = jax.nn.sigmoid(a_ref[t])
        kk = k * kk_p
        kk = kk / jnp.maximum(jnp.sqrt(jnp.sum(kk * kk, axis=0, keepdims=True)), 1e-12)
        kmod = k * (1.0 + (a - 1.0) * ka_p)
        b = kk * a

        def vgroup(g, c2):
            base = pl.multiple_of(g * vgrp, vgrp)
            v8 = v_ref[t, pl.ds(base, vgrp), :]
            ys = []
            for j in range(vgrp):
                s_old = s_ref[base + j]
                sa = jnp.sum(s_old * kk, axis=0, keepdims=True)
                s_new = s_old * d - sa * b + v8[j:j + 1, :] * kmod
                s_ref[base + j] = s_new
                ys.append(jnp.sum(s_new * r, axis=0, keepdims=True))
            y_ref[t, pl.ds(base, vgrp), :] = jnp.concatenate(ys, axis=0)
            return c2

        lax.fori_loop(0, nh // vgrp, vgroup, 0)

        y = y_ref[t]
        mean = jnp.mean(y, axis=0, keepdims=True)
        var = jnp.mean(jnp.square(y - mean), axis=0, keepdims=True)
        yn = (y - mean) * lax.rsqrt(var + RW_GN_EPS) * ln_g + ln_b
        bonus = jnp.sum(r * kmod * rk_p, axis=0, keepdims=True) * v
        y_ref[t] = yn + bonus
        return carry

    lax.fori_loop(0, tb, step, 0)


def _rwkv_scan_kernel(r_ref, k_ref, v_ref, w_ref, a_ref, kk_p_ref, ka_p_ref, rk_p_ref, lg_ref, lb_ref, s0_ref,
                      y_ref, sT_ref, s_ref, *, tb, nh):
    tblk = pl.program_id(1)

    @pl.when(tblk == 0)
    def _():
        s_ref[...] = s0_ref[...]

    _rwkv_steps(r_ref, k_ref, v_ref, w_ref, a_ref, y_ref, (kk_p_ref, ka_p_ref, rk_p_ref, lg_ref, lb_ref), s_ref, tb, nh)

    @pl.when(tblk == pl.num_programs(1) - 1)
    def _():
        sT_ref[...] = s_ref[...]


def _rwkv_scan(r, k, v, w_pre, a_pre, kk_p, ka_p, rk_p, ln_g, ln_b, s0, *, tb):
    t, nh, c = r.shape
    lanes = V7X_LANES
    assert c % lanes == 0 and t % tb == 0
    seq = pl.BlockSpec((tb, nh, lanes), lambda ci, ti: (ti, 0, ci))
    par = pl.BlockSpec((nh, lanes), lambda ci, ti: (0, ci))
    st = pl.BlockSpec((nh, nh, lanes), lambda ci, ti: (0, 0, ci))
    est = 6 * 2 * _nbytes((tb, nh, lanes), F32) + 5 * _nbytes((nh, nh, lanes), F32)
    kern = functools.partial(_rwkv_scan_kernel, tb=tb, nh=nh)
    return pl.pallas_call(
        kern,
        grid=(c // lanes, t // tb),
        in_specs=[seq] * 5 + [par] * 5 + [st],
        out_specs=[seq, st],
        out_shape=[jax.ShapeDtypeStruct((t, nh, c), F32), jax.ShapeDtypeStruct((nh, nh, c), F32)],
        scratch_shapes=[pltpu.VMEM((nh, nh, lanes), F32)],
        compiler_params=_params(("parallel", "arbitrary"), est),
        name="rwkv_scan",
    )(r, k, v, w_pre, a_pre, kk_p, ka_p, rk_p, ln_g, ln_b, s0)


def _to_chains(x, bsz, t, nheads):
    return x.reshape(bsz, t, nheads, RW_HEAD).transpose(1, 3, 0, 2).reshape(t, RW_HEAD, bsz * nheads)


def _from_chains(y, bsz, t, nheads):
    return y.reshape(t, RW_HEAD, bsz, nheads).transpose(2, 0, 3, 1).reshape(bsz * t, nheads * RW_HEAD)


def _rwkv_group(r, k, v, w_pre, a_pre, wkv0, p, bsz, t):
    d_model = r.shape[1]
    nheads = d_model // RW_HEAD
    c = bsz * nheads
    chain_param = lambda q: jnp.tile(q.reshape(nheads, RW_HEAD).T, (1, bsz))
    if wkv0 is None:
        s0 = jnp.zeros((RW_HEAD, RW_HEAD, c), F32)
    else:
        s0 = wkv0.astype(F32).transpose(2, 3, 0, 1).reshape(RW_HEAD, RW_HEAD, c)
    tb = t if t <= 32 else 32
    y, s_fin = _rwkv_scan(*[_to_chains(q, bsz, t, nheads) for q in (r, k, v, w_pre, a_pre)],
                          *[chain_param(p[name]) for name in ('rw_k_k', 'rw_k_a', 'rw_r_k', 'rw_ln_g', 'rw_ln_b')],
                          s0, tb=tb)
    s_fin = s_fin.reshape(RW_HEAD, RW_HEAD, bsz, nheads).transpose(2, 3, 0, 1)
    return _from_chains(y, bsz, t, nheads), s_fin


def _rwkv_chunk_kernel(r_ref, k_ref, v_ref, low_ref, loa_ref, log_ref, w2_ref, a2_ref, g2_ref, w0_ref, a0_ref,
                       kkp_ref, kap_ref, rkp_ref, lg_ref, lb_ref,
                       y_ref, sT_ref, s2_ref, w_ref, a_ref, g_ref, *, chunk, npairs, unroll):
    L = chunk
    lanes = V7X_LANES
    nh = RW_HEAD
    ci = pl.program_id(1)

    @pl.when(ci == 0)
    def _():
        s2_ref[...] = jnp.zeros_like(s2_ref)

    w_ref[...] = w0_ref[...] + jnp.dot(low_ref[...], w2_ref[...], preferred_element_type=F32)
    a_ref[...] = a0_ref[...] + jnp.dot(loa_ref[...], a2_ref[...], preferred_element_type=F32)
    g_ref[...] = jnp.dot(log_ref[...], g2_ref[...], preferred_element_type=F32)

    lane_l = lax.broadcasted_iota(jnp.int32, (L, lanes), 1)
    row_l = lax.broadcasted_iota(jnp.int32, (L, lanes), 0)
    head0 = lane_l < nh
    n2 = 2 * L
    ti = lax.broadcasted_iota(jnp.int32, (n2, n2), 0)
    si = lax.broadcasted_iota(jnp.int32, (n2, n2), 1)
    strict = si < ti
    incl = si <= ti
    sq_r = lax.broadcasted_iota(jnp.int32, (lanes, lanes), 0) // nh
    sq_c = lax.broadcasted_iota(jnp.int32, (lanes, lanes), 1) // nh
    seg_ones = jnp.where(sq_r == sq_c, 1.0, 0.0).astype(BF16)
    nt = (((1,), (1,)), ((), ()))
    tn = (((0,), (0,)), ((), ()))

    def seg_sums(xs):
        his = [x.astype(BF16) for x in xs]
        los = [(x - hi.astype(F32)).astype(BF16) for x, hi in zip(xs, his)]
        return [jnp.dot(hi, seg_ones, preferred_element_type=F32) + jnp.dot(lo, seg_ones, preferred_element_type=F32)
                for hi, lo in zip(his, los)]

    def stack(x):
        return jnp.concatenate([jnp.where(head0, x, 0.0), jnp.where(head0, 0.0, x)], axis=0)

    dot = lambda x, y: jnp.dot(x, y, preferred_element_type=F32)
    dot_nt = lambda x, y: lax.dot_general(x, y, nt, preferred_element_type=F32)
    dot_tn = lambda x, y: lax.dot_general(x, y, tn, preferred_element_type=F32)

    def wave(ps):
        idx = range(len(ps))
        ls = [pl.ds(p * lanes, lanes) for p in ps]
        r = [r_ref[:, s] for s in ls]
        v = [v_ref[:, s] for s in ls]
        logd, cl, kk0, kmod, a = [], [], [], [], []
        for i in idx:
            k_i = k_ref[:, ls[i]]
            ld = -jnp.exp(-jax.nn.softplus(-w_ref[:, ls[i]]) - 0.5)
            c = ld
            sh = 1
            while sh < L:
                c = c + jnp.where(row_l >= sh, pltpu.roll(c, sh, axis=0), 0.0)
                sh *= 2
            a_i = jax.nn.sigmoid(a_ref[:, ls[i]])
            logd.append(ld)
            cl.append(c)
            a.append(a_i)
            kk0.append(k_i * kkp_ref[:, ls[i]])
            kmod.append(k_i * (1.0 + (a_i - 1.0) * kap_ref[:, ls[i]]))
        sums = seg_sums([kk0[i] * kk0[i] for i in idx] + [r[i] * kmod[i] * rkp_ref[:, ls[i]] for i in idx])
        nrm2 = sums[:len(ps)]
        bonus = [sums[len(ps) + i] * v[i] for i in idx]
        x1, x2, vs, d_last = [], [], [], []
        for i in idx:
            kk = kk0[i] / jnp.maximum(jnp.sqrt(nrm2[i]), 1e-12)
            d_inc = jnp.exp(cl[i])
            d_inv = jnp.exp(-cl[i])
            xa = stack(kk * jnp.exp(cl[i] - logd[i]))
            xr = stack(r[i] * d_inc)
            xb = stack(kk * a[i] * d_inv)
            xk = stack(kmod[i] * d_inv)
            x1.append(jnp.concatenate([xa, xr], axis=0).astype(BF16))
            x2.append(jnp.concatenate([xb, xk], axis=0).astype(BF16))
            vs.append(stack(v[i]))
            d_last.append(d_inc[L - 1:L, :])
        s2 = [s2_ref[p] for p in ps]
        g = [dot_nt(x1[i], x2[i]) for i in idx]
        pz = [dot_nt(x1[i], s2[i].astype(BF16)) for i in idx]
        m_k = [jnp.where(strict, g[i][:n2, n2:], 0.0).astype(BF16) for i in idx]
        m_pow = [jnp.where(strict, g[i][:n2, :n2], 0.0).astype(BF16) for i in idx]
        acat = [jnp.concatenate([jnp.where(incl, g[i][n2:, :n2], 0.0), jnp.where(incl, g[i][n2:, n2:], 0.0)],
                                axis=1).astype(BF16) for i in idx]
        vsb = [vs[i].astype(BF16) for i in idx]
        rhs = [pz[i][:n2] + dot(m_k[i], vsb[i]) for i in idx]
        u = [rhs[i] - dot(m_pow[i], rhs[i].astype(BF16)) for i in idx]
        pw = 2
        while pw < L:
            m_pow = [dot(m_pow[i], m_pow[i]).astype(BF16) for i in idx]
            u = [u[i] + dot(m_pow[i], u[i].astype(BF16)) for i in idx]
            pw *= 2
        wcat = [jnp.concatenate([-u[i], vs[i]], axis=0).astype(BF16) for i in idx]
        ys = [pz[i][n2:] + dot(acat[i], wcat[i]) for i in idx]
        upd = [dot_tn(wcat[i], x2[i]) for i in idx]
        for i in idx:
            s2_ref[ps[i]] = (s2[i] + upd[i]) * d_last[i]
        y = [ys[i][:L] + ys[i][L:] for i in idx]
        mean = [z * (1.0 / nh) for z in seg_sums(y)]
        yc = [y[i] - mean[i] for i in idx]
        var = [z * (1.0 / nh) for z in seg_sums([yc[i] * yc[i] for i in idx])]
        for i in idx:
            yn = yc[i] * lax.rsqrt(var[i] + RW_GN_EPS) * lg_ref[:, ls[i]] + lb_ref[:, ls[i]]
            y_ref[:, ls[i]] = ((yn + bonus[i]) * g_ref[:, ls[i]]).astype(y_ref.dtype)

    for p0 in range(0, npairs, unroll):
        wave(list(range(p0, p0 + unroll)))

    @pl.when(ci == pl.num_programs(1) - 1)
    def _():
        for p in range(npairs):
            blk = s2_ref[p]
            sT_ref[2 * p] = blk[:nh, :nh]
            sT_ref[2 * p + 1] = blk[nh:, nh:]


def _rwkv_chunked(arrs, lows, lora_w, lora_b, params, row0, bsz, t):
    m, d = arrs[0].shape
    lanes = V7X_LANES
    chunk = min(t, 64)
    npairs = d // lanes
    assert t % chunk == 0 and row0 % chunk == 0 and d % lanes == 0 and lanes == 2 * RW_HEAD
    nc = t // chunk
    rb0 = row0 // chunk
    rowblk = lambda b, c: (rb0 + b * nc + c, 0)
    seq = pl.BlockSpec((chunk, d), rowblk)
    vec = pl.BlockSpec((1, d), lambda b, c: (0, 0))
    whole = lambda a: pl.BlockSpec(a.shape, lambda b, c: (0, 0))
    est = 4 * 2 * _nbytes((chunk, d), F32) + 3 * _nbytes((chunk, d), F32) + 3 * _nbytes((npairs, lanes, lanes), F32) \
        + 2 * sum(_nbytes(w.shape, BF16) for w in lora_w) + (8 << 20)
    kern = functools.partial(_rwkv_chunk_kernel, chunk=chunk, npairs=npairs, unroll=16)
    return pl.pallas_call(
        kern,
        grid=(bsz, nc),
        in_specs=[seq] * 3 + [pl.BlockSpec((chunk, lo.shape[1]), rowblk) for lo in lows] + [whole(w) for w in lora_w]
        + [vec] * 7,
        out_specs=[seq, pl.BlockSpec((None, 2 * npairs, RW_HEAD, RW_HEAD), lambda b, c: (b, 0, 0, 0))],
        out_shape=[jax.ShapeDtypeStruct((m, d), BF16),
                   jax.ShapeDtypeStruct((bsz, 2 * npairs, RW_HEAD, RW_HEAD), F32)],
        scratch_shapes=[pltpu.VMEM((npairs, lanes, lanes), F32)] + [pltpu.VMEM((chunk, d), F32)] * 3,
        compiler_params=_params(("parallel", "arbitrary"), est),
        name="rwkv_chunk",
    )(*arrs, *lows, *lora_w, *lora_b, *[_row(q) for q in params])


def _rwkv_layer(x, groups, states, p, wb):
    d = x.shape[1]
    h, (xr, xw, xk, xv, xa, xg) = _rw_mix(x, p['norm_g'], p['rw_mu'], states[1][1], groups)
    w_rkv = wb['rw_w_rkv']
    r = _mm([_full(xr)], [(w_rkv, (0,), 0)], n=d, out_dtype=F32, name="rw_r")
    k = _mm([_full(xk)], [(w_rkv, (1,), 0)], n=d, out_dtype=F32, name="rw_k")
    v = _mm([_full(xv)], [(w_rkv, (2,), 0)], n=d, out_dtype=F32, name="rw_v")
    lo_w = _mm([_full(xw)], [(wb['rw_w1'], (), 0)], n=wb['rw_w1'].shape[1], out_dtype=BF16,
               epilogue=lambda accs, e, ev: jnp.tanh(accs[0]), name="rw_w1")
    lo_a = _mm([_full(xa)], [(wb['rw_a1'], (), 0)], n=wb['rw_a1'].shape[1], out_dtype=BF16, name="rw_a1")
    lo_g = _mm([_full(xg)], [(wb['rw_g1'], (), 0)], n=wb['rw_g1'].shape[1], out_dtype=BF16,
               epilogue=lambda accs, e, ev: jax.nn.sigmoid(accs[0]), name="rw_g1")
    lora_w = (wb['rw_w2'], wb['rw_a2'], wb['rw_g2'])
    lora_b = (_row(p['rw_w0']), _row(p['rw_a0']))
    add_vec = lambda accs, e, ev: ev[0] + accs[0]

    y, new_states = None, []
    for (r0, b, t), st in zip(groups, states):
        sl = slice(r0, r0 + b * t)
        if st[0] is None:
            assert y is None
            y, s_fin = _rwkv_chunked((r, k, v), (lo_w, lo_a, lo_g), lora_w, lora_b,
                                     [p[name] for name in ('rw_k_k', 'rw_k_a', 'rw_r_k', 'rw_ln_g', 'rw_ln_b')], r0, b, t)
        else:
            w_pre = _mm([_full(lo_w[sl])], [(lora_w[0], (), 0)], n=d, out_dtype=F32, evecs=[lora_b[0]],
                        epilogue=add_vec, name="rw_w2")
            a_pre = _mm([_full(lo_a[sl])], [(lora_w[1], (), 0)], n=d, out_dtype=F32, evecs=[lora_b[1]],
                        epilogue=add_vec, name="rw_a2")
            g = _mm([_full(lo_g[sl])], [(lora_w[2], (), 0)], n=d, out_dtype=F32, name="rw_g2")
            y_g, s_fin = _rwkv_group(r[sl], k[sl], v[sl], w_pre, a_pre, st[0], p, b, t)
            y = lax.dynamic_update_slice(y, (y_g * g).astype(y.dtype), (r0, 0))
        new_states.append((s_fin, lax.slice(h, (r0 + t - 1, 0), (r0 + b * t, d), (t, 1))))
    x = _mm([_full(y)], [(wb['rw_w_o'], (), 0)], n=d, out_dtype=F32, e_blocks=[x],
            epilogue=lambda accs, e, ev: e[0] + accs[0], name="rw_out")
    return x, new_states


def _mlstm_kernel(q_ref, k_ref, v_ref, og_ref, gt_ref, gtt_ref, bg_ref, bgt_ref, ng_ref, c0_ref, n0_ref, m0_ref, *rest,
                  nheads, dk, dv, chunk, bb, aliased):
    if aliased:
        rest = rest[1:]
    h_ref, cT_ref, nT_ref, mT_ref, c_ref, n_ref, m_ref = rest
    ci = pl.program_id(1)

    @pl.when(ci == 0)
    def _():
        c_ref[...] = c0_ref[...]
        n_ref[...] = n0_ref[...]
        m_ref[...] = m0_ref[...]

    L = chunk
    cap = lambda z: ML_GATE_CAP * jnp.tanh(z / ML_GATE_CAP)
    gates_c_all = cap(gt_ref[...] + bg_ref[...])
    gates_r_all = cap(gtt_ref[...] + bgt_ref[...])
    row_id = lax.broadcasted_iota(jnp.int32, (L, L), 0)
    col_id = lax.broadcasted_iota(jnp.int32, (L, L), 1)
    causal = col_id <= row_id
    k_scale = dk ** -0.5

    nt = (((1,), (1,)), ((), ()))
    tn = (((0,), (0,)), ((), ()))
    chains = [(bi, hd) for bi in range(bb) for hd in range(nheads)]
    idx = range(len(chains))
    rows = [slice(bi * L, (bi + 1) * L) for bi, _ in chains]
    li_c, b_c, m_prev, dmat, inter, m_t = [], [], [], [], [], []
    for (bi, hd), rw in zip(chains, rows):
        gates_c = gates_c_all[rw, :]
        gates_r = gates_r_all[:, rw]
        li_r = gates_r[hd:hd + 1, :]
        lf_c = jax.nn.log_sigmoid(gates_c[:, nheads + hd:nheads + hd + 1])
        lf_r = jax.nn.log_sigmoid(gates_r[nheads + hd:nheads + hd + 1, :])
        bc = jnp.sum(jnp.where(causal, lf_r, 0.0), axis=1, keepdims=True)
        br = jnp.sum(jnp.where(row_id <= col_id, lf_c, 0.0), axis=0, keepdims=True)
        mp = m_ref[bi, hd][:, :1]
        dm = jnp.where(causal, bc - br + li_r, -jnp.inf)
        it = bc + mp
        li_c.append(gates_c[:, hd:hd + 1])
        b_c.append(bc)
        m_prev.append(mp)
        dmat.append(dm)
        inter.append(it)
        m_t.append(jnp.maximum(jnp.max(dm, axis=1, keepdims=True), it))

    q = [q_ref[rw, hd * dk:(hd + 1) * dk] for (_, hd), rw in zip(chains, rows)]
    k = [k_ref[rw, hd * dk:(hd + 1) * dk] * k_scale for (_, hd), rw in zip(chains, rows)]
    v = [v_ref[rw, hd * dv:(hd + 1) * dv] for (_, hd), rw in zip(chains, rows)]
    qb = [z.astype(BF16) for z in q]
    kb = [z.astype(BF16) for z in k]
    c_old = [c_ref[bi, hd] for bi, hd in chains]
    n_old = [n_ref[bi, hd] for bi, hd in chains]
    qk = [lax.dot_general(qb[i], kb[i], nt, preferred_element_type=F32) for i in idx]
    cq = [lax.dot_general(qb[i], c_old[i].astype(BF16), nt, preferred_element_type=F32) for i in idx]
    s = [qk[i] * jnp.exp(dmat[i] - m_t[i]) for i in idx]
    sv = [jnp.dot(s[i].astype(BF16), v[i].astype(BF16), preferred_element_type=F32) for i in idx]
    m_new, ws = [], []
    for i in idx:
        b_last = b_c[i][L - 1:L, :]
        ws_log = b_last - b_c[i] + li_c[i]
        mn = jnp.maximum(b_last + m_prev[i], jnp.max(ws_log, axis=0, keepdims=True))
        m_new.append(mn)
        ws.append(jnp.exp(ws_log - mn))
    upd = [lax.dot_general((v[i] * ws[i]).astype(BF16), kb[i], tn, preferred_element_type=F32) for i in idx]
    h_out = [[None] * bb for _ in range(nheads)]
    for i, ((bi, hd), rw) in enumerate(zip(chains, rows)):
        w_inter = jnp.exp(inter[i] - m_t[i])
        num = sv[i] + w_inter * cq[i]
        nq = jnp.sum(q[i] * n_old[i], axis=1, keepdims=True)
        den = jnp.sum(s[i], axis=1, keepdims=True) + w_inter * nq
        h = num / jnp.maximum(jnp.abs(den), jnp.exp(-m_t[i]))
        hn = h * lax.rsqrt(jnp.mean(h * h, axis=1, keepdims=True) + ML_NORM_EPS)
        h_out[hd][bi] = hn * ng_ref[:, hd * dv:(hd + 1) * dv]
        dec = jnp.exp(b_c[i][L - 1:L, :] + m_prev[i] - m_new[i])
        c_ref[bi, hd] = dec * c_old[i] + upd[i]
        n_ref[bi, hd] = dec * n_old[i] + jnp.sum(ws[i] * k[i], axis=0, keepdims=True)
        m_ref[bi, hd] = jnp.broadcast_to(m_new[i], m_ref.shape[2:])
    for hd in range(nheads):
        cols = slice(hd * dv, (hd + 1) * dv)
        h_all = h_out[hd][0] if bb == 1 else jnp.concatenate(h_out[hd], axis=0)
        h_ref[:, cols] = (jax.nn.sigmoid(og_ref[:, cols]) * h_all).astype(h_ref.dtype)

    @pl.when(ci == pl.num_programs(1) - 1)
    def _():
        cT_ref[...] = c_ref[...]
        nT_ref[...] = n_ref[...]
        mT_ref[...] = m_ref[...]


def _mlstm_group(qkvo, gates, b_gates, norm_g, c0, n0, m0, group, h_prev_out):
    r0, bsz, t = group
    m_rows = qkvo.shape[0]
    nheads = ML_HEADS
    dk = c0.shape[-1]
    dv = c0.shape[-2]
    lanes = V7X_LANES
    chunk = min(t, lanes)
    bb = 1 if chunk >= V7X_SUBLANES else 2 * V7X_SUBLANES // chunk
    rows = bb * chunk
    assert t % chunk == 0 and bsz % bb == 0 and r0 % rows == 0
    nc = t // chunk
    nsteps = (bsz // bb) * nc
    rb0 = r0 // rows
    g_grp = gates[r0:r0 + bsz * t]
    g3t = g_grp.reshape(nsteps, rows, 2 * nheads).transpose(0, 2, 1)
    n0r = n0.astype(F32).reshape(bsz, nheads, 1, dk)
    m0r = jnp.broadcast_to(m0.astype(F32).reshape(bsz, nheads, 1, 1), (bsz, nheads, 1, lanes))
    wq, wv = nheads * dk, nheads * dv
    rowblk = lambda b, c: rb0 + b * nc + c
    seq = lambda w, cb: pl.BlockSpec((rows, w), functools.partial(lambda b, c, cb_: (rowblk(b, c), cb_), cb_=cb))
    st_c = pl.BlockSpec((bb, nheads, dv, dk), lambda b, c: (b, 0, 0, 0))
    st_n = pl.BlockSpec((bb, nheads, 1, dk), lambda b, c: (b, 0, 0, 0))
    st_m = pl.BlockSpec((bb, nheads, 1, lanes), lambda b, c: (b, 0, 0, 0))
    est = 2 * 3 * _nbytes((rows, 2 * wq + 2 * wv), F32) + 5 * bb * _nbytes((nheads, dv, dk), F32) + (8 << 20)
    aliased = h_prev_out is not None
    kern = functools.partial(_mlstm_kernel, nheads=nheads, dk=dk, dv=dv, chunk=chunk, bb=bb, aliased=aliased)
    in_specs = [seq(wq, 0), seq(wq, 1), seq(wv, wq * 2 // wv), seq(wv, wq * 2 // wv + 1),
                pl.BlockSpec((rows, 2 * nheads), lambda b, c: (rowblk(b, c), 0)),
                pl.BlockSpec((None, 2 * nheads, rows), lambda b, c: (b * nc + c, 0, 0)),
                pl.BlockSpec((1, 2 * nheads), lambda b, c: (0, 0)),
                pl.BlockSpec((2 * nheads, 1), lambda b, c: (0, 0)),
                pl.BlockSpec((1, wv), lambda b, c: (0, 0)),
                st_c, st_n, st_m]
    args = [qkvo, qkvo, qkvo, qkvo, gates, g3t, _row(b_gates), b_gates.reshape(-1, 1), _row(norm_g),
            c0.astype(F32), n0r, m0r]
    io_alias = {}
    if aliased:
        in_specs.append(pl.BlockSpec(memory_space=pl.ANY))
        args.append(h_prev_out)
        io_alias = {len(args) - 1: 0}
    h, c_fin, n_fin, m_fin = pl.pallas_call(
        kern,
        grid=(bsz // bb, nc),
        in_specs=in_specs,
        out_specs=[pl.BlockSpec((rows, wv), lambda b, c: (rowblk(b, c), 0)), st_c, st_n, st_m],
        out_shape=[jax.ShapeDtypeStruct((m_rows, wv), BF16),
                   jax.ShapeDtypeStruct((bsz, nheads, dv, dk), F32),
                   jax.ShapeDtypeStruct((bsz, nheads, 1, dk), F32),
                   jax.ShapeDtypeStruct((bsz, nheads, 1, lanes), F32)],
        scratch_shapes=[pltpu.VMEM((bb, nheads, dv, dk), F32), pltpu.VMEM((bb, nheads, 1, dk), F32),
                        pltpu.VMEM((bb, nheads, 1, lanes), F32)],
        input_output_aliases=io_alias,
        compiler_params=_params(("parallel", "arbitrary"), est),
        name="mlstm_chunk",
    )(*args)
    return h, c_fin, n_fin.reshape(bsz, nheads, dk), m_fin[:, :, 0, 0]


def _mlstm_layer(x, groups, states, p, wb):
    d = x.shape[1]
    nheads = ML_HEADS
    dv = d // nheads
    dk = dv // 2
    n_main = 2 * nheads * dk + 2 * nheads * dv
    xn = _rmsnorm(x, p['norm_g'], BF16)
    qkvo = _mm([_full(xn)], [(wb['ml_w_main'], (), 0)], n=n_main, out_dtype=F32, name="ml_in")
    gates = _mm([_full(xn)], [(wb['ml_w_gates'], (), 0)], n=2 * nheads, out_dtype=F32, name="ml_gates")
    h, new_states = None, []
    for grp, st in zip(groups, states):
        h, c_fin, n_fin, m_fin = _mlstm_group(qkvo, gates, p['ml_b_gates'], p['ml_norm_g'], st[0], st[1], st[2], grp, h)
        new_states.append((c_fin, n_fin, m_fin))
    x = _mm([_full(h)], [(wb['ml_w_o'], (), 0)], n=d, out_dtype=F32,
            e_blocks=[x], epilogue=lambda accs, e, ev: e[0] + accs[0], name="ml_out")
    return x, new_states


def _dwconv_kernel(u_ref, buf_ref, w_ref, *rest, ksize, t, bb, rows, halo, aliased):
    if aliased:
        rest = rest[1:]
    y_ref, ext_ref = rest
    w = w_ref[...]
    sub = V7X_SUBLANES
    lead = halo - (ksize - 1)
    n_tiles = t // rows
    taps_by_shift = [[j for j in range(ksize) if (j + lead) % sub == r] for r in range(sub)]
    ext_ref[pl.ds(halo + t, sub), :] = jnp.zeros((sub, ext_ref.shape[1]), F32)

    for b in range(bb):
        ext_ref[pl.ds(lead, ksize - 1), :] = buf_ref[b]
        ext_ref[pl.ds(halo, t), :] = u_ref[pl.ds(b * t, t), :]

        def tile(ti, carry):
            base = ti * rows if isinstance(ti, int) else pl.multiple_of(ti * rows, sub)
            acc = None
            for r in range(sub):
                part = None
                for j in taps_by_shift[r]:
                    q8 = (j + lead) - r
                    term = w[j:j + 1, :] * ext_ref[pl.ds(base + q8, rows + sub), :]
                    part = term if part is None else part + term
                if part is not None:
                    shifted = part[r:r + rows, :]
                    acc = shifted if acc is None else acc + shifted
            y_ref[pl.ds(b * t + base, rows), :] = acc
            return carry

        if n_tiles == 1:
            tile(0, 0)
        else:
            lax.fori_loop(0, n_tiles, tile, 0)


def _dwconv_group(u, buf, w, group, y_prev_out):
    r0, bsz, t = group
    m_rows, c = u.shape
    ksize = w.shape[0]
    sub = V7X_SUBLANES
    halo = _round_up(ksize - 1, sub)
    bc = 256
    bb = 1 if t >= 64 else min(bsz, 32)
    rows = min(t, 64)
    blk = bb * t
    assert bsz % bb == 0 and t % rows == 0 and c % bc == 0 and r0 % blk == 0 and blk % sub == 0
    rb0 = r0 // blk
    aliased = y_prev_out is not None
    est = 4 * _nbytes((blk, bc), F32) + _nbytes((halo + t + sub, bc), F32) + (4 << 20)
    kern = functools.partial(_dwconv_kernel, ksize=ksize, t=t, bb=bb, rows=rows, halo=halo, aliased=aliased)
    in_specs = [pl.BlockSpec((blk, bc), lambda b, ci: (rb0 + b, ci)),
                pl.BlockSpec((bb, ksize - 1, bc), lambda b, ci: (b, 0, ci)),
                pl.BlockSpec((ksize, bc), lambda b, ci: (0, ci))]
    args = [u, buf.astype(F32), w]
    io_alias = {}
    if aliased:
        in_specs.append(pl.BlockSpec(memory_space=pl.ANY))
        args.append(y_prev_out)
        io_alias = {3: 0}
    return pl.pallas_call(
        kern,
        grid=(bsz // bb, c // bc),
        in_specs=in_specs,
        out_specs=pl.BlockSpec((blk, bc), lambda b, ci: (rb0 + b, ci)),
        out_shape=jax.ShapeDtypeStruct((m_rows, c), F32),
        scratch_shapes=[pltpu.VMEM((halo + t + sub, bc), F32)],
        input_output_aliases=io_alias,
        compiler_params=_params(("parallel", "parallel"), est),
        name="dwconv",
    )(*args)


def _conv_groups(u, w_dw, groups, bufs):
    y, new_bufs = None, []
    c = u.shape[1]
    km1 = w_dw.shape[0] - 1
    for (r0, b, t), buf in zip(groups, bufs):
        y = _dwconv_group(u, buf, w_dw, (r0, b, t), y)
        if t >= km1:
            tail = r0 + t - km1 + jnp.arange(b)[:, None] * t + jnp.arange(km1)[None, :]
            new_bufs.append(u[tail])
        else:
            new_bufs.append(jnp.concatenate([buf.astype(F32)[:, t:], u[r0:r0 + b * t].reshape(b, t, c)], axis=1))
    return y, new_bufs


def _conformer_layer(x, groups, states, p, wb):
    d = x.shape[1]
    b1 = _row(p['cf_b_pw1'])
    w1 = wb['cf_w_pw1']
    u = _mm([_full(x)], [(w1, (), 0), (w1, (), d)], n=d, out_dtype=F32, avecs=[_row(p['norm_g'])],
            prologue=lambda a, v, pw: _rms(a[0]) * v[0], evecs=[b1[:, :d], b1[:, d:]],
            epilogue=lambda accs, e, ev: (accs[0] + ev[0]) * jax.nn.sigmoid(accs[1] + ev[1]), name="cf_pw1")
    y, new_bufs = _conv_groups(u, p['cf_w_dw'], groups, [st[0] for st in states])

    def ln_silu(a, v, pw):
        z = a[0] + v[0]
        mu = jnp.mean(z, axis=-1, keepdims=True)
        var = jnp.mean(jnp.square(z - mu), axis=-1, keepdims=True)
        z = (z - mu) * lax.rsqrt(var + CF_LN_EPS) * v[1] + v[2]
        return z * jax.nn.sigmoid(z)

    x = _mm([_full(y)], [(wb['cf_w_pw2'], (), 0)], n=d, out_dtype=F32, prologue=ln_silu,
            avecs=[_row(p['cf_b_dw']), _row(p['cf_ln_g']), _row(p['cf_ln_b'])],
            e_blocks=[x], evecs=[_row(p['cf_b_pw2'])],
            epilogue=lambda accs, e, ev: e[0] + (accs[0] + ev[0]), name="cf_pw2")
    return x, [(nb,) for nb in new_bufs]


def _sconv_layer(x, groups, states, p, wb):
    d = x.shape[1]
    xn = _rmsnorm(x, p['norm_g'], BF16)
    w_in = wb['sc_w_in']
    b_gate = _mm([_full(xn)], [(w_in, (), 0)], n=d, out_dtype=F32, name="sc_b")
    ch = _mm([_full(xn)], [(w_in, (), d), (w_in, (), 2 * d)], n=d, out_dtype=F32,
             epilogue=lambda accs, e, ev: accs[0] * accs[1], name="sc_ch")
    y, new_bufs = _conv_groups(ch, p['sc_w_dw'], groups, [st[0] for st in states])
    x = _mm([_full(b_gate), _full(y)], [(wb['sc_w_out'], (), 0)], n=d, out_dtype=F32,
            prologue=lambda a, v, pw: a[0] * a[1], e_blocks=[x],
            epilogue=lambda accs, e, ev: e[0] + accs[0], name="sc_out")
    return x, [(nb,) for nb in new_bufs]


def kernel(x_prompt, x_sample, state_rwkv_wkv, state_rwkv_shift, state_mlstm_C, state_mlstm_n, state_mlstm_m,
           state_conf_conv, state_sconv_conv, norm_g, ffn_w_gu, ffn_w_down, final_norm_g,
           rw_mu, rw_w_rkv, rw_w0, rw_w1, rw_w2, rw_a0, rw_a1, rw_a2, rw_g1, rw_g2, rw_k_k, rw_k_a, rw_r_k,
           rw_ln_g, rw_ln_b, rw_w_o, ml_w_in, ml_b_gates, ml_norm_g, ml_w_o,
           cf_w_pw1, cf_b_pw1, cf_w_dw, cf_b_dw, cf_ln_g, cf_ln_b, cf_w_pw2, cf_b_pw2,
           sc_w_in, sc_w_dw, sc_w_out):
    bp, tp, d = x_prompt.shape
    bs, ts, _ = x_sample.shape
    depth = norm_g.shape[0]
    n_mixers = 4
    groups = [(0, bp, tp), (bp * tp, bs, ts)]
    x = jnp.concatenate([x_prompt.reshape(bp * tp, d), x_sample.reshape(bs * ts, d)], axis=0).astype(F32)

    dv = d // ML_HEADS
    dk = dv // 2
    cf_k = cf_w_dw.shape[1]
    sc_k = sc_w_dw.shape[1]
    zeros = lambda *s: jnp.zeros(s, F32)

    out_states = {name: ([], []) for name in ('wkv', 'shift', 'C', 'n', 'm', 'cf', 'sc')}
    for i in range(depth):
        j = i // n_mixers
        kind = i % n_mixers
        x = _ffn(x, norm_g[i, 0], ffn_w_gu, ffn_w_down, (i, 0))
        if kind == 0:
            p = dict(norm_g=norm_g[i, 1], rw_mu=rw_mu[j], rw_w0=rw_w0[j], rw_a0=rw_a0[j], rw_k_k=rw_k_k[j],
                     rw_k_a=rw_k_a[j], rw_r_k=rw_r_k[j], rw_ln_g=rw_ln_g[j], rw_ln_b=rw_ln_b[j])
            wb = dict(rw_w_rkv=rw_w_rkv[j].astype(BF16), rw_w1=rw_w1[j].astype(BF16), rw_w2=rw_w2[j].astype(BF16),
                      rw_a1=rw_a1[j].astype(BF16), rw_a2=rw_a2[j].astype(BF16), rw_g1=rw_g1[j].astype(BF16),
                      rw_g2=rw_g2[j].astype(BF16), rw_w_o=rw_w_o[j].astype(BF16))
            states = [(None, zeros(bp, d)), (state_rwkv_wkv[j], state_rwkv_shift[j].astype(F32))]
            x, new = _rwkv_layer(x, groups, states, p, wb)
            for gi in range(2):
                out_states['wkv'][gi].append(new[gi][0])
                out_states['shift'][gi].append(new[gi][1])
        elif kind == 1:
            p = dict(norm_g=norm_g[i, 1], ml_b_gates=ml_b_gates[j], ml_norm_g=ml_norm_g[j])
            n_main = 2 * ML_HEADS * dk + 2 * ML_HEADS * dv
            wb = dict(ml_w_main=ml_w_in[j][:, :n_main].astype(BF16), ml_w_gates=ml_w_in[j][:, n_main:].astype(BF16),
                      ml_w_o=ml_w_o[j].astype(BF16))
            states = [(zeros(bp, ML_HEADS, dv, dk), zeros(bp, ML_HEADS, dk), zeros(bp, ML_HEADS)),
                      (state_mlstm_C[j], state_mlstm_n[j], state_mlstm_m[j])]
            x, new = _mlstm_layer(x, groups, states, p, wb)
            for gi in range(2):
                out_states['C'][gi].append(new[gi][0])
                out_states['n'][gi].append(new[gi][1])
                out_states['m'][gi].append(new[gi][2])
        elif kind == 2:
            p = dict(norm_g=norm_g[i, 1], cf_b_pw1=cf_b_pw1[j], cf_w_dw=cf_w_dw[j], cf_b_dw=cf_b_dw[j],
                     cf_ln_g=cf_ln_g[j], cf_ln_b=cf_ln_b[j], cf_b_pw2=cf_b_pw2[j])
            wb = dict(cf_w_pw1=cf_w_pw1[j].astype(BF16), cf_w_pw2=cf_w_pw2[j].astype(BF16))
            states = [(zeros(bp, cf_k - 1, d),), (state_conf_conv[j],)]
            x, new = _conformer_layer(x, groups, states, p, wb)
            for gi in range(2):
                out_states['cf'][gi].append(new[gi][0])
        else:
            p = dict(norm_g=norm_g[i, 1], sc_w_dw=sc_w_dw[j])
            wb = dict(sc_w_in=sc_w_in[j].astype(BF16), sc_w_out=sc_w_out[j].astype(BF16))
            states = [(zeros(bp, sc_k - 1, d),), (state_sconv_conv[j],)]
            x, new = _sconv_layer(x, groups, states, p, wb)
            for gi in range(2):
                out_states['sc'][gi].append(new[gi][0])
        x = _ffn(x, norm_g[i, 2], ffn_w_gu, ffn_w_down, (i, 1))

    y_prompt = _rmsnorm(x, final_norm_g, x_prompt.dtype, 0, bp * tp).reshape(bp, tp, d)
    y_sample = _rmsnorm(x, final_norm_g, x_sample.dtype, bp * tp, bs * ts).reshape(bs, ts, d)
    order = ('wkv', 'shift', 'C', 'n', 'm', 'cf', 'sc')
    dt = x_prompt.dtype
    prompt_out = tuple(jnp.stack(out_states[name][0]).astype(dt) for name in order)
    sample_out = tuple(jnp.stack(out_states[name][1]).astype(st.dtype) for name, st in
                       zip(order, (state_rwkv_wkv, state_rwkv_shift, state_mlstm_C, state_mlstm_n, state_mlstm_m,
                                   state_conf_conv, state_sconv_conv)))
    return (y_prompt, y_sample) + prompt_out + sample_out
```

```python
import functools
import math

import jax
import jax.numpy as jnp
from jax import lax
from jax.experimental import pallas as pl
from jax.experimental.pallas import tpu as pltpu

F32 = jnp.float32
BF16 = jnp.bfloat16

NORM_EPS = 1e-6
RW_HEAD = 64
RW_GN_EPS = 64e-5
ML_HEADS = 8
ML_GATE_CAP = 15.0
ML_NORM_EPS = 1e-6
CF_LN_EPS = 1e-5

V7X_LANES = 128
V7X_SUBLANES = 8
V7X_VMEM_BYTES = 64 * 1024 * 1024
VMEM_LIMIT_CAP = V7X_VMEM_BYTES - 6 * 1024 * 1024


def _nbytes(shape, dtype):
    return math.prod(shape) * jnp.dtype(dtype).itemsize


def _params(sem, vmem_est):
    limit = int(min(max(vmem_est * 1.2 + (2 << 20), 16 << 20), VMEM_LIMIT_CAP))
    return pltpu.CompilerParams(dimension_semantics=sem, vmem_limit_bytes=limit)


def _pick_bm(m, target, also_divides=()):
    best = None
    for bm in range(16, min(m, target) + 1, 16):
        if m % bm == 0 and all(o % bm == 0 for o in also_divides):
            best = bm
    assert best is not None
    return best


def _rms(x):
    return x * lax.rsqrt(jnp.mean(x * x, axis=-1, keepdims=True) + NORM_EPS)


def _round_up(x, m):
    return -(-x // m) * m


def _mm_kernel(*refs, counts, prologue, epilogue, use_scratch, has_side):
    n_a, n_av, n_pw, n_w, n_e, n_ev = counts
    pos = 0
    groups = []
    for c in counts:
        groups.append(refs[pos:pos + c])
        pos += c
    a_refs, av_refs, pw_refs, w_refs, e_refs, ev_refs = groups
    if has_side:
        side_in, o_ref, side_out = refs[pos:pos + 3]
        side_out[...] = side_in[...].astype(side_out.dtype)
    else:
        o_ref = refs[pos]
    if use_scratch:
        lhs_ref = refs[-1]

        @pl.when(pl.program_id(1) == 0)
        def _():
            lhs_ref[...] = prologue([r[...] for r in a_refs], [r[...] for r in av_refs],
                                    [r[...] for r in pw_refs]).astype(BF16)

        lhs = lhs_ref[...]
    else:
        lhs = a_refs[0][...]
    accs = [jnp.dot(lhs, w[...].astype(BF16), preferred_element_type=F32) for w in w_refs]
    out = epilogue(accs, [r[...] for r in e_refs], [r[...] for r in ev_refs])
    o_ref[...] = out.astype(o_ref.dtype)


def _mm(a_list, w_list, *, n, out_dtype, prologue=None, epilogue=None, avecs=(), pw_list=(), e_blocks=(), evecs=(),
        side_cast=None, bm_target=1088, bn=512, name="mm"):
    m = a_list[0][0].shape[0]
    k = w_list[0][0].shape[-2]
    bm = _pick_bm(m, bm_target)
    bn = min(bn, n)
    assert n % bn == 0
    use_scratch = prologue is not None or a_list[0][0].dtype != BF16
    if prologue is None:
        prologue = lambda a, v, pw: a[0]
    if epilogue is None:
        epilogue = lambda accs, e, ev: accs[0]

    in_specs, args, est = [], [], 0
    for arr, ka, cb in a_list:
        in_specs.append(pl.BlockSpec((bm, ka), functools.partial(lambda i, j, c: (i, c), c=cb)))
        args.append(arr)
        est += 2 * _nbytes((bm, ka), arr.dtype)
    for v in avecs:
        in_specs.append(pl.BlockSpec(v.shape, lambda i, j: (0, 0)))
        args.append(v)
    for pw in pw_list:
        in_specs.append(pl.BlockSpec(pw.shape, lambda i, j: (0, 0)))
        args.append(pw)
        est += 2 * _nbytes(pw.shape, pw.dtype)
    for arr, lead, off in w_list:
        assert off % bn == 0 and arr.shape[-2] == k
        in_specs.append(pl.BlockSpec((None,) * len(lead) + (k, bn),
                                     functools.partial(lambda i, j, l, o: l + (0, j + o), l=tuple(lead), o=off // bn)))
        args.append(arr)
        est += 2 * _nbytes((k, bn), arr.dtype) + (_nbytes((k, bn), BF16) if arr.dtype != BF16 else 0)
    for e in e_blocks:
        in_specs.append(pl.BlockSpec((bm, bn), lambda i, j: (i, j)))
        args.append(e)
        est += 2 * _nbytes((bm, bn), e.dtype)
    for v in evecs:
        in_specs.append(pl.BlockSpec((1, bn), lambda i, j: (0, j)))
        args.append(v)
    est += 2 * _nbytes((bm, bn), out_dtype) + (len(w_list) + 2) * _nbytes((bm, bn), F32)
    scratch = []
    if use_scratch:
        scratch.append(pltpu.VMEM((bm, k), BF16))
        est += _nbytes((bm, k), BF16) + 2 * _nbytes((bm, a_list[0][1]), F32)

    grid = (m // bm, n // bn)
    out_specs = pl.BlockSpec((bm, bn), lambda i, j: (i, j))
    out_shape = jax.ShapeDtypeStruct((m, n), out_dtype)
    if side_cast is not None:
        s_arr, s_lead = side_cast
        s_rows, s_cols = s_arr.shape[-2:]
        steps = grid[0] * grid[1]
        assert s_rows % (steps * 2 * V7X_SUBLANES) == 0
        rps = s_rows // steps
        nj = grid[1]
        in_specs.append(pl.BlockSpec((None,) * len(s_lead) + (rps, s_cols),
                                     functools.partial(lambda i, j, l: l + (i * nj + j, 0), l=tuple(s_lead))))
        args.append(s_arr)
        out_specs = [out_specs, pl.BlockSpec((rps, s_cols), lambda i, j: (i * nj + j, 0))]
        out_shape = [out_shape, jax.ShapeDtypeStruct((s_rows, s_cols), BF16)]
        est += 2 * _nbytes((rps, s_cols), F32) + 2 * _nbytes((rps, s_cols), BF16)

    counts = (len(a_list), len(avecs), len(pw_list), len(w_list), len(e_blocks), len(evecs))
    kern = functools.partial(_mm_kernel, counts=counts, prologue=prologue, epilogue=epilogue, use_scratch=use_scratch,
                             has_side=side_cast is not None)
    return pl.pallas_call(
        kern,
        grid=grid,
        in_specs=in_specs,
        out_specs=out_specs,
        out_shape=out_shape,
        scratch_shapes=scratch,
        compiler_params=_params(("parallel", "arbitrary"), est),
        name=name,
    )(*args)


def _full(arr):
    return (arr, arr.shape[1], 0)


def _row(v):
    return v.reshape(1, -1)


def _rmsnorm_kernel(x_ref, g_ref, o_ref):
    o_ref[...] = (_rms(x_ref[...]) * g_ref[...]).astype(o_ref.dtype)


def _rmsnorm(x, g, out_dtype, row0=0, nrows=None):
    d = x.shape[1]
    m = x.shape[0] if nrows is None else nrows
    bm = _pick_bm(m, 1088, also_divides=(row0,) if row0 else ())
    rb0 = row0 // bm
    est = 2 * _nbytes((bm, d), F32) * 3
    return pl.pallas_call(
        _rmsnorm_kernel,
        grid=(m // bm,),
        in_specs=[pl.BlockSpec((bm, d), lambda i: (rb0 + i, 0)), pl.BlockSpec((1, d), lambda i: (0, 0))],
        out_specs=pl.BlockSpec((bm, d), lambda i: (i, 0)),
        out_shape=jax.ShapeDtypeStruct((m, d), out_dtype),
        compiler_params=_params(("parallel",), est),
        name="rmsnorm",
    )(x, _row(g))


def _ffn(x, g, w_gu, w_down, lead):
    d_ff = w_down.shape[-2]
    h, w_down_b = _mm([_full(x)], [(w_gu, lead, 0), (w_gu, lead, d_ff)], n=d_ff, out_dtype=BF16, avecs=[_row(g)],
                      prologue=lambda a, v, pw: _rms(a[0]) * v[0],
                      epilogue=lambda accs, e, ev: accs[0] * jax.nn.sigmoid(accs[0]) * accs[1],
                      side_cast=(w_down, lead), name="ffn_up")
    return _mm([_full(h)], [(w_down_b, (), 0)], n=x.shape[1], out_dtype=F32, e_blocks=[x],
               epilogue=lambda accs, e, ev: e[0] + 0.5 * accs[0], name="ffn_down")


def _rw_mix_kernel(x_ref, g_ref, mu_ref, ov_ref, h_ref, *rest, n_prompt_blocks, blocks_per_seq, ts):
    outs, carry_ref = rest[:-1], rest[-1]
    pid = pl.program_id(0)

    @pl.when(pid == 0)
    def _():
        carry_ref[...] = jnp.zeros_like(carry_ref)

    h = _rms(x_ref[...]) * g_ref[...]
    h_ref[...] = h
    bm = h.shape[0]
    rolled = pltpu.roll(h, 1, axis=0)
    row = lax.broadcasted_iota(jnp.int32, h.shape, 0)
    carry = jnp.where(pid % blocks_per_seq == 0, 0.0, carry_ref[...])
    prev_prompt = jnp.where(row == 0, carry, rolled)
    prev_sample = jnp.where(row % ts == 0, ov_ref[...], rolled)
    prev = jnp.where(pid >= n_prompt_blocks, prev_sample, prev_prompt)
    carry_ref[...] = h[bm - 1:bm, :]
    delta = prev - h
    for i, o in enumerate(outs):
        o[...] = (h + delta * mu_ref[i:i + 1, :]).astype(BF16)


def _rw_mix(x, g, mu, shift_sample, groups):
    m, d = x.shape
    (_, bp, tp), (r0s, bs, ts) = groups
    bm = _pick_bm(tp, 512, also_divides=(bs * ts,))
    assert bm % ts == 0 and r0s == bp * tp
    n_prompt_blocks = (bp * tp) // bm
    ov = jnp.zeros((bs, ts, d), F32).at[:, 0].set(shift_sample.astype(F32)).reshape(bs * ts, d)
    n_mix = mu.shape[0]
    blk = pl.BlockSpec((bm, d), lambda i: (i, 0))
    est = 2 * _nbytes((bm, d), F32) * 3 + 2 * n_mix * _nbytes((bm, d), BF16) + 6 * _nbytes((bm, d), F32)
    kern = functools.partial(_rw_mix_kernel, n_prompt_blocks=n_prompt_blocks, blocks_per_seq=tp // bm, ts=ts)
    outs = pl.pallas_call(
        kern,
        grid=(m // bm,),
        in_specs=[blk, pl.BlockSpec((1, d), lambda i: (0, 0)), pl.BlockSpec((n_mix, d), lambda i: (0, 0)),
                  pl.BlockSpec((bm, d), lambda i: (jnp.maximum(i - n_prompt_blocks, 0), 0))],
        out_specs=[blk] * (1 + n_mix),
        out_shape=[jax.ShapeDtypeStruct((m, d), F32)] + [jax.ShapeDtypeStruct((m, d), BF16)] * n_mix,
        scratch_shapes=[pltpu.VMEM((1, d), F32)],
        compiler_params=_params(("arbitrary",), est),
        name="rw_mix",
    )(x, _row(g), mu, ov)
    return outs[0], outs[1:]


def _rwkv_steps(r_ref, k_ref, v_ref, w_ref, a_ref, y_ref, param_refs, s_ref, tb, nh):
    kk_p, ka_p, rk_p, ln_g, ln_b = (q[...] for q in param_refs)
    vgrp = V7X_SUBLANES

    def step(t, carry):
        r = r_ref[t]
        k = k_ref[t]
        v = v_ref[t]
        logw = -jax.nn.softplus(-w_ref[t]) - 0.5
        d = jnp.exp(-jnp.exp(logw))
        a = jax.nn.sigmoid(a_ref[t])
        kk = k * kk_p
        kk = kk / jnp.maximum(jnp.sqrt(jnp.sum(kk * kk, axis=0, keepdims=True)), 1e-12)
        kmod = k * (1.0 + (a - 1.0) * ka_p)
        b = kk * a

        def vgroup(g, c2):
            base = pl.multiple_of(g * vgrp, vgrp)
            v8 = v_ref[t, pl.ds(base, vgrp), :]
            ys = []
            for j in range(vgrp):
                s_old = s_ref[base + j]
                sa = jnp.sum(s_old * kk, axis=0, keepdims=True)
                s_new = s_old * d - sa * b + v8[j:j + 1, :] * kmod
                s_ref[base + j] = s_new
                ys.append(jnp.sum(s_new * r, axis=0, keepdims=True))
            y_ref[t, pl.ds(base, vgrp), :] = jnp.concatenate(ys, axis=0)
            return c2

        lax.fori_loop(0, nh // vgrp, vgroup, 0)

        y = y_ref[t]
        mean = jnp.mean(y, axis=0, keepdims=True)
        var = jnp.mean(jnp.square(y - mean), axis=0, keepdims=True)
        yn = (y - mean) * lax.rsqrt(var + RW_GN_EPS) * ln_g + ln_b
        bonus = jnp.sum(r * kmod * rk_p, axis=0, keepdims=True) * v
        y_ref[t] = yn + bonus
        return carry

    lax.fori_loop(0, tb, step, 0)


def _rwkv_scan_kernel(r_ref, k_ref, v_ref, w_ref, a_ref, kk_p_ref, ka_p_ref, rk_p_ref, lg_ref, lb_ref, s0_ref,
                      y_ref, sT_ref, s_ref, *, tb, nh):
    tblk = pl.program_id(1)

    @pl.when(tblk == 0)
    def _():
        s_ref[...] = s0_ref[...]

    _rwkv_steps(r_ref, k_ref, v_ref, w_ref, a_ref, y_ref, (kk_p_ref, ka_p_ref, rk_p_ref, lg_ref, lb_ref), s_ref, tb, nh)

    @pl.when(tblk == pl.num_programs(1) - 1)
    def _():
        sT_ref[...] = s_ref[...]


def _rwkv_scan(r, k, v, w_pre, a_pre, kk_p, ka_p, rk_p, ln_g, ln_b, s0, *, tb):
    t, nh, c = r.shape
    lanes = V7X_LANES
    assert c % lanes == 0 and t % tb == 0
    seq = pl.BlockSpec((tb, nh, lanes), lambda ci, ti: (ti, 0, ci))
    par = pl.BlockSpec((nh, lanes), lambda ci, ti: (0, ci))
    st = pl.BlockSpec((nh, nh, lanes), lambda ci, ti: (0, 0, ci))
    est = 6 * 2 * _nbytes((tb, nh, lanes), F32) + 5 * _nbytes((nh, nh, lanes), F32)
    kern = functools.partial(_rwkv_scan_kernel, tb=tb, nh=nh)
    return pl.pallas_call(
        kern,
        grid=(c // lanes, t // tb),
        in_specs=[seq] * 5 + [par] * 5 + [st],
        out_specs=[seq, st],
        out_shape=[jax.ShapeDtypeStruct((t, nh, c), F32), jax.ShapeDtypeStruct((nh, nh, c), F32)],
        scratch_shapes=[pltpu.VMEM((nh, nh, lanes), F32)],
        compiler_params=_params(("parallel", "arbitrary"), est),
        name="rwkv_scan",
    )(r, k, v, w_pre, a_pre, kk_p, ka_p, rk_p, ln_g, ln_b, s0)


def _to_chains(x, bsz, t, nheads):
    return x.reshape(bsz, t, nheads, RW_HEAD).transpose(1, 3, 0, 2).reshape(t, RW_HEAD, bsz * nheads)


def _from_chains(y, bsz, t, nheads):
    return y.reshape(t, RW_HEAD, bsz, nheads).transpose(2, 0, 3, 1).reshape(bsz * t, nheads * RW_HEAD)


def _rwkv_group(r, k, v, w_pre, a_pre, wkv0, p, bsz, t):
    d_model = r.shape[1]
    nheads = d_model // RW_HEAD
    c = bsz * nheads
    chain_param = lambda q: jnp.tile(q.reshape(nheads, RW_HEAD).T, (1, bsz))
    if wkv0 is None:
        s0 = jnp.zeros((RW_HEAD, RW_HEAD, c), F32)
    else:
        s0 = wkv0.astype(F32).transpose(2, 3, 0, 1).reshape(RW_HEAD, RW_HEAD, c)
    tb = t if t <= 32 else 32
    y, s_fin = _rwkv_scan(*[_to_chains(q, bsz, t, nheads) for q in (r, k, v, w_pre, a_pre)],
                          *[chain_param(p[name]) for name in ('rw_k_k', 'rw_k_a', 'rw_r_k', 'rw_ln_g', 'rw_ln_b')],
                          s0, tb=tb)
    s_fin = s_fin.reshape(RW_HEAD, RW_HEAD, bsz, nheads).transpose(2, 3, 0, 1)
    return _from_chains(y, bsz, t, nheads), s_fin


def _rwkv_chunk_kernel(r_ref, k_ref, v_ref, low_ref, loa_ref, log_ref, w2_ref, a2_ref, g2_ref, w0_ref, a0_ref,
                       kkp_ref, kap_ref, rkp_ref, lg_ref, lb_ref,
                       y_ref, sT_ref, s2_ref, w_ref, a_ref, g_ref, *, chunk, npairs, unroll):
    L = chunk
    lanes = V7X_LANES
    nh = RW_HEAD
    ci = pl.program_id(1)

    @pl.when(ci == 0)
    def _():
        s2_ref[...] = jnp.zeros_like(s2_ref)

    w_ref[...] = w0_ref[...] + jnp.dot(low_ref[...], w2_ref[...], preferred_element_type=F32)
    a_ref[...] = a0_ref[...] + jnp.dot(loa_ref[...], a2_ref[...], preferred_element_type=F32)
    g_ref[...] = jnp.dot(log_ref[...], g2_ref[...], preferred_element_type=F32)

    lane_l = lax.broadcasted_iota(jnp.int32, (L, lanes), 1)
    row_l = lax.broadcasted_iota(jnp.int32, (L, lanes), 0)
    head0 = lane_l < nh
    n2 = 2 * L
    ti = lax.broadcasted_iota(jnp.int32, (n2, n2), 0)
    si = lax.broadcasted_iota(jnp.int32, (n2, n2), 1)
    strict = si < ti
    incl = si <= ti
    sq_r = lax.broadcasted_iota(jnp.int32, (lanes, lanes), 0) // nh
    sq_c = lax.broadcasted_iota(jnp.int32, (lanes, lanes), 1) // nh
    seg_ones = jnp.where(sq_r == sq_c, 1.0, 0.0).astype(BF16)
    nt = (((1,), (1,)), ((), ()))
    tn = (((0,), (0,)), ((), ()))

    def seg_sums(xs):
        his = [x.astype(BF16) for x in xs]
        los = [(x - hi.astype(F32)).astype(BF16) for x, hi in zip(xs, his)]
        return [jnp.dot(hi, seg_ones, preferred_element_type=F32) + jnp.dot(lo, seg_ones, preferred_element_type=F32)
                for hi, lo in zip(his, los)]

    def stack(x):
        return jnp.concatenate([jnp.where(head0, x, 0.0), jnp.where(head0, 0.0, x)], axis=0)

    dot = lambda x, y: jnp.dot(x, y, preferred_element_type=F32)
    dot_nt = lambda x, y: lax.dot_general(x, y, nt, preferred_element_type=F32)
    dot_tn = lambda x, y: lax.dot_general(x, y, tn, preferred_element_type=F32)

    def wave(ps):
        idx = range(len(ps))
        ls = [pl.ds(p * lanes, lanes) for p in ps]
        r = [r_ref[:, s] for s in ls]
        v = [v_ref[:, s] for s in ls]
        logd, cl, kk0, kmod, a = [], [], [], [], []
        for i in idx:
            k_i = k_ref[:, ls[i]]
            ld = -jnp.exp(-jax.nn.softplus(-w_ref[:, ls[i]]) - 0.5)
            c = ld
            sh = 1
            while sh < L:
                c = c + jnp.where(row_l >= sh, pltpu.roll(c, sh, axis=0), 0.0)
                sh *= 2
            a_i = jax.nn.sigmoid(a_ref[:, ls[i]])
            logd.append(ld)
            cl.append(c)
            a.append(a_i)
            kk0.append(k_i * kkp_ref[:, ls[i]])
            kmod.append(k_i * (1.0 + (a_i - 1.0) * kap_ref[:, ls[i]]))
        sums = seg_sums([kk0[i] * kk0[i] for i in idx] + [r[i] * kmod[i] * rkp_ref[:, ls[i]] for i in idx])
        nrm2 = sums[:len(ps)]
        bonus = [sums[len(ps) + i] * v[i] for i in idx]
        x1, x2, vs, d_last = [], [], [], []
        for i in idx:
            kk = kk0[i] / jnp.maximum(jnp.sqrt(nrm2[i]), 1e-12)
            d_inc = jnp.exp(cl[i])
            d_inv = jnp.exp(-cl[i])
            xa = stack(kk * jnp.exp(cl[i] - logd[i]))
            xr = stack(r[i] * d_inc)
            xb = stack(kk * a[i] * d_inv)
            xk = stack(kmod[i] * d_inv)
            x1.append(jnp.concatenate([xa, xr], axis=0).astype(BF16))
            x2.append(jnp.concatenate([xb, xk], axis=0).astype(BF16))
            vs.append(stack(v[i]))
            d_last.append(d_inc[L - 1:L, :])
        s2 = [s2_ref[p] for p in ps]
        g = [dot_nt(x1[i], x2[i]) for i in idx]
        pz = [dot_nt(x1[i], s2[i].astype(BF16)) for i in idx]
        m_k = [jnp.where(strict, g[i][:n2, n2:], 0.0).astype(BF16) for i in idx]
        m_pow = [jnp.where(strict, g[i][:n2, :n2], 0.0).astype(BF16) for i in idx]
        acat = [jnp.concatenate([jnp.where(incl, g[i][n2:, :n2], 0.0), jnp.where(incl, g[i][n2:, n2:], 0.0)],
                                axis=1).astype(BF16) for i in idx]
        vsb = [vs[i].astype(BF16) for i in idx]
        rhs = [pz[i][:n2] + dot(m_k[i], vsb[i]) for i in idx]
        u = [rhs[i] - dot(m_pow[i], rhs[i].astype(BF16)) for i in idx]
        pw = 2
        while pw < L:
            m_pow = [dot(m_pow[i], m_pow[i]).astype(BF16) for i in idx]
            u = [u[i] + dot(m_pow[i], u[i].astype(BF16)) for i in idx]
            pw *= 2
        wcat = [jnp.concatenate([-u[i], vs[i]], axis=0).astype(BF16) for i in idx]
        ys = [pz[i][n2:] + dot(acat[i], wcat[i]) for i in idx]
        upd = [dot_tn(wcat[i], x2[i]) for i in idx]
        for i in idx:
            s2_ref[ps[i]] = (s2[i] + upd[i]) * d_last[i]
        y = [ys[i][:L] + ys[i][L:] for i in idx]
        mean = [z * (1.0 / nh) for z in seg_sums(y)]
        yc = [y[i] - mean[i] for i in idx]
        var = [z * (1.0 / nh) for z in seg_sums([yc[i] * yc[i] for i in idx])]
        for i in idx:
            yn = yc[i] * lax.rsqrt(var[i] + RW_GN_EPS) * lg_ref[:, ls[i]] + lb_ref[:, ls[i]]
            y_ref[:, ls[i]] = ((yn + bonus[i]) * g_ref[:, ls[i]]).astype(y_ref.dtype)

    for p0 in range(0, npairs, unroll):
        wave(list(range(p0, p0 + unroll)))

    @pl.when(ci == pl.num_programs(1) - 1)
    def _():
        for p in range(npairs):
            blk = s2_ref[p]
            sT_ref[2 * p] = blk[:nh, :nh]
            sT_ref[2 * p + 1] = blk[nh:, nh:]


def _rwkv_chunked(arrs, lows, lora_w, lora_b, params, row0, bsz, t):
    m, d = arrs[0].shape
    lanes = V7X_LANES
    chunk = min(t, 64)
    npairs = d // lanes
    assert t % chunk == 0 and row0 % chunk == 0 and d % lanes == 0 and lanes == 2 * RW_HEAD
    nc = t // chunk
    rb0 = row0 // chunk
    rowblk = lambda b, c: (rb0 + b * nc + c, 0)
    seq = pl.BlockSpec((chunk, d), rowblk)
    vec = pl.BlockSpec((1, d), lambda b, c: (0, 0))
    whole = lambda a: pl.BlockSpec(a.shape, lambda b, c: (0, 0))
    est = 4 * 2 * _nbytes((chunk, d), F32) + 3 * _nbytes((chunk, d), F32) + 3 * _nbytes((npairs, lanes, lanes), F32) \
        + 2 * sum(_nbytes(w.shape, BF16) for w in lora_w) + (8 << 20)
    kern = functools.partial(_rwkv_chunk_kernel, chunk=chunk, npairs=npairs, unroll=16)
    return pl.pallas_call(
        kern,
        grid=(bsz, nc),
        in_specs=[seq] * 3 + [pl.BlockSpec((chunk, lo.shape[1]), rowblk) for lo in lows] + [whole(w) for w in lora_w]
        + [vec] * 7,
        out_specs=[seq, pl.BlockSpec((None, 2 * npairs, RW_HEAD, RW_HEAD), lambda b, c: (b, 0, 0, 0))],
        out_shape=[jax.ShapeDtypeStruct((m, d), BF16),
                   jax.ShapeDtypeStruct((bsz, 2 * npairs, RW_HEAD, RW_HEAD), F32)],
        scratch_shapes=[pltpu.VMEM((npairs, lanes, lanes), F32)] + [pltpu.VMEM((chunk, d), F32)] * 3,
        compiler_params=_params(("parallel", "arbitrary"), est),
        name="rwkv_chunk",
    )(*arrs, *lows, *lora_w, *lora_b, *[_row(q) for q in params])


def _rwkv_layer(x, groups, states, p, wb):
    d = x.shape[1]
    h, (xr, xw, xk, xv, xa, xg) = _rw_mix(x, p['norm_g'], p['rw_mu'], states[1][1], groups)
    w_rkv = wb['rw_w_rkv']
    r = _mm([_full(xr)], [(w_rkv, (0,), 0)], n=d, out_dtype=F32, name="rw_r")
    k = _mm([_full(xk)], [(w_rkv, (1,), 0)], n=d, out_dtype=F32, name="rw_k")
    v = _mm([_full(xv)], [(w_rkv, (2,), 0)], n=d, out_dtype=F32, name="rw_v")
    lo_w = _mm([_full(xw)], [(wb['rw_w1'], (), 0)], n=wb['rw_w1'].shape[1], out_dtype=BF16,
               epilogue=lambda accs, e, ev: jnp.tanh(accs[0]), name="rw_w1")
    lo_a = _mm([_full(xa)], [(wb['rw_a1'], (), 0)], n=wb['rw_a1'].shape[1], out_dtype=BF16, name="rw_a1")
    lo_g = _mm([_full(xg)], [(wb['rw_g1'], (), 0)], n=wb['rw_g1'].shape[1], out_dtype=BF16,
               epilogue=lambda accs, e, ev: jax.nn.sigmoid(accs[0]), name="rw_g1")
    lora_w = (wb['rw_w2'], wb['rw_a2'], wb['rw_g2'])
    lora_b = (_row(p['rw_w0']), _row(p['rw_a0']))
    add_vec = lambda accs, e, ev: ev[0] + accs[0]

    y, new_states = None, []
    for (r0, b, t), st in zip(groups, states):
        sl = slice(r0, r0 + b * t)
        if st[0] is None:
            assert y is None
            y, s_fin = _rwkv_chunked((r, k, v), (lo_w, lo_a, lo_g), lora_w, lora_b,
                                     [p[name] for name in ('rw_k_k', 'rw_k_a', 'rw_r_k', 'rw_ln_g', 'rw_ln_b')], r0, b, t)
        else:
            w_pre = _mm([_full(lo_w[sl])], [(lora_w[0], (), 0)], n=d, out_dtype=F32, evecs=[lora_b[0]],
                        epilogue=add_vec, name="rw_w2")
            a_pre = _mm([_full(lo_a[sl])], [(lora_w[1], (), 0)], n=d, out_dtype=F32, evecs=[lora_b[1]],
                        epilogue=add_vec, name="rw_a2")
            g = _mm([_full(lo_g[sl])], [(lora_w[2], (), 0)], n=d, out_dtype=F32, name="rw_g2")
            y_g, s_fin = _rwkv_group(r[sl], k[sl], v[sl], w_pre, a_pre, st[0], p, b, t)
            y = lax.dynamic_update_slice(y, (y_g * g).astype(y.dtype), (r0, 0))
        new_states.append((s_fin, lax.slice(h, (r0 + t - 1, 0), (r0 + b * t, d), (t, 1))))
    x = _mm([_full(y)], [(wb['rw_w_o'], (), 0)], n=d, out_dtype=F32, e_blocks=[x],
            epilogue=lambda accs, e, ev: e[0] + accs[0], name="rw_out")
    return x, new_states


def _mlstm_kernel(q_ref, k_ref, v_ref, og_ref, gt_ref, gtt_ref, bg_ref, bgt_ref, ng_ref, c0_ref, n0_ref, m0_ref, *rest,
                  nheads, dk, dv, chunk, bb, aliased):
    if aliased:
        rest = rest[1:]
    h_ref, cT_ref, nT_ref, mT_ref, c_ref, n_ref, m_ref = rest
    ci = pl.program_id(1)

    @pl.when(ci == 0)
    def _():
        c_ref[...] = c0_ref[...]
        n_ref[...] = n0_ref[...]
        m_ref[...] = m0_ref[...]

    L = chunk
    cap = lambda z: ML_GATE_CAP * jnp.tanh(z / ML_GATE_CAP)
    gates_c_all = cap(gt_ref[...] + bg_ref[...])
    gates_r_all = cap(gtt_ref[...] + bgt_ref[...])
    row_id = lax.broadcasted_iota(jnp.int32, (L, L), 0)
    col_id = lax.broadcasted_iota(jnp.int32, (L, L), 1)
    causal = col_id <= row_id
    k_scale = dk ** -0.5

    nt = (((1,), (1,)), ((), ()))
    tn = (((0,), (0,)), ((), ()))
    chains = [(bi, hd) for bi in range(bb) for hd in range(nheads)]
    idx = range(len(chains))
    rows = [slice(bi * L, (bi + 1) * L) for bi, _ in chains]
    li_c, b_c, m_prev, dmat, inter, m_t = [], [], [], [], [], []
    for (bi, hd), rw in zip(chains, rows):
        gates_c = gates_c_all[rw, :]
        gates_r = gates_r_all[:, rw]
        li_r = gates_r[hd:hd + 1, :]
        lf_c = jax.nn.log_sigmoid(gates_c[:, nheads + hd:nheads + hd + 1])
        lf_r = jax.nn.log_sigmoid(gates_r[nheads + hd:nheads + hd + 1, :])
        bc = jnp.sum(jnp.where(causal, lf_r, 0.0), axis=1, keepdims=True)
        br = jnp.sum(jnp.where(row_id <= col_id, lf_c, 0.0), axis=0, keepdims=True)
        mp = m_ref[bi, hd][:, :1]
        dm = jnp.where(causal, bc - br + li_r, -jnp.inf)
        it = bc + mp
        li_c.append(gates_c[:, hd:hd + 1])
        b_c.append(bc)
        m_prev.append(mp)
        dmat.append(dm)
        inter.append(it)
        m_t.append(jnp.maximum(jnp.max(dm, axis=1, keepdims=True), it))

    q = [q_ref[rw, hd * dk:(hd + 1) * dk] for (_, hd), rw in zip(chains, rows)]
    k = [k_ref[rw, hd * dk:(hd + 1) * dk] * k_scale for (_, hd), rw in zip(chains, rows)]
    v = [v_ref[rw, hd * dv:(hd + 1) * dv] for (_, hd), rw in zip(chains, rows)]
    qb = [z.astype(BF16) for z in q]
    kb = [z.astype(BF16) for z in k]
    c_old = [c_ref[bi, hd] for bi, hd in chains]
    n_old = [n_ref[bi, hd] for bi, hd in chains]
    qk = [lax.dot_general(qb[i], kb[i], nt, preferred_element_type=F32) for i in idx]
    cq = [lax.dot_general(qb[i], c_old[i].astype(BF16), nt, preferred_element_type=F32) for i in idx]
    s = [qk[i] * jnp.exp(dmat[i] - m_t[i]) for i in idx]
    sv = [jnp.dot(s[i].astype(BF16), v[i].astype(BF16), preferred_element_type=F32) for i in idx]
    m_new, ws = [], []
    for i in idx:
        b_last = b_c[i][L - 1:L, :]
        ws_log = b_last - b_c[i] + li_c[i]
        mn = jnp.maximum(b_last + m_prev[i], jnp.max(ws_log, axis=0, keepdims=True))
        m_new.append(mn)
        ws.append(jnp.exp(ws_log - mn))
    upd = [lax.dot_general((v[i] * ws[i]).astype(BF16), kb[i], tn, preferred_element_type=F32) for i in idx]
    h_out = [[None] * bb for _ in range(nheads)]
    w_inter = [jnp.exp(inter[i] - m_t[i]) for i in idx]
    num = [sv[i] + w_inter[i] * cq[i] for i in idx]
    nq = [jnp.sum(q[i] * n_old[i], axis=1, keepdims=True) for i in idx]
    ssum = [jnp.sum(s[i], axis=1, keepdims=True) for i in idx]
    den = [ssum[i] + w_inter[i] * nq[i] for i in idx]
    h = [num[i] / jnp.maximum(jnp.abs(den[i]), jnp.exp(-m_t[i])) for i in idx]
    msq = [jnp.mean(h[i] * h[i], axis=1, keepdims=True) for i in idx]
    hn = [h[i] * lax.rsqrt(msq[i] + ML_NORM_EPS) for i in idx]
    dec = [jnp.exp(b_c[i][L - 1:L, :] + m_prev[i] - m_new[i]) for i in idx]
    n_add = [jnp.sum(ws[i] * k[i], axis=0, keepdims=True) for i in idx]
    for i, (bi, hd) in enumerate(chains):
        h_out[hd][bi] = hn[i] * ng_ref[:, hd * dv:(hd + 1) * dv]
        c_ref[bi, hd] = dec[i] * c_old[i] + upd[i]
        n_ref[bi, hd] = dec[i] * n_old[i] + n_add[i]
        m_ref[bi, hd] = jnp.broadcast_to(m_new[i], m_ref.shape[2:])
    for hd in range(nheads):
        cols = slice(hd * dv, (hd + 1) * dv)
        h_all = h_out[hd][0] if bb == 1 else jnp.concatenate(h_out[hd], axis=0)
        h_ref[:, cols] = (jax.nn.sigmoid(og_ref[:, cols]) * h_all).astype(h_ref.dtype)

    @pl.when(ci == pl.num_programs(1) - 1)
    def _():
        cT_ref[...] = c_ref[...]
        nT_ref[...] = n_ref[...]
        mT_ref[...] = m_ref[...]


def _mlstm_group(qkvo, gates, b_gates, norm_g, c0, n0, m0, group, h_prev_out):
    r0, bsz, t = group
    m_rows = qkvo.shape[0]
    nheads = ML_HEADS
    dk = c0.shape[-1]
    dv = c0.shape[-2]
    lanes = V7X_LANES
    chunk = min(t, lanes)
    bb = 1 if chunk >= V7X_SUBLANES else 2 * V7X_SUBLANES // chunk
    rows = bb * chunk
    assert t % chunk == 0 and bsz % bb == 0 and r0 % rows == 0
    nc = t // chunk
    nsteps = (bsz // bb) * nc
    rb0 = r0 // rows
    g_grp = gates[r0:r0 + bsz * t]
    g3t = g_grp.reshape(nsteps, rows, 2 * nheads).transpose(0, 2, 1)
    n0r = n0.astype(F32).reshape(bsz, nheads, 1, dk)
    m0r = jnp.broadcast_to(m0.astype(F32).reshape(bsz, nheads, 1, 1), (bsz, nheads, 1, lanes))
    wq, wv = nheads * dk, nheads * dv
    rowblk = lambda b, c: rb0 + b * nc + c
    seq = lambda w, cb: pl.BlockSpec((rows, w), functools.partial(lambda b, c, cb_: (rowblk(b, c), cb_), cb_=cb))
    st_c = pl.BlockSpec((bb, nheads, dv, dk), lambda b, c: (b, 0, 0, 0))
    st_n = pl.BlockSpec((bb, nheads, 1, dk), lambda b, c: (b, 0, 0, 0))
    st_m = pl.BlockSpec((bb, nheads, 1, lanes), lambda b, c: (b, 0, 0, 0))
    est = 2 * 3 * _nbytes((rows, 2 * wq + 2 * wv), F32) + 5 * bb * _nbytes((nheads, dv, dk), F32) + (8 << 20)
    aliased = h_prev_out is not None
    kern = functools.partial(_mlstm_kernel, nheads=nheads, dk=dk, dv=dv, chunk=chunk, bb=bb, aliased=aliased)
    in_specs = [seq(wq, 0), seq(wq, 1), seq(wv, wq * 2 // wv), seq(wv, wq * 2 // wv + 1),
                pl.BlockSpec((rows, 2 * nheads), lambda b, c: (rowblk(b, c), 0)),
                pl.BlockSpec((None, 2 * nheads, rows), lambda b, c: (b * nc + c, 0, 0)),
                pl.BlockSpec((1, 2 * nheads), lambda b, c: (0, 0)),
                pl.BlockSpec((2 * nheads, 1), lambda b, c: (0, 0)),
                pl.BlockSpec((1, wv), lambda b, c: (0, 0)),
                st_c, st_n, st_m]
    args = [qkvo, qkvo, qkvo, qkvo, gates, g3t, _row(b_gates), b_gates.reshape(-1, 1), _row(norm_g),
            c0.astype(F32), n0r, m0r]
    io_alias = {}
    if aliased:
        in_specs.append(pl.BlockSpec(memory_space=pl.ANY))
        args.append(h_prev_out)
        io_alias = {len(args) - 1: 0}
    h, c_fin, n_fin, m_fin = pl.pallas_call(
        kern,
        grid=(bsz // bb, nc),
        in_specs=in_specs,
        out_specs=[pl.BlockSpec((rows, wv), lambda b, c: (rowblk(b, c), 0)), st_c, st_n, st_m],
        out_shape=[jax.ShapeDtypeStruct((m_rows, wv), BF16),
                   jax.ShapeDtypeStruct((bsz, nheads, dv, dk), F32),
                   jax.ShapeDtypeStruct((bsz, nheads, 1, dk), F32),
                   jax.ShapeDtypeStruct((bsz, nheads, 1, lanes), F32)],
        scratch_shapes=[pltpu.VMEM((bb, nheads, dv, dk), F32), pltpu.VMEM((bb, nheads, 1, dk), F32),
                        pltpu.VMEM((bb, nheads, 1, lanes), F32)],
        input_output_aliases=io_alias,
        compiler_params=_params(("parallel", "arbitrary"), est),
        name="mlstm_chunk",
    )(*args)
    return h, c_fin, n_fin.reshape(bsz, nheads, dk), m_fin[:, :, 0, 0]


def _mlstm_layer(x, groups, states, p, wb):
    d = x.shape[1]
    nheads = ML_HEADS
    dv = d // nheads
    dk = dv // 2
    n_main = 2 * nheads * dk + 2 * nheads * dv
    xn = _rmsnorm(x, p['norm_g'], BF16)
    qkvo = _mm([_full(xn)], [(wb['ml_w_main'], (), 0)], n=n_main, out_dtype=F32, name="ml_in")
    gates = _mm([_full(xn)], [(wb['ml_w_gates'], (), 0)], n=2 * nheads, out_dtype=F32, name="ml_gates")
    h, new_states = None, []
    for grp, st in zip(groups, states):
        h, c_fin, n_fin, m_fin = _mlstm_group(qkvo, gates, p['ml_b_gates'], p['ml_norm_g'], st[0], st[1], st[2], grp, h)
        new_states.append((c_fin, n_fin, m_fin))
    x = _mm([_full(h)], [(wb['ml_w_o'], (), 0)], n=d, out_dtype=F32,
            e_blocks=[x], epilogue=lambda accs, e, ev: e[0] + accs[0], name="ml_out")
    return x, new_states


def _dwconv_kernel(u_ref, buf_ref, w_ref, *rest, ksize, t, bb, rows, halo, aliased):
    if aliased:
        rest = rest[1:]
    y_ref, ext_ref = rest
    w = w_ref[...]
    sub = V7X_SUBLANES
    lead = halo - (ksize - 1)
    n_tiles = t // rows
    taps_by_shift = [[j for j in range(ksize) if (j + lead) % sub == r] for r in range(sub)]
    ext_ref[pl.ds(halo + t, sub), :] = jnp.zeros((sub, ext_ref.shape[1]), F32)

    for b in range(bb):
        ext_ref[pl.ds(lead, ksize - 1), :] = buf_ref[b]
        ext_ref[pl.ds(halo, t), :] = u_ref[pl.ds(b * t, t), :]

        def tile(ti, carry):
            base = ti * rows if isinstance(ti, int) else pl.multiple_of(ti * rows, sub)
            acc = None
            for r in range(sub):
                part = None
                for j in taps_by_shift[r]:
                    q8 = (j + lead) - r
                    term = w[j:j + 1, :] * ext_ref[pl.ds(base + q8, rows + sub), :]
                    part = term if part is None else part + term
                if part is not None:
                    shifted = part[r:r + rows, :]
                    acc = shifted if acc is None else acc + shifted
            y_ref[pl.ds(b * t + base, rows), :] = acc
            return carry

        if n_tiles == 1:
            tile(0, 0)
        else:
            lax.fori_loop(0, n_tiles, tile, 0)


def _dwconv_group(u, buf, w, group, y_prev_out):
    r0, bsz, t = group
    m_rows, c = u.shape
    ksize = w.shape[0]
    sub = V7X_SUBLANES
    halo = _round_up(ksize - 1, sub)
    bc = 256
    bb = 1 if t >= 64 else min(bsz, 32)
    rows = min(t, 64)
    blk = bb * t
    assert bsz % bb == 0 and t % rows == 0 and c % bc == 0 and r0 % blk == 0 and blk % sub == 0
    rb0 = r0 // blk
    aliased = y_prev_out is not None
    est = 4 * _nbytes((blk, bc), F32) + _nbytes((halo + t + sub, bc), F32) + (4 << 20)
    kern = functools.partial(_dwconv_kernel, ksize=ksize, t=t, bb=bb, rows=rows, halo=halo, aliased=aliased)
    in_specs = [pl.BlockSpec((blk, bc), lambda b, ci: (rb0 + b, ci)),
                pl.BlockSpec((bb, ksize - 1, bc), lambda b, ci: (b, 0, ci)),
                pl.BlockSpec((ksize, bc), lambda b, ci: (0, ci))]
    args = [u, buf.astype(F32), w]
    io_alias = {}
    if aliased:
        in_specs.append(pl.BlockSpec(memory_space=pl.ANY))
        args.append(y_prev_out)
        io_alias = {3: 0}
    return pl.pallas_call(
        kern,
        grid=(bsz // bb, c // bc),
        in_specs=in_specs,
        out_specs=pl.BlockSpec((blk, bc), lambda b, ci: (rb0 + b, ci)),
        out_shape=jax.ShapeDtypeStruct((m_rows, c), F32),
        scratch_shapes=[pltpu.VMEM((halo + t + sub, bc), F32)],
        input_output_aliases=io_alias,
        compiler_params=_params(("parallel", "parallel"), est),
        name="dwconv",
    )(*args)


def _conv_groups(u, w_dw, groups, bufs):
    y, new_bufs = None, []
    c = u.shape[1]
    km1 = w_dw.shape[0] - 1
    for (r0, b, t), buf in zip(groups, bufs):
        y = _dwconv_group(u, buf, w_dw, (r0, b, t), y)
        if t >= km1:
            tail = r0 + t - km1 + jnp.arange(b)[:, None] * t + jnp.arange(km1)[None, :]
            new_bufs.append(u[tail])
        else:
            new_bufs.append(jnp.concatenate([buf.astype(F32)[:, t:], u[r0:r0 + b * t].reshape(b, t, c)], axis=1))
    return y, new_bufs


def _conformer_layer(x, groups, states, p, wb):
    d = x.shape[1]
    b1 = _row(p['cf_b_pw1'])
    w1 = wb['cf_w_pw1']
    u = _mm([_full(x)], [(w1, (), 0), (w1, (), d)], n=d, out_dtype=F32, avecs=[_row(p['norm_g'])],
            prologue=lambda a, v, pw: _rms(a[0]) * v[0], evecs=[b1[:, :d], b1[:, d:]],
            epilogue=lambda accs, e, ev: (accs[0] + ev[0]) * jax.nn.sigmoid(accs[1] + ev[1]), name="cf_pw1")
    y, new_bufs = _conv_groups(u, p['cf_w_dw'], groups, [st[0] for st in states])

    def ln_silu(a, v, pw):
        z = a[0] + v[0]
        mu = jnp.mean(z, axis=-1, keepdims=True)
        var = jnp.mean(jnp.square(z - mu), axis=-1, keepdims=True)
        z = (z - mu) * lax.rsqrt(var + CF_LN_EPS) * v[1] + v[2]
        return z * jax.nn.sigmoid(z)

    x = _mm([_full(y)], [(wb['cf_w_pw2'], (), 0)], n=d, out_dtype=F32, prologue=ln_silu,
            avecs=[_row(p['cf_b_dw']), _row(p['cf_ln_g']), _row(p['cf_ln_b'])],
            e_blocks=[x], evecs=[_row(p['cf_b_pw2'])],
            epilogue=lambda accs, e, ev: e[0] + (accs[0] + ev[0]), name="cf_pw2")
    return x, [(nb,) for nb in new_bufs]


def _sconv_layer(x, groups, states, p, wb):
    d = x.shape[1]
    xn = _rmsnorm(x, p['norm_g'], BF16)
    w_in = wb['sc_w_in']
    b_gate = _mm([_full(xn)], [(w_in, (), 0)], n=d, out_dtype=F32, name="sc_b")
    ch = _mm([_full(xn)], [(w_in, (), d), (w_in, (), 2 * d)], n=d, out_dtype=F32,
             epilogue=lambda accs, e, ev: accs[0] * accs[1], name="sc_ch")
    y, new_bufs = _conv_groups(ch, p['sc_w_dw'], groups, [st[0] for st in states])
    x = _mm([_full(b_gate), _full(y)], [(wb['sc_w_out'], (), 0)], n=d, out_dtype=F32,
            prologue=lambda a, v, pw: a[0] * a[1], e_blocks=[x],
            epilogue=lambda accs, e, ev: e[0] + accs[0], name="sc_out")
    return x, [(nb,) for nb in new_bufs]


def kernel(x_prompt, x_sample, state_rwkv_wkv, state_rwkv_shift, state_mlstm_C, state_mlstm_n, state_mlstm_m,
           state_conf_conv, state_sconv_conv, norm_g, ffn_w_gu, ffn_w_down, final_norm_g,
           rw_mu, rw_w_rkv, rw_w0, rw_w1, rw_w2, rw_a0, rw_a1, rw_a2, rw_g1, rw_g2, rw_k_k, rw_k_a, rw_r_k,
           rw_ln_g, rw_ln_b, rw_w_o, ml_w_in, ml_b_gates, ml_norm_g, ml_w_o,
           cf_w_pw1, cf_b_pw1, cf_w_dw, cf_b_dw, cf_ln_g, cf_ln_b, cf_w_pw2, cf_b_pw2,
           sc_w_in, sc_w_dw, sc_w_out):
    bp, tp, d = x_prompt.shape
    bs, ts, _ = x_sample.shape
    depth = norm_g.shape[0]
    n_mixers = 4
    groups = [(0, bp, tp), (bp * tp, bs, ts)]
    x = jnp.concatenate([x_prompt.reshape(bp * tp, d), x_sample.reshape(bs * ts, d)], axis=0).astype(F32)

    dv = d // ML_HEADS
    dk = dv // 2
    cf_k = cf_w_dw.shape[1]
    sc_k = sc_w_dw.shape[1]
    zeros = lambda *s: jnp.zeros(s, F32)

    out_states = {name: ([], []) for name in ('wkv', 'shift', 'C', 'n', 'm', 'cf', 'sc')}
    for i in range(depth):
        j = i // n_mixers
        kind = i % n_mixers
        x = _ffn(x, norm_g[i, 0], ffn_w_gu, ffn_w_down, (i, 0))
        if kind == 0:
            p = dict(norm_g=norm_g[i, 1], rw_mu=rw_mu[j], rw_w0=rw_w0[j], rw_a0=rw_a0[j], rw_k_k=rw_k_k[j],
                     rw_k_a=rw_k_a[j], rw_r_k=rw_r_k[j], rw_ln_g=rw_ln_g[j], rw_ln_b=rw_ln_b[j])
            wb = dict(rw_w_rkv=rw_w_rkv[j].astype(BF16), rw_w1=rw_w1[j].astype(BF16), rw_w2=rw_w2[j].astype(BF16),
                      rw_a1=rw_a1[j].astype(BF16), rw_a2=rw_a2[j].astype(BF16), rw_g1=rw_g1[j].astype(BF16),
                      rw_g2=rw_g2[j].astype(BF16), rw_w_o=rw_w_o[j].astype(BF16))
            states = [(None, zeros(bp, d)), (state_rwkv_wkv[j], state_rwkv_shift[j].astype(F32))]
            x, new = _rwkv_layer(x, groups, states, p, wb)
            for gi in range(2):
                out_states['wkv'][gi].append(new[gi][0])
                out_states['shift'][gi].append(new[gi][1])
        elif kind == 1:
            p = dict(norm_g=norm_g[i, 1], ml_b_gates=ml_b_gates[j], ml_norm_g=ml_norm_g[j])
            n_main = 2 * ML_HEADS * dk + 2 * ML_HEADS * dv
            wb = dict(ml_w_main=ml_w_in[j][:, :n_main].astype(BF16), ml_w_gates=ml_w_in[j][:, n_main:].astype(BF16),
                      ml_w_o=ml_w_o[j].astype(BF16))
            states = [(zeros(bp, ML_HEADS, dv, dk), zeros(bp, ML_HEADS, dk), zeros(bp, ML_HEADS)),
                      (state_mlstm_C[j], state_mlstm_n[j], state_mlstm_m[j])]
            x, new = _mlstm_layer(x, groups, states, p, wb)
            for gi in range(2):
                out_states['C'][gi].append(new[gi][0])
                out_states['n'][gi].append(new[gi][1])
                out_states['m'][gi].append(new[gi][2])
        elif kind == 2:
            p = dict(norm_g=norm_g[i, 1], cf_b_pw1=cf_b_pw1[j], cf_w_dw=cf_w_dw[j], cf_b_dw=cf_b_dw[j],
                     cf_ln_g=cf_ln_g[j], cf_ln_b=cf_ln_b[j], cf_b_pw2=cf_b_pw2[j])
            wb = dict(cf_w_pw1=cf_w_pw1[j].astype(BF16), cf_w_pw2=cf_w_pw2[j].astype(BF16))
            states = [(zeros(bp, cf_k - 1, d),), (state_conf_conv[j],)]
            x, new = _conformer_layer(x, groups, states, p, wb)
            for gi in range(2):
                out_states['cf'][gi].append(new[gi][0])
        else:
            p = dict(norm_g=norm_g[i, 1], sc_w_dw=sc_w_dw[j])
            wb = dict(sc_w_in=sc_w_in[j].astype(BF16), sc_w_out=sc_w_out[j].astype(BF16))
            states = [(zeros(bp, sc_k - 1, d),), (state_sconv_conv[j],)]
            x, new = _sconv_layer(x, groups, states, p, wb)
            for gi in range(2):
                out_states['sc'][gi].append(new[gi][0])
        x = _ffn(x, norm_g[i, 2], ffn_w_gu, ffn_w_down, (i, 1))

    y_prompt = _rmsnorm(x, final_norm_g, x_prompt.dtype, 0, bp * tp).reshape(bp, tp, d)
    y_sample = _rmsnorm(x, final_norm_g, x_sample.dtype, bp * tp, bs * ts).reshape(bs, ts, d)
    order = ('wkv', 'shift', 'C', 'n', 'm', 'cf', 'sc')
    dt = x_prompt.dtype
    prompt_out = tuple(jnp.stack(out_states[name][0]).astype(dt) for name in order)
    sample_out = tuple(jnp.stack(out_states[name][1]).astype(st.dtype) for name, st in
                       zip(order, (state_rwkv_wkv, state_rwkv_shift, state_mlstm_C, state_mlstm_n, state_mlstm_m,
                                   state_conf_conv, state_sconv_conv)))
    return (y_prompt, y_sample) + prompt_out + sample_out
```
